```python
import math
import jax
import jax.numpy as jnp
from jax import lax
import numpy as np

D_MODEL = 4096
BATCH = 4
SEQ = 2048
DEPTH = 1

A_HEAD_DIM = 128
A_HEADS = D_MODEL // (2 * A_HEAD_DIM)
IDX_HEADS = 32
IDX_HEAD_DIM = 128
IDX_ROPE_DIM = 64
TOPK_MAX = 256
V_HEAD_DIM = 128
B_HEADS = D_MODEL // (2 * V_HEAD_DIM)
Q_LORA_RANK = 1024
KV_LORA_RANK = 512
QK_NOPE_DIM = 128
QK_ROPE_DIM = 64
D_FF = 11008
CONV_WIDTH = 3
REL_BUCKETS = 32
REL_MAX_DIST = 128
ROPE_THETA = 10000.0
Q_BLOCK = 128
LN_EPS = 1e-5
RMS_EPS = 1e-6
NEG_INF = -1e30
DEEPNORM_ALPHA = (2 * DEPTH) ** 0.25
DEEPNORM_BETA = (8 * DEPTH) ** -0.25
MIX_WIDTH = A_HEADS * A_HEAD_DIM + B_HEADS * V_HEAD_DIM
IN_SPLITS = (A_HEADS * A_HEAD_DIM, A_HEAD_DIM, A_HEAD_DIM,
             IDX_HEADS * IDX_HEAD_DIM, IDX_HEAD_DIM, IDX_HEADS,
             Q_LORA_RANK, KV_LORA_RANK, QK_ROPE_DIM)
IN_WIDTH = sum(IN_SPLITS)
SPLIT_POINTS = [int(v) for v in np.cumsum(IN_SPLITS)[:-1]]

kernel_name = 'hybrid_dsa_mla_convffn_layer'


def layer_norm(x, g, b):
    xf = x.astype(jnp.float32)
    mu = jnp.mean(xf, axis=-1, keepdims=True)
    var = jnp.mean(jnp.square(xf - mu), axis=-1, keepdims=True)
    y = (xf - mu) * lax.rsqrt(var + LN_EPS)
    return (y * g.astype(jnp.float32) + b.astype(jnp.float32)).astype(x.dtype)


def rms_norm(x, g):
    xf = x.astype(jnp.float32)
    y = xf * lax.rsqrt(jnp.mean(jnp.square(xf), axis=-1, keepdims=True) + RMS_EPS)
    return (y * g.astype(jnp.float32)).astype(x.dtype)


def rope(x, pos):
    half = x.shape[-1] // 2
    freqs = ROPE_THETA ** (-jnp.arange(half, dtype=jnp.float32) / half)
    ang = pos.astype(jnp.float32)[:, :, None] * freqs
    cos = jnp.cos(ang)[:, :, None, :].astype(x.dtype)
    sin = jnp.sin(ang)[:, :, None, :].astype(x.dtype)
    x1, x2 = x[..., :half], x[..., half:]
    return jnp.concatenate([x1 * cos - x2 * sin, x1 * sin + x2 * cos], axis=-1)


def rel_bucket(rel):
    n = jnp.maximum(rel, 0)
    max_exact = REL_BUCKETS // 2
    nf = jnp.maximum(n, 1).astype(jnp.float32)
    large = max_exact + (jnp.log(nf / max_exact) / math.log(REL_MAX_DIST / max_exact)
                         * (REL_BUCKETS - max_exact)).astype(jnp.int32)
    large = jnp.minimum(large, REL_BUCKETS - 1)
    return jnp.where(n < max_exact, n, large)


def dsa_attention(q, k, v, iq, ik, iw, pos, rel_bias):
    B_, S = q.shape[0], q.shape[1]
    topk = min(TOPK_MAX, S // 4)
    nope = IDX_HEAD_DIM - IDX_ROPE_DIM
    iq = jnp.concatenate([iq[..., :nope], rope(iq[..., nope:], pos)], axis=-1)
    ik_h = ik[:, :, None, :]
    ik = jnp.concatenate([ik_h[..., :nope], rope(ik_h[..., nope:], pos)], axis=-1)[:, :, 0, :]
    iw = iw * (IDX_HEADS ** -0.5)
    idx_scale = IDX_HEAD_DIM ** -0.5
    att_scale = A_HEAD_DIM ** -0.5
    key_idx = jnp.arange(S)
    gather = jax.vmap(lambda arr, ii: arr[ii])

    def block(i):
        t0 = i * Q_BLOCK
        qb = lax.dynamic_slice_in_dim(q, t0, Q_BLOCK, axis=1)
        iqb = lax.dynamic_slice_in_dim(iq, t0, Q_BLOCK, axis=1)
        iwb = lax.dynamic_slice_in_dim(iw, t0, Q_BLOCK, axis=1)
        posb = lax.dynamic_slice_in_dim(pos, t0, Q_BLOCK, axis=1)
        tq = t0 + jnp.arange(Q_BLOCK)
        causal = key_idx[None, :] <= tq[:, None]
        dots = jnp.einsum('bthd,bsd->bths', iqb, ik, preferred_element_type=jnp.float32) * idx_scale
        score = jnp.einsum('bths,bth->bts', jax.nn.relu(dots), iwb.astype(jnp.float32))
        score = jnp.where(causal[None], score, NEG_INF)
        _, sel = lax.top_k(score, topk)
        valid = sel <= tq[None, :, None]
        k_sel = gather(k, sel)
        v_sel = gather(v, sel)
        pos_sel = gather(pos, sel)
        bias = rel_bias[rel_bucket(posb[:, :, None] - pos_sel)]
        logits = jnp.einsum('bthd,btkd->bhtk', qb, k_sel, preferred_element_type=jnp.float32) * att_scale
        logits = logits + jnp.transpose(bias, (0, 3, 1, 2)).astype(jnp.float32)
        logits = jnp.where(valid[:, None], logits, NEG_INF)
        p = jax.nn.softmax(logits, axis=-1).astype(v.dtype)
        return jnp.einsum('bhtk,btkd->bthd', p, v_sel)

    out = lax.map(block, jnp.arange(S // Q_BLOCK))
    return jnp.moveaxis(out, 0, 1).reshape(B_, S, A_HEADS * A_HEAD_DIM)


def mla_attention(q_lat, kv_lat, k_rope, pos, q_norm_g, w_uq, kv_norm_g, w_ukv):
    B_, S = q_lat.shape[0], q_lat.shape[1]
    q = (rms_norm(q_lat, q_norm_g) @ w_uq).reshape(B_, S, B_HEADS, QK_NOPE_DIM + QK_ROPE_DIM)
    q = jnp.concatenate([q[..., :QK_NOPE_DIM], rope(q[..., QK_NOPE_DIM:], pos)], axis=-1)
    kv = (rms_norm(kv_lat, kv_norm_g) @ w_ukv).reshape(B_, S, B_HEADS, QK_NOPE_DIM + V_HEAD_DIM)
    k_r = jnp.broadcast_to(rope(k_rope[:, :, None, :], pos), (B_, S, B_HEADS, QK_ROPE_DIM))
    k = jnp.concatenate([kv[..., :QK_NOPE_DIM], k_r], axis=-1)
    v = kv[..., QK_NOPE_DIM:]
    scale = (QK_NOPE_DIM + QK_ROPE_DIM) ** -0.5
    key_idx = jnp.arange(S)

    def block(i):
        t0 = i * Q_BLOCK
        qb = lax.dynamic_slice_in_dim(q, t0, Q_BLOCK, axis=1)
        tq = t0 + jnp.arange(Q_BLOCK)
        logits = jnp.einsum('bthd,bshd->bhts', qb, k, preferred_element_type=jnp.float32) * scale
        logits = jnp.where((key_idx[None, :] <= tq[:, None])[None, None], logits, NEG_INF)
        p = jax.nn.softmax(logits, axis=-1).astype(v.dtype)
        return jnp.einsum('bhts,bshd->bthd', p, v)

    out = lax.map(block, jnp.arange(S // Q_BLOCK))
    return jnp.moveaxis(out, 0, 1).reshape(B_, S, B_HEADS * V_HEAD_DIM)


def causal_dwconv(h, w, b):
    S = h.shape[1]
    hp = jnp.pad(h, ((0, 0), (CONV_WIDTH - 1, 0), (0, 0)))
    out = b
    for j in range(CONV_WIDTH):
        out = out + w[j] * hp[:, j:j + S]
    return out


def conv_glu_ffn(u, w_gate, w_up, conv_w, conv_b, w_down):
    g = causal_dwconv(u @ w_gate, conv_w, conv_b)
    return (jax.nn.silu(g) * (u @ w_up)) @ w_down


def setup_inputs(seed: int = 0) -> dict:
    key = jax.random.key(seed)
    ks = jax.random.split(key, 24)
    f32 = jnp.float32
    L = DEPTH

    def nrm(k, shape, scale):
        return jax.random.normal(k, shape, f32) * scale

    x = nrm(ks[0], (BATCH, SEQ, D_MODEL), 1.0)
    c = nrm(ks[1], (BATCH, D_MODEL), 1.0)
    positions = (jax.random.randint(ks[2], (BATCH, 1), 0, 1024, dtype=jnp.int32)
                 + jnp.arange(SEQ, dtype=jnp.int32)[None, :])
    w_ada = nrm(ks[3], (L, D_MODEL, 6 * D_MODEL), 0.1 * D_MODEL ** -0.5)
    b_ada = nrm(ks[4], (L, 6 * D_MODEL), 0.01)
    col_scale = np.ones((IN_WIDTH,), np.float32)
    v_off = IN_SPLITS[0] + IN_SPLITS[1]
    col_scale[v_off:v_off + A_HEAD_DIM] = DEEPNORM_BETA
    w_in = nrm(ks[5], (L, D_MODEL, IN_WIDTH), D_MODEL ** -0.5) * jnp.asarray(col_scale)
    rel_bias = nrm(ks[6], (REL_BUCKETS, A_HEADS), 0.5)
    q_norm_g = 1.0 + nrm(ks[7], (L, Q_LORA_RANK), 0.02)
    w_uq = nrm(ks[8], (L, Q_LORA_RANK, B_HEADS * (QK_NOPE_DIM + QK_ROPE_DIM)), Q_LORA_RANK ** -0.5)
    kv_norm_g = 1.0 + nrm(ks[9], (L, KV_LORA_RANK), 0.02)
    w_uk = nrm(ks[10], (L, KV_LORA_RANK, B_HEADS, QK_NOPE_DIM), KV_LORA_RANK ** -0.5)
    w_uv = nrm(ks[11], (L, KV_LORA_RANK, B_HEADS, V_HEAD_DIM), DEEPNORM_BETA * KV_LORA_RANK ** -0.5)
    w_ukv = jnp.concatenate([w_uk, w_uv], axis=-1).reshape(L, KV_LORA_RANK, B_HEADS * (QK_NOPE_DIM + V_HEAD_DIM))
    w_o = nrm(ks[12], (L, MIX_WIDTH, D_MODEL), DEEPNORM_BETA * MIX_WIDTH ** -0.5)
    ln1_g = 1.0 + nrm(ks[13], (L, D_MODEL), 0.02)
    ln1_b = nrm(ks[14], (L, D_MODEL), 0.02)
    w_gate = nrm(ks[15], (L, D_MODEL, D_FF), D_MODEL ** -0.5)
    w_up = nrm(ks[16], (L, D_MODEL, D_FF), DEEPNORM_BETA * D_MODEL ** -0.5)
    conv_w = nrm(ks[17], (L, CONV_WIDTH, D_FF), CONV_WIDTH ** -0.5)
    conv_b = nrm(ks[18], (L, D_FF), 0.02)
    w_down = nrm(ks[19], (L, D_FF, D_MODEL), DEEPNORM_BETA * D_FF ** -0.5)
    ln2_g = 1.0 + nrm(ks[20], (L, D_MODEL), 0.02)
    ln2_b = nrm(ks[21], (L, D_MODEL), 0.02)
    return {'x': x, 'c': c, 'positions': positions, 'w_ada': w_ada, 'b_ada': b_ada,
            'w_in': w_in, 'rel_bias': rel_bias, 'q_norm_g': q_norm_g, 'w_uq': w_uq,
            'kv_norm_g': kv_norm_g, 'w_ukv': w_ukv, 'w_o': w_o, 'ln1_g': ln1_g, 'ln1_b': ln1_b,
            'w_gate': w_gate, 'w_up': w_up, 'conv_w': conv_w, 'conv_b': conv_b,
            'w_down': w_down, 'ln2_g': ln2_g, 'ln2_b': ln2_b}


def reference(x, c, positions, w_ada, b_ada, w_in, rel_bias, q_norm_g, w_uq, kv_norm_g, w_ukv,
              w_o, ln1_g, ln1_b, w_gate, w_up, conv_w, conv_b, w_down, ln2_g, ln2_b):
    B_, S = x.shape[0], x.shape[1]
    c_act = jax.nn.silu(c)
    for l in range(DEPTH):
        mod = (c_act @ w_ada[l] + b_ada[l])[:, None, :]
        sh_a, sc_a, g_a, sh_f, sc_f, g_f = jnp.split(mod, 6, axis=-1)
        u = x * (1 + sc_a) + sh_a
        a_q, a_k, a_v, i_q, i_k, i_w, b_ql, b_kvl, b_kr = jnp.split(u @ w_in[l], SPLIT_POINTS, axis=-1)
        y_a = dsa_attention(a_q.reshape(B_, S, A_HEADS, A_HEAD_DIM), a_k, a_v,
                            i_q.reshape(B_, S, IDX_HEADS, IDX_HEAD_DIM), i_k, i_w,
                            positions, rel_bias)
        y_b = mla_attention(b_ql, b_kvl, b_kr, positions, q_norm_g[l], w_uq[l], kv_norm_g[l], w_ukv[l])
        mix = jnp.concatenate([y_a, y_b], axis=-1) @ w_o[l]
        x = layer_norm(DEEPNORM_ALPHA * x + (1 + g_a) * mix, ln1_g[l], ln1_b[l])
        u = x * (1 + sc_f) + sh_f
        y = conv_glu_ffn(u, w_gate[l], w_up[l], conv_w[l], conv_b[l], w_down[l])
        x = layer_norm(DEEPNORM_ALPHA * x + (1 + g_f) * y, ln2_g[l], ln2_b[l])
    return x
```

```python
import functools
import math

import jax
import jax.numpy as jnp
import numpy as np
from jax import lax
from jax.experimental import pallas as pl
from jax.experimental.pallas import tpu as pltpu

F32 = jnp.float32
BF16 = jnp.bfloat16

D_MODEL = 4096
A_HEAD_DIM = 128
A_HEADS = 16
IDX_HEADS = 32
IDX_HEAD_DIM = 128
IDX_ROPE_DIM = 64
TOPK_MAX = 256
V_HEAD_DIM = 128
B_HEADS = 16
Q_LORA_RANK = 1024
KV_LORA_RANK = 512
QK_NOPE_DIM = 128
QK_ROPE_DIM = 64
D_FF = 11008
CONV_WIDTH = 3
REL_BUCKETS = 32
REL_MAX_DIST = 128
ROPE_THETA = 10000.0
Q_BLOCK = 128
LN_EPS = 1e-5
RMS_EPS = 1e-6
NEG_INF = -1e30

LANES = 128
SUBLANES = 8
VMEM_LIMIT_BYTES = 56 * 1024 * 1024

P_WIDTH = 8192
COL_IQ = 0
COL_AQ = 4096
COL_QL = 6144
COL_KVL = 7168
COL_AK = 7680
COL_AV = 7808
COL_IK = 7936
COL_MISC = 8064
MISC_IW = 64

_NT = (((1,), (1,)), ((), ()))


def _cparams(n_axes):
    return pltpu.CompilerParams(
        dimension_semantics=("arbitrary",) * n_axes,
        vmem_limit_bytes=VMEM_LIMIT_BYTES)


def _ada_body(c_ref, w_ref, b_ref, o_ref):
    c = c_ref[...]
    act = (c / (1.0 + jnp.exp(-c))).astype(BF16)
    o_ref[...] = jnp.dot(act, w_ref[...].astype(BF16),
                         preferred_element_type=F32) + b_ref[...]


def _ada(c8, w, b):
    d, n = w.shape
    tn = 512
    return pl.pallas_call(
        _ada_body,
        grid=(n // tn,),
        in_specs=[pl.BlockSpec((SUBLANES, d), lambda j: (0, 0)),
                  pl.BlockSpec((d, tn), lambda j: (0, j)),
                  pl.BlockSpec((1, tn), lambda j: (0, j))],
        out_specs=pl.BlockSpec((SUBLANES, tn), lambda j: (0, j)),
        out_shape=jax.ShapeDtypeStruct((SUBLANES, n), F32),
        compiler_params=_cparams(1),
        name="ada_proj",
    )(c8, w, b.reshape(1, n))


def _modulate_body(x_ref, sh_ref, sc_ref, o_ref):
    o_ref[...] = (x_ref[...] * (1.0 + sc_ref[0]) + sh_ref[0]).astype(BF16)


def _modulate(x2, mod3, seq, shift_idx, scale_idx):
    t, d = x2.shape
    tm = 256
    per_seq = seq // tm
    return pl.pallas_call(
        _modulate_body,
        grid=(t // tm,),
        in_specs=[pl.BlockSpec((tm, d), lambda i: (i, 0)),
                  pl.BlockSpec((1, 1, d), lambda i: (6 * (i // per_seq) + shift_idx, 0, 0)),
                  pl.BlockSpec((1, 1, d), lambda i: (6 * (i // per_seq) + scale_idx, 0, 0))],
        out_specs=pl.BlockSpec((tm, d), lambda i: (i, 0)),
        out_shape=jax.ShapeDtypeStruct((t, d), BF16),
        compiler_params=_cparams(1),
        name="modulate",
    )(x2, mod3, mod3)


def _mm_body(a_ref, w_ref, o_ref):
    o_ref[...] = jnp.dot(a_ref[...], w_ref[...].astype(BF16),
                         preferred_element_type=F32).astype(o_ref.dtype)


def _matmul(a, w, out_dtype, tm, tn, name):
    m, k = a.shape
    n = w.shape[1]
    return pl.pallas_call(
        _mm_body,
        grid=(m // tm, n // tn),
        in_specs=[pl.BlockSpec((tm, k), lambda i, j: (i, 0)),
                  pl.BlockSpec((k, tn), lambda i, j: (0, j))],
        out_specs=pl.BlockSpec((tm, tn), lambda i, j: (i, j)),
        out_shape=jax.ShapeDtypeStruct((m, n), out_dtype),
        compiler_params=_cparams(2),
        name=name,
    )(a, w)


def _mm2_body(a0_ref, a1_ref, w0_ref, w1_ref, o_ref):
    acc = jnp.dot(a0_ref[...], w0_ref[...].astype(BF16), preferred_element_type=F32)
    acc = acc + jnp.dot(a1_ref[...], w1_ref[...].astype(BF16), preferred_element_type=F32)
    o_ref[...] = acc


def _matmul_concat(a0, a1, w, tm, tn, name):
    m, k0 = a0.shape
    k1 = a1.shape[1]
    assert k0 == k1
    n = w.shape[1]
    return pl.pallas_call(
        _mm2_body,
        grid=(m // tm, n // tn),
        in_specs=[pl.BlockSpec((tm, k0), lambda i, j: (i, 0)),
                  pl.BlockSpec((tm, k1), lambda i, j: (i, 0)),
                  pl.BlockSpec((k0, tn), lambda i, j: (0, j)),
                  pl.BlockSpec((k1, tn), lambda i, j: (1, j))],
        out_specs=pl.BlockSpec((tm, tn), lambda i, j: (i, j)),
        out_shape=jax.ShapeDtypeStruct((m, n), F32),
        compiler_params=_cparams(2),
        name=name,
    )(a0, a1, w, w)


def _rms_body(x_ref, g_ref, o_ref):
    x = x_ref[...]
    ms = jnp.mean(x * x, axis=-1, keepdims=True)
    o_ref[...] = (x * lax.rsqrt(ms + RMS_EPS) * g_ref[...]).astype(BF16)


def _rmsnorm_cols(p, col, width, gain, name):
    t = p.shape[0]
    tm = 512
    cb = col // width
    return pl.pallas_call(
        _rms_body,
        grid=(t // tm,),
        in_specs=[pl.BlockSpec((tm, width), lambda i: (i, cb)),
                  pl.BlockSpec((1, width), lambda i: (0, 0))],
        out_specs=pl.BlockSpec((tm, width), lambda i: (i, 0)),
        out_shape=jax.ShapeDtypeStruct((t, width), BF16),
        compiler_params=_cparams(1),
        name=name,
    )(p, gain.reshape(1, width))


def _resln_body(x_ref, y_ref, gate_ref, lg_ref, lb_ref, *rest, alpha, with_mod):
    z = alpha * x_ref[...] + (1.0 + gate_ref[0]) * y_ref[...]
    mu = jnp.mean(z, axis=-1, keepdims=True)
    zc = z - mu
    var = jnp.mean(zc * zc, axis=-1, keepdims=True)
    out = zc * lax.rsqrt(var + LN_EPS) * lg_ref[...] + lb_ref[...]
    if with_mod:
        sh_ref, sc_ref, o_ref, u_ref = rest
        u_ref[...] = (out * (1.0 + sc_ref[0]) + sh_ref[0]).astype(BF16)
    else:
        (o_ref,) = rest
    o_ref[...] = out


def _resln(x2, y, mod3, seq, gate_idx, ln_g, ln_b, alpha, mod_idx=None):
    t, d = x2.shape
    tm = 256
    per_seq = seq // tm
    with_mod = mod_idx is not None
    row = pl.BlockSpec((tm, d), lambda i: (i, 0))
    vec = pl.BlockSpec((1, d), lambda i: (0, 0))

    def mod_spec(idx):
        return pl.BlockSpec((1, 1, d), lambda i: (6 * (i // per_seq) + idx, 0, 0))

    in_specs = [row, row, mod_spec(gate_idx), vec, vec]
    args = [x2, y, mod3, ln_g.reshape(1, d), ln_b.reshape(1, d)]
    out_specs = row
    out_shape = jax.ShapeDtypeStruct((t, d), F32)
    if with_mod:
        in_specs += [mod_spec(mod_idx[0]), mod_spec(mod_idx[1])]
        args += [mod3, mod3]
        out_specs = [row, row]
        out_shape = [out_shape, jax.ShapeDtypeStruct((t, d), BF16)]
    return pl.pallas_call(
        functools.partial(_resln_body, alpha=alpha, with_mod=with_mod),
        grid=(t // tm,),
        in_specs=in_specs,
        out_specs=out_specs,
        out_shape=out_shape,
        compiler_params=_cparams(1),
        name="residual_ln",
    )(*args)


def _ffn_gu_body(a_ref, halo_ref, wg_ref, wu_ref, cw_ref, cb_ref, o_ref, *, tiles_per_seq):
    i = pl.program_id(0)
    a = a_ref[...]
    wg = wg_ref[...].astype(BF16)
    wu = wu_ref[...].astype(BF16)
    g = jnp.dot(a, wg, preferred_element_type=F32)
    up = jnp.dot(a, wu, preferred_element_type=F32)
    gh = jnp.dot(halo_ref[...], wg, preferred_element_type=F32)
    gh = jnp.where(i % tiles_per_seq == 0, 0.0, gh)
    prev1 = gh[SUBLANES - 1:SUBLANES, :]
    prev2 = gh[SUBLANES - 2:SUBLANES - 1, :]
    row = lax.broadcasted_iota(jnp.int32, g.shape, 0)
    g1 = jnp.where(row == 0, prev1, pltpu.roll(g, 1, 0))
    g2 = jnp.where(row == 0, prev2, jnp.where(row == 1, prev1, pltpu.roll(g, 2, 0)))
    cw = cw_ref[...]
    conv = cb_ref[...] + cw[0:1, :] * g2 + cw[1:2, :] * g1 + cw[2:3, :] * g
    o_ref[...] = (conv / (1.0 + jnp.exp(-conv)) * up).astype(BF16)


def _ffn_gate_up(u, w_gate, w_up, conv_w, conv_b, seq):
    t, d = u.shape
    f = w_gate.shape[1]
    tm, tn = 1024, 256
    halo_blocks = tm // SUBLANES
    return pl.pallas_call(
        functools.partial(_ffn_gu_body, tiles_per_seq=seq // tm),
        grid=(t // tm, f // tn),
        in_specs=[pl.BlockSpec((tm, d), lambda i, j: (i, 0)),
                  pl.BlockSpec((SUBLANES, d), lambda i, j: (jnp.maximum(i * halo_blocks - 1, 0), 0)),
                  pl.BlockSpec((d, tn), lambda i, j: (0, j)),
                  pl.BlockSpec((d, tn), lambda i, j: (0, j)),
                  pl.BlockSpec((CONV_WIDTH, tn), lambda i, j: (0, j)),
                  pl.BlockSpec((1, tn), lambda i, j: (0, j))],
        out_specs=pl.BlockSpec((tm, tn), lambda i, j: (i, j)),
        out_shape=jax.ShapeDtypeStruct((t, f), BF16),
        compiler_params=_cparams(2),
        name="ffn_gate_up",
    )(u, u, w_gate, w_up, conv_w, conv_b.reshape(1, f))


def _ffn_down_body(am_ref, at_ref, wm_ref, wt_ref, o_ref, acc_ref, *, n_main, n_tail):
    k = pl.program_id(2)

    @pl.when(k == 0)
    def _():
        acc_ref[...] = jnp.zeros_like(acc_ref)

    @pl.when(k < n_main)
    def _():
        acc_ref[...] += jnp.dot(am_ref[...], wm_ref[...].astype(BF16),
                                preferred_element_type=F32)

    @pl.when(k >= n_main)
    def _():
        acc_ref[...] += jnp.dot(at_ref[...], wt_ref[...].astype(BF16),
                                preferred_element_type=F32)

    @pl.when(k == n_main + n_tail - 1)
    def _():
        o_ref[...] = acc_ref[...]


def _ffn_down(h, w_down):
    t, f = h.shape
    d = w_down.shape[1]
    tm, tn, tk, tk_tail = 1024, 1024, 1024, 256
    n_main = f // tk
    n_tail = (f - n_main * tk) // tk_tail
    assert n_main * tk + n_tail * tk_tail == f and n_tail > 0
    tail0 = n_main * tk // tk_tail

    def main_k(k):
        return jnp.minimum(k, n_main - 1)

    def tail_k(k):
        return tail0 + jnp.maximum(k - n_main, 0)

    return pl.pallas_call(
        functools.partial(_ffn_down_body, n_main=n_main, n_tail=n_tail),
        grid=(t // tm, d // tn, n_main + n_tail),
        in_specs=[pl.BlockSpec((tm, tk), lambda i, j, k: (i, main_k(k))),
                  pl.BlockSpec((tm, tk_tail), lambda i, j, k: (i, tail_k(k))),
                  pl.BlockSpec((tk, tn), lambda i, j, k: (main_k(k), j)),
                  pl.BlockSpec((tk_tail, tn), lambda i, j, k: (tail_k(k), j))],
        out_specs=pl.BlockSpec((tm, tn), lambda i, j, k: (i, j)),
        out_shape=jax.ShapeDtypeStruct((t, d), F32),
        scratch_shapes=[pltpu.VMEM((tm, tn), F32)],
        compiler_params=_cparams(3),
        name="ffn_down",
    )(h, h, w_down, w_down)


def _rope_tile(x, c, s_lo, s_hi):
    half = IDX_ROPE_DIM // 2
    return (x * c + pltpu.roll(x, LANES - half, 1) * s_lo
            + pltpu.roll(x, half, 1) * s_hi)


KCHUNK = 2 * Q_BLOCK


def _dsa_body(iq_ref, aq_ref, ak_ref, av_ref, ik_ref, misc_ref, rope_ref, bias_ref,
              o_ref,
              ikr_ref, iq2_ref, wb_ref, st_ref, madd_ref, q2_ref, m_ref, l_ref, acc_ref,
              *, topk):
    i = pl.program_id(1)
    n_chunks = i // 2 + 1
    t0 = i * Q_BLOCK
    rc, rs_lo, rs_hi = rope_ref[0], rope_ref[1], rope_ref[2]

    @pl.when(i == 0)
    def _():
        ikr_ref[...] = jnp.zeros_like(ikr_ref)

    ikr_ref[pl.ds(pl.multiple_of(t0, Q_BLOCK), Q_BLOCK), :] = _rope_tile(
        ik_ref[...], rc, rs_lo, rs_hi).astype(BF16)

    for h in range(IDX_HEADS):
        sl = slice(h * Q_BLOCK, (h + 1) * Q_BLOCK)
        iq2_ref[sl, :] = _rope_tile(iq_ref[:, sl], rc, rs_lo, rs_hi).astype(BF16)

    w_scale = (IDX_HEADS ** -0.5) * (IDX_HEAD_DIM ** -0.5)
    for h in range(IDX_HEADS):
        col = misc_ref[:, MISC_IW + h:MISC_IW + h + 1] * w_scale
        wb_ref[h] = jnp.broadcast_to(col, (Q_BLOCK, LANES))

    heads_per_dot = 4

    def score_chunk(jj, carry):
        k0 = pl.multiple_of(jj * KCHUNK, KCHUNK)
        kc = ikr_ref[pl.ds(k0, KCHUNK), :]
        sc = jnp.zeros((Q_BLOCK, KCHUNK), F32)
        for g in range(IDX_HEADS // heads_per_dot):
            rows = slice(g * heads_per_dot * Q_BLOCK, (g + 1) * heads_per_dot * Q_BLOCK)
            d = lax.dot_general(iq2_ref[rows, :], kc, _NT, preferred_element_type=F32)
            for hh in range(heads_per_dot):
                w = wb_ref[g * heads_per_dot + hh]
                dh = jnp.maximum(d[hh * Q_BLOCK:(hh + 1) * Q_BLOCK, :], 0.0)
                sc = sc + dh * jnp.concatenate([w, w], axis=1)
        tq = t0 + lax.broadcasted_iota(jnp.int32, sc.shape, 0)
        sk = k0 + lax.broadcasted_iota(jnp.int32, sc.shape, 1)
        sc = jnp.where(sk <= tq, sc, NEG_INF)
        st_ref[pl.ds(k0, Q_BLOCK), :] = sc[:, :Q_BLOCK].T
        st_ref[pl.ds(k0 + Q_BLOCK, Q_BLOCK), :] = sc[:, Q_BLOCK:].T
        return carry

    lax.fori_loop(0, n_chunks, score_chunk, 0)

    def count(pred):
        def body(jj, acc):
            k0 = pl.multiple_of(jj * KCHUNK, KCHUNK)
            hit = jnp.where(pred(st_ref[pl.ds(k0, KCHUNK), :], k0), 1.0, 0.0)
            return acc + jnp.sum(hit.reshape(KCHUNK // SUBLANES, SUBLANES, LANES), axis=0)
        acc = lax.fori_loop(0, n_chunks, body, jnp.zeros((SUBLANES, LANES), F32))
        return jnp.sum(acc, axis=0, keepdims=True)

    def key_to_f32(key):
        bits = key ^ ((key >> 31) & jnp.int32(0x7FFFFFFF))
        return lax.bitcast_convert_type(bits, F32)

    kf = float(topk)

    def select_pass(p, key):
        cand = key + jnp.left_shift(jnp.int32(1), 31 - p)
        thr_c = key_to_f32(cand)
        n_ge = count(lambda blk, k0: blk >= thr_c)
        return jnp.where(n_ge >= kf, cand, key)

    key0 = jnp.full((1, LANES), jnp.iinfo(jnp.int32).min, jnp.int32)
    thr = key_to_f32(lax.fori_loop(0, 32, select_pass, key0))

    def write_mask(sel_fn):
        def body(jj, carry):
            k0 = pl.multiple_of(jj * KCHUNK, KCHUNK)
            blk = st_ref[pl.ds(k0, KCHUNK), :]
            madd_ref[pl.ds(k0, KCHUNK), :] = jnp.where(sel_fn(blk, k0), 0.0, NEG_INF)
            return carry
        lax.fori_loop(0, n_chunks, body, 0)

    write_mask(lambda blk, k0: blk >= thr)

    n_ge = count(lambda blk, k0: blk >= thr)
    tie = jnp.where((n_ge > kf) & (thr > NEG_INF), 1.0, 0.0)

    @pl.when(jnp.max(tie) > 0.0)
    def _():
        need = kf - count(lambda blk, k0: blk > thr)

        def key_index(shape, k0):
            return k0 + lax.broadcasted_iota(jnp.int32, shape, 0)

        def index_pass(p, lo):
            cand = lo + jnp.left_shift(jnp.int32(1), 10 - p)
            n_eq = count(lambda blk, k0: (blk == thr) & (key_index(blk.shape, k0) < cand))
            return jnp.where(n_eq < need, cand, lo)

        lo = lax.fori_loop(0, 11, index_pass, jnp.zeros((1, LANES), jnp.int32))
        write_mask(lambda blk, k0: (blk > thr)
                   | ((blk == thr) & (key_index(blk.shape, k0) <= lo)))

    att_scale = A_HEAD_DIM ** -0.5
    for h in range(A_HEADS):
        sl = slice(h * Q_BLOCK, (h + 1) * Q_BLOCK)
        q2_ref[sl, :] = (aq_ref[:, sl] * att_scale).astype(BF16)
    m_ref[...] = jnp.full_like(m_ref, NEG_INF)
    l_ref[...] = jnp.zeros_like(l_ref)
    acc_ref[...] = jnp.zeros_like(acc_ref)

    heads_per_att = 4

    def att_chunk(jj, carry):
        k0 = pl.multiple_of(jj * KCHUNK, KCHUNK)
        kc = ak_ref[pl.ds(k0, KCHUNK), :].astype(BF16)
        vc = av_ref[pl.ds(k0, KCHUNK), :].astype(BF16)
        sel = jnp.concatenate([madd_ref[pl.ds(k0, Q_BLOCK), :].T,
                               madd_ref[pl.ds(k0 + Q_BLOCK, Q_BLOCK), :].T], axis=1)
        tq = t0 + lax.broadcasted_iota(jnp.int32, sel.shape, 0)
        sk = k0 + lax.broadcasted_iota(jnp.int32, sel.shape, 1)
        sel = jnp.where(sk <= tq, sel, NEG_INF)
        bidx = jnp.minimum(i - 2 * jj, 3)
        for g in range(A_HEADS // heads_per_att):
            rows = slice(g * heads_per_att * Q_BLOCK, (g + 1) * heads_per_att * Q_BLOCK)
            lg = lax.dot_general(q2_ref[rows, :], kc, _NT, preferred_element_type=F32)
            ps = []
            alphas = []
            for hh in range(heads_per_att):
                h = g * heads_per_att + hh
                hs = slice(h * Q_BLOCK, (h + 1) * Q_BLOCK)
                s = lg[hh * Q_BLOCK:(hh + 1) * Q_BLOCK, :] + bias_ref[bidx, h] + sel
                m_old = m_ref[hs, :]
                m_new = jnp.maximum(m_old, jnp.max(s, axis=1, keepdims=True))
                alpha = jnp.exp(m_old - m_new)
                p = jnp.exp(s - m_new)
                l_ref[hs, :] = alpha * l_ref[hs, :] + jnp.sum(p, axis=1, keepdims=True)
                m_ref[hs, :] = m_new
                ps.append(p.astype(BF16))
                alphas.append(alpha)
            pv = jnp.dot(jnp.concatenate(ps, axis=0), vc, preferred_element_type=F32)
            for hh in range(heads_per_att):
                h = g * heads_per_att + hh
                hs = slice(h * Q_BLOCK, (h + 1) * Q_BLOCK)
                acc_ref[hs, :] = alphas[hh] * acc_ref[hs, :] + pv[hh * Q_BLOCK:(hh + 1) * Q_BLOCK, :]
        return carry

    lax.fori_loop(0, n_chunks, att_chunk, 0)

    for h in range(A_HEADS):
        hs = slice(h * Q_BLOCK, (h + 1) * Q_BLOCK)
        o_ref[:, hs] = (acc_ref[hs, :] / l_ref[hs, :]).astype(BF16)


def _dsa(p, rope_a, bias_tiles, batch, seq):
    t = p.shape[0]
    nb = seq // Q_BLOCK
    topk = min(TOPK_MAX, seq // 4)
    assert topk <= KCHUNK and seq % KCHUNK == 0
    qrow = lambda b, i: b * nb + i
    return pl.pallas_call(
        functools.partial(_dsa_body, topk=topk),
        grid=(batch, nb),
        in_specs=[
            pl.BlockSpec((Q_BLOCK, IDX_HEADS * IDX_HEAD_DIM), lambda b, i: (qrow(b, i), COL_IQ // 4096)),
            pl.BlockSpec((Q_BLOCK, A_HEADS * A_HEAD_DIM), lambda b, i: (qrow(b, i), COL_AQ // 2048)),
            pl.BlockSpec((seq, LANES), lambda b, i: (b, COL_AK // LANES)),
            pl.BlockSpec((seq, LANES), lambda b, i: (b, COL_AV // LANES)),
            pl.BlockSpec((Q_BLOCK, LANES), lambda b, i: (qrow(b, i), COL_IK // LANES)),
            pl.BlockSpec((Q_BLOCK, LANES), lambda b, i: (qrow(b, i), COL_MISC // LANES)),
            pl.BlockSpec((3, Q_BLOCK, LANES), lambda b, i: (0, qrow(b, i), 0)),
            pl.BlockSpec(bias_tiles.shape, lambda b, i: (0, 0, 0, 0)),
        ],
        out_specs=pl.BlockSpec((Q_BLOCK, A_HEADS * A_HEAD_DIM), lambda b, i: (qrow(b, i), 0)),
        out_shape=jax.ShapeDtypeStruct((t, A_HEADS * A_HEAD_DIM), BF16),
        scratch_shapes=[
            pltpu.VMEM((seq, IDX_HEAD_DIM), BF16),
            pltpu.VMEM((IDX_HEADS * Q_BLOCK, IDX_HEAD_DIM), BF16),
            pltpu.VMEM((IDX_HEADS, Q_BLOCK, LANES), F32),
            pltpu.VMEM((seq, Q_BLOCK), F32),
            pltpu.VMEM((seq, Q_BLOCK), F32),
            pltpu.VMEM((A_HEADS * Q_BLOCK, A_HEAD_DIM), BF16),
            pltpu.VMEM((A_HEADS * Q_BLOCK, 1), F32),
            pltpu.VMEM((A_HEADS * Q_BLOCK, 1), F32),
            pltpu.VMEM((A_HEADS * Q_BLOCK, A_HEAD_DIM), F32),
        ],
        compiler_params=_cparams(2),
        name="dsa_attention",
    )(p, p, p, p, p, p, rope_a, bias_tiles)


MLA_BLOCK = 512


def _mla_body(q_ref, kv_ref, misc_ref, ropeq_ref, ropek_ref, o_ref, krr_ref, kc_ref):
    h = pl.program_id(1)
    qi = pl.program_id(2)

    @pl.when((h == 0) & (qi == 0))
    def _():
        krr_ref[...] = _rope_tile(misc_ref[...], ropek_ref[0], ropek_ref[1],
                                  ropek_ref[2]).astype(BF16)

    @pl.when(qi == 0)
    def _():
        kc_ref[:, :QK_NOPE_DIM] = kv_ref[:, :QK_NOPE_DIM]
        kc_ref[:, QK_NOPE_DIM:] = krr_ref[...]

    scale = (QK_NOPE_DIM + QK_ROPE_DIM) ** -0.5
    q = q_ref[...]
    q_rope = _rope_tile(q[:, QK_NOPE_DIM:], ropeq_ref[0], ropeq_ref[1], ropeq_ref[2])
    qc = (jnp.concatenate([q[:, :QK_NOPE_DIM], q_rope], axis=1) * scale).astype(BF16)

    def step(j, carry, masked):
        m_old, l_old, acc = carry
        k0 = pl.multiple_of(j * MLA_BLOCK, MLA_BLOCK)
        s = lax.dot_general(qc, kc_ref[pl.ds(k0, MLA_BLOCK), :], _NT,
                            preferred_element_type=F32)
        if masked:
            tq = lax.broadcasted_iota(jnp.int32, s.shape, 0)
            sk = lax.broadcasted_iota(jnp.int32, s.shape, 1)
            s = jnp.where(sk <= tq, s, NEG_INF)
        m_new = jnp.maximum(m_old, jnp.max(s, axis=1, keepdims=True))
        alpha = jnp.exp(m_old - m_new)
        p = jnp.exp(s - m_new)
        l_new = alpha * l_old + jnp.sum(p, axis=1, keepdims=True)
        v = kv_ref[pl.ds(k0, MLA_BLOCK), QK_NOPE_DIM:]
        acc = alpha * acc + jnp.dot(p.astype(BF16), v, preferred_element_type=F32)
        return m_new, l_new, acc

    carry = (jnp.full((MLA_BLOCK, 1), NEG_INF, F32),
             jnp.zeros((MLA_BLOCK, 1), F32),
             jnp.zeros((MLA_BLOCK, V_HEAD_DIM), F32))
    carry = lax.fori_loop(0, qi, functools.partial(step, masked=False), carry)
    _, l_fin, acc = step(qi, carry, masked=True)
    o_ref[...] = (acc / l_fin).astype(BF16)


def _mla(q, kv, p, rope_b, batch, seq):
    t = q.shape[0]
    nq = seq // MLA_BLOCK
    head_w = 2 * LANES
    qrow = lambda b, h, qi: b * nq + qi
    return pl.pallas_call(
        _mla_body,
        grid=(batch, B_HEADS, nq),
        in_specs=[
            pl.BlockSpec((MLA_BLOCK, head_w), lambda b, h, qi: (qrow(b, h, qi), h)),
            pl.BlockSpec((seq, head_w), lambda b, h, qi: (b, h)),
            pl.BlockSpec((seq, LANES), lambda b, h, qi: (b, COL_MISC // LANES)),
            pl.BlockSpec((3, MLA_BLOCK, LANES), lambda b, h, qi: (0, qrow(b, h, qi), 0)),
            pl.BlockSpec((3, seq, LANES), lambda b, h, qi: (0, b, 0)),
        ],
        out_specs=pl.BlockSpec((MLA_BLOCK, V_HEAD_DIM), lambda b, h, qi: (qrow(b, h, qi), h)),
        out_shape=jax.ShapeDtypeStruct((t, B_HEADS * V_HEAD_DIM), BF16),
        scratch_shapes=[pltpu.VMEM((seq, LANES), BF16),
                        pltpu.VMEM((seq, head_w), BF16)],
        compiler_params=_cparams(3),
        name="mla_attention",
    )(q, kv, p, rope_b, rope_b)


def _rel_bucket(rel):
    n = jnp.maximum(rel, 0)
    max_exact = REL_BUCKETS // 2
    nf = jnp.maximum(n, 1).astype(F32)
    large = max_exact + (jnp.log(nf / max_exact) / math.log(REL_MAX_DIST / max_exact)
                         * (REL_BUCKETS - max_exact)).astype(jnp.int32)
    large = jnp.minimum(large, REL_BUCKETS - 1)
    return jnp.where(n < max_exact, n, large)


def _bias_tiles(rel_bias):
    tq = jnp.arange(Q_BLOCK, dtype=jnp.int32)[:, None]
    sk = jnp.arange(KCHUNK, dtype=jnp.int32)[None, :]
    tiles = []
    for o in range(4):
        bucket = _rel_bucket(o * Q_BLOCK + tq - sk)
        tiles.append(jnp.transpose(rel_bias[bucket], (2, 0, 1)))
    return jnp.stack(tiles).astype(F32)


def _rope_tables(positions):
    half = IDX_ROPE_DIM // 2
    freqs = ROPE_THETA ** (-jnp.arange(half, dtype=F32) / half)
    ang = positions.astype(F32).reshape(-1, 1) * freqs
    cos, sin = jnp.cos(ang), jnp.sin(ang)
    one, zero = jnp.ones_like(cos), jnp.zeros_like(cos)
    cat = lambda parts: jnp.concatenate(parts, axis=1)
    rope_a = jnp.stack([cat([one, one, cos, cos]), cat([zero, zero, -sin, zero]),
                        cat([zero, zero, zero, sin])])
    rope_b = jnp.stack([cat([cos, cos, zero, zero]), cat([-sin, zero, zero, zero]),
                        cat([zero, sin, zero, zero])])
    return rope_a, rope_b


def _reorder_w_in(w):
    splits = (A_HEADS * A_HEAD_DIM, A_HEAD_DIM, A_HEAD_DIM, IDX_HEADS * IDX_HEAD_DIM,
              IDX_HEAD_DIM, IDX_HEADS, Q_LORA_RANK, KV_LORA_RANK, QK_ROPE_DIM)
    offs = np.concatenate([[0], np.cumsum(splits)])
    aq, ak, av, iq, ik, iw, ql, kvl, kr = [w[:, int(offs[n]):int(offs[n + 1])] for n in range(9)]
    pad = jnp.zeros((w.shape[0], LANES - QK_ROPE_DIM - IDX_HEADS), w.dtype)
    out = jnp.concatenate([iq, aq, ql, kvl, ak, av, ik, kr, iw, pad], axis=1).astype(BF16)
    assert out.shape[1] == P_WIDTH
    return out


def _reorder_w_uq(w):
    r = w.shape[0]
    w3 = w.reshape(r, B_HEADS, QK_NOPE_DIM + QK_ROPE_DIM)
    w3 = jnp.pad(w3, ((0, 0), (0, 0), (0, 2 * LANES - QK_NOPE_DIM - QK_ROPE_DIM)))
    return w3.reshape(r, B_HEADS * 2 * LANES)


def kernel(x, c, positions, w_ada, b_ada, w_in, rel_bias, q_norm_g, w_uq, kv_norm_g, w_ukv,
           w_o, ln1_g, ln1_b, w_gate, w_up, conv_w, conv_b, w_down, ln2_g, ln2_b):
    batch, seq, d = x.shape
    depth = w_ada.shape[0]
    t = batch * seq
    assert d == D_MODEL and batch <= SUBLANES and seq % 1024 == 0
    alpha = (2 * depth) ** 0.25

    rope_a, rope_b = _rope_tables(positions)
    bias_tiles = _bias_tiles(rel_bias)
    c8 = jnp.zeros((SUBLANES, d), F32).at[:batch].set(c)
    x2 = x.reshape(t, d)

    for l in range(depth):
        mod = _ada(c8, w_ada[l], b_ada[l])
        mod3 = mod[:batch].reshape(batch * 6, 1, d)

        u = _modulate(x2, mod3, seq, shift_idx=0, scale_idx=1)
        p = _matmul(u, _reorder_w_in(w_in[l]), F32, 1024, 512, "in_proj")
        y_a = _dsa(p, rope_a, bias_tiles, batch, seq)
        q_lat = _rmsnorm_cols(p, COL_QL, Q_LORA_RANK, q_norm_g[l], "q_rmsnorm")
        kv_lat = _rmsnorm_cols(p, COL_KVL, KV_LORA_RANK, kv_norm_g[l], "kv_rmsnorm")
        q = _matmul(q_lat, _reorder_w_uq(w_uq[l]), F32, 1024, 512, "q_up_proj")
        kv = _matmul(kv_lat, w_ukv[l], BF16, 1024, 512, "kv_up_proj")
        y_b = _mla(q, kv, p, rope_b, batch, seq)
        mix = _matmul_concat(y_a, y_b, w_o[l], 1024, 512, "out_proj")
        x2, u = _resln(x2, mix, mod3, seq, 2, ln1_g[l], ln1_b[l], alpha, mod_idx=(3, 4))

        hidden = _ffn_gate_up(u, w_gate[l], w_up[l], conv_w[l], conv_b[l], seq)
        y = _ffn_down(hidden, w_down[l])
        x2 = _resln(x2, y, mod3, seq, 5, ln2_g[l], ln2_b[l], alpha)

    return x2.reshape(batch, seq, d)
```

```python
import functools
import math

import jax
import jax.numpy as jnp
import numpy as np
from jax import lax
from jax.experimental import pallas as pl
from jax.experimental.pallas import tpu as pltpu

F32 = jnp.float32
BF16 = jnp.bfloat16

D_MODEL = 4096
A_HEAD_DIM = 128
A_HEADS = 16
IDX_HEADS = 32
IDX_HEAD_DIM = 128
IDX_ROPE_DIM = 64
TOPK_MAX = 256
V_HEAD_DIM = 128
B_HEADS = 16
Q_LORA_RANK = 1024
KV_LORA_RANK = 512
QK_NOPE_DIM = 128
QK_ROPE_DIM = 64
D_FF = 11008
CONV_WIDTH = 3
REL_BUCKETS = 32
REL_MAX_DIST = 128
ROPE_THETA = 10000.0
Q_BLOCK = 128
LN_EPS = 1e-5
RMS_EPS = 1e-6
NEG_INF = -1e30

LANES = 128
SUBLANES = 8
VMEM_LIMIT_BYTES = 56 * 1024 * 1024

P_WIDTH = 8192
COL_IQ = 0
COL_AQ = 4096
COL_QL = 6144
COL_KVL = 7168
COL_AK = 7680
COL_AV = 7808
COL_IK = 7936
COL_MISC = 8064
MISC_IW = 32

_NT = (((1,), (1,)), ((), ()))


def _cparams(n_axes):
    return pltpu.CompilerParams(
        dimension_semantics=("arbitrary",) * n_axes,
        vmem_limit_bytes=VMEM_LIMIT_BYTES)


def _ada_body(c_ref, w_ref, b_ref, o_ref):
    c = c_ref[...]
    act = (c / (1.0 + jnp.exp(-c))).astype(BF16)
    o_ref[...] = jnp.dot(act, w_ref[...].astype(BF16),
                         preferred_element_type=F32) + b_ref[...]


def _ada(c8, w, b):
    d, n = w.shape
    tn = 512
    return pl.pallas_call(
        _ada_body,
        grid=(n // tn,),
        in_specs=[pl.BlockSpec((SUBLANES, d), lambda j: (0, 0)),
                  pl.BlockSpec((d, tn), lambda j: (0, j)),
                  pl.BlockSpec((1, tn), lambda j: (0, j))],
        out_specs=pl.BlockSpec((SUBLANES, tn), lambda j: (0, j)),
        out_shape=jax.ShapeDtypeStruct((SUBLANES, n), F32),
        compiler_params=_cparams(1),
        name="ada_proj",
    )(c8, w, b.reshape(1, n))


def _modulate_body(x_ref, sh_ref, sc_ref, o_ref):
    o_ref[...] = (x_ref[...] * (1.0 + sc_ref[0]) + sh_ref[0]).astype(BF16)


def _modulate(x2, mod3, seq, shift_idx, scale_idx):
    t, d = x2.shape
    tm = 256
    per_seq = seq // tm
    return pl.pallas_call(
        _modulate_body,
        grid=(t // tm,),
        in_specs=[pl.BlockSpec((tm, d), lambda i: (i, 0)),
                  pl.BlockSpec((1, 1, d), lambda i: (6 * (i // per_seq) + shift_idx, 0, 0)),
                  pl.BlockSpec((1, 1, d), lambda i: (6 * (i // per_seq) + scale_idx, 0, 0))],
        out_specs=pl.BlockSpec((tm, d), lambda i: (i, 0)),
        out_shape=jax.ShapeDtypeStruct((t, d), BF16),
        compiler_params=_cparams(1),
        name="modulate",
    )(x2, mod3, mod3)


def _mm_body(a_ref, w_ref, o_ref):
    o_ref[...] = jnp.dot(a_ref[...], w_ref[...].astype(BF16),
                         preferred_element_type=F32).astype(o_ref.dtype)


def _matmul(a, w, out_dtype, tm, tn, name):
    m, k = a.shape
    n = w.shape[1]
    return pl.pallas_call(
        _mm_body,
        grid=(m // tm, n // tn),
        in_specs=[pl.BlockSpec((tm, k), lambda i, j: (i, 0)),
                  pl.BlockSpec((k, tn), lambda i, j: (0, j))],
        out_specs=pl.BlockSpec((tm, tn), lambda i, j: (i, j)),
        out_shape=jax.ShapeDtypeStruct((m, n), out_dtype),
        compiler_params=_cparams(2),
        name=name,
    )(a, w)


def _mm2_body(a0_ref, a1_ref, w0_ref, w1_ref, o_ref):
    acc = jnp.dot(a0_ref[...], w0_ref[...].astype(BF16), preferred_element_type=F32)
    acc = acc + jnp.dot(a1_ref[...], w1_ref[...].astype(BF16), preferred_element_type=F32)
    o_ref[...] = acc


def _matmul_concat(a0, a1, w, tm, tn, name):
    m, k0 = a0.shape
    k1 = a1.shape[1]
    assert k0 == k1
    n = w.shape[1]
    return pl.pallas_call(
        _mm2_body,
        grid=(m // tm, n // tn),
        in_specs=[pl.BlockSpec((tm, k0), lambda i, j: (i, 0)),
                  pl.BlockSpec((tm, k1), lambda i, j: (i, 0)),
                  pl.BlockSpec((k0, tn), lambda i, j: (0, j)),
                  pl.BlockSpec((k1, tn), lambda i, j: (1, j))],
        out_specs=pl.BlockSpec((tm, tn), lambda i, j: (i, j)),
        out_shape=jax.ShapeDtypeStruct((m, n), F32),
        compiler_params=_cparams(2),
        name=name,
    )(a0, a1, w, w)


def _rms_body(x_ref, g_ref, o_ref):
    x = x_ref[...]
    ms = jnp.mean(x * x, axis=-1, keepdims=True)
    o_ref[...] = (x * lax.rsqrt(ms + RMS_EPS) * g_ref[...]).astype(BF16)


def _rmsnorm_cols(p, col, width, gain, name):
    t = p.shape[0]
    tm = 512
    cb = col // width
    return pl.pallas_call(
        _rms_body,
        grid=(t // tm,),
        in_specs=[pl.BlockSpec((tm, width), lambda i: (i, cb)),
                  pl.BlockSpec((1, width), lambda i: (0, 0))],
        out_specs=pl.BlockSpec((tm, width), lambda i: (i, 0)),
        out_shape=jax.ShapeDtypeStruct((t, width), BF16),
        compiler_params=_cparams(1),
        name=name,
    )(p, gain.reshape(1, width))


def _resln_body(x_ref, y_ref, gate_ref, lg_ref, lb_ref, *rest, alpha, with_mod):
    z = alpha * x_ref[...] + (1.0 + gate_ref[0]) * y_ref[...]
    mu = jnp.mean(z, axis=-1, keepdims=True)
    zc = z - mu
    var = jnp.mean(zc * zc, axis=-1, keepdims=True)
    out = zc * lax.rsqrt(var + LN_EPS) * lg_ref[...] + lb_ref[...]
    if with_mod:
        sh_ref, sc_ref, o_ref, u_ref = rest
        u_ref[...] = (out * (1.0 + sc_ref[0]) + sh_ref[0]).astype(BF16)
    else:
        (o_ref,) = rest
    o_ref[...] = out


def _resln(x2, y, mod3, seq, gate_idx, ln_g, ln_b, alpha, mod_idx=None):
    t, d = x2.shape
    tm = 256
    per_seq = seq // tm
    with_mod = mod_idx is not None
    row = pl.BlockSpec((tm, d), lambda i: (i, 0))
    vec = pl.BlockSpec((1, d), lambda i: (0, 0))

    def mod_spec(idx):
        return pl.BlockSpec((1, 1, d), lambda i: (6 * (i // per_seq) + idx, 0, 0))

    in_specs = [row, row, mod_spec(gate_idx), vec, vec]
    args = [x2, y, mod3, ln_g.reshape(1, d), ln_b.reshape(1, d)]
    out_specs = row
    out_shape = jax.ShapeDtypeStruct((t, d), F32)
    if with_mod:
        in_specs += [mod_spec(mod_idx[0]), mod_spec(mod_idx[1])]
        args += [mod3, mod3]
        out_specs = [row, row]
        out_shape = [out_shape, jax.ShapeDtypeStruct((t, d), BF16)]
    return pl.pallas_call(
        functools.partial(_resln_body, alpha=alpha, with_mod=with_mod),
        grid=(t // tm,),
        in_specs=in_specs,
        out_specs=out_specs,
        out_shape=out_shape,
        compiler_params=_cparams(1),
        name="residual_ln",
    )(*args)


def _ffn_gu_body(a_ref, halo_ref, wg_ref, wu_ref, cw_ref, cb_ref, o_ref, *, tiles_per_seq):
    i = pl.program_id(0)
    a = a_ref[...]
    wg = wg_ref[...].astype(BF16)
    wu = wu_ref[...].astype(BF16)
    g = jnp.dot(a, wg, preferred_element_type=F32)
    up = jnp.dot(a, wu, preferred_element_type=F32)
    gh = jnp.dot(halo_ref[...], wg, preferred_element_type=F32)
    gh = jnp.where(i % tiles_per_seq == 0, 0.0, gh)
    prev1 = gh[SUBLANES - 1:SUBLANES, :]
    prev2 = gh[SUBLANES - 2:SUBLANES - 1, :]
    row = lax.broadcasted_iota(jnp.int32, g.shape, 0)
    g1 = jnp.where(row == 0, prev1, pltpu.roll(g, 1, 0))
    g2 = jnp.where(row == 0, prev2, jnp.where(row == 1, prev1, pltpu.roll(g, 2, 0)))
    cw = cw_ref[...]
    conv = cb_ref[...] + cw[0:1, :] * g2 + cw[1:2, :] * g1 + cw[2:3, :] * g
    o_ref[...] = (conv / (1.0 + jnp.exp(-conv)) * up).astype(BF16)


def _ffn_gate_up(u, w_gate, w_up, conv_w, conv_b, seq):
    t, d = u.shape
    f = w_gate.shape[1]
    tm, tn = 1024, 256
    halo_blocks = tm // SUBLANES
    return pl.pallas_call(
        functools.partial(_ffn_gu_body, tiles_per_seq=seq // tm),
        grid=(t // tm, f // tn),
        in_specs=[pl.BlockSpec((tm, d), lambda i, j: (i, 0)),
                  pl.BlockSpec((SUBLANES, d), lambda i, j: (jnp.maximum(i * halo_blocks - 1, 0), 0)),
                  pl.BlockSpec((d, tn), lambda i, j: (0, j)),
                  pl.BlockSpec((d, tn), lambda i, j: (0, j)),
                  pl.BlockSpec((CONV_WIDTH, tn), lambda i, j: (0, j)),
                  pl.BlockSpec((1, tn), lambda i, j: (0, j))],
        out_specs=pl.BlockSpec((tm, tn), lambda i, j: (i, j)),
        out_shape=jax.ShapeDtypeStruct((t, f), BF16),
        compiler_params=_cparams(2),
        name="ffn_gate_up",
    )(u, u, w_gate, w_up, conv_w, conv_b.reshape(1, f))


def _ffn_down_body(am_ref, at_ref, wm_ref, wt_ref, o_ref, acc_ref, *, n_main, n_tail):
    k = pl.program_id(2)

    @pl.when(k == 0)
    def _():
        acc_ref[...] = jnp.zeros_like(acc_ref)

    @pl.when(k < n_main)
    def _():
        acc_ref[...] += jnp.dot(am_ref[...], wm_ref[...].astype(BF16),
                                preferred_element_type=F32)

    @pl.when(k >= n_main)
    def _():
        acc_ref[...] += jnp.dot(at_ref[...], wt_ref[...].astype(BF16),
                                preferred_element_type=F32)

    @pl.when(k == n_main + n_tail - 1)
    def _():
        o_ref[...] = acc_ref[...]


def _ffn_down(h, w_down):
    t, f = h.shape
    d = w_down.shape[1]
    tm, tn, tk, tk_tail = 1024, 1024, 1024, 256
    n_main = f // tk
    n_tail = (f - n_main * tk) // tk_tail
    assert n_main * tk + n_tail * tk_tail == f and n_tail > 0
    tail0 = n_main * tk // tk_tail

    def main_k(k):
        return jnp.minimum(k, n_main - 1)

    def tail_k(k):
        return tail0 + jnp.maximum(k - n_main, 0)

    return pl.pallas_call(
        functools.partial(_ffn_down_body, n_main=n_main, n_tail=n_tail),
        grid=(t // tm, d // tn, n_main + n_tail),
        in_specs=[pl.BlockSpec((tm, tk), lambda i, j, k: (i, main_k(k))),
                  pl.BlockSpec((tm, tk_tail), lambda i, j, k: (i, tail_k(k))),
                  pl.BlockSpec((tk, tn), lambda i, j, k: (main_k(k), j)),
                  pl.BlockSpec((tk_tail, tn), lambda i, j, k: (tail_k(k), j))],
        out_specs=pl.BlockSpec((tm, tn), lambda i, j, k: (i, j)),
        out_shape=jax.ShapeDtypeStruct((t, d), F32),
        scratch_shapes=[pltpu.VMEM((tm, tn), F32)],
        compiler_params=_cparams(3),
        name="ffn_down",
    )(h, h, w_down, w_down)


def _rope_tile(x, c, s):
    return x * c + pltpu.roll(x, LANES // 2, 1) * s


KCHUNK = 2 * Q_BLOCK
N_BIAS_TILES = 4


def _bias_tiles_body(rb_ref, o_ref):
    o = pl.program_id(0)
    tq = lax.broadcasted_iota(jnp.int32, (Q_BLOCK, KCHUNK), 0)
    sk = lax.broadcasted_iota(jnp.int32, (Q_BLOCK, KCHUNK), 1)
    n = jnp.maximum(o * Q_BLOCK + tq - sk, 0)
    max_exact = REL_BUCKETS // 2
    nf = jnp.maximum(n, 1).astype(F32)
    large = max_exact + (jnp.log(nf / max_exact) / math.log(REL_MAX_DIST / max_exact)
                         * (REL_BUCKETS - max_exact)).astype(jnp.int32)
    large = jnp.minimum(large, REL_BUCKETS - 1)
    bucket = jnp.where(n < max_exact, n, large)
    hits = [bucket == b for b in range(REL_BUCKETS - 1)]
    for h in range(A_HEADS):
        tile = jnp.full((Q_BLOCK, KCHUNK), rb_ref[REL_BUCKETS - 1, h], F32)
        for b in range(REL_BUCKETS - 1):
            tile = jnp.where(hits[b], rb_ref[b, h], tile)
        o_ref[0, h] = tile


def _bias_tiles(rel_bias):
    return pl.pallas_call(
        _bias_tiles_body,
        grid=(N_BIAS_TILES,),
        in_specs=[pl.BlockSpec(memory_space=pltpu.SMEM)],
        out_specs=pl.BlockSpec((1, A_HEADS, Q_BLOCK, KCHUNK), lambda o: (o, 0, 0, 0)),
        out_shape=jax.ShapeDtypeStruct((N_BIAS_TILES, A_HEADS, Q_BLOCK, KCHUNK), F32),
        compiler_params=_cparams(1),
        name="rel_bias_tiles",
    )(rel_bias)


SEL_ROWS = 512
SEL_SLAB = 64


def _dsa_body(iq_ref, aq_ref, ak_ref, av_ref, ik_ref, misc_ref, rope_ref, bias_ref,
              o_ref,
              ikr_ref, akb_ref, vaug_ref, iq2_ref, wb_ref, st_ref, madd_ref, q2_ref,
              m_ref, l_ref, acc_ref, thr_ref,
              *, topk, seq):
    i = pl.program_id(1)
    n_chunks = i // 2 + 1
    t0 = i * Q_BLOCK
    rc, rs = rope_ref[0], rope_ref[1]

    @pl.when(i == 0)
    def _():
        ikr_ref[...] = jnp.zeros_like(ikr_ref)
        akb_ref[...] = ak_ref[...].astype(BF16)
        vaug_ref[:, :A_HEAD_DIM] = av_ref[...].astype(BF16)
        vaug_ref[:, A_HEAD_DIM:] = jnp.ones((seq, LANES), BF16)

    ikr_ref[pl.ds(pl.multiple_of(t0, Q_BLOCK), Q_BLOCK), :] = _rope_tile(
        ik_ref[...], rc, rs).astype(BF16)

    for h in range(IDX_HEADS):
        sl = slice(h * Q_BLOCK, (h + 1) * Q_BLOCK)
        iq2_ref[sl, :] = _rope_tile(iq_ref[:, sl], rc, rs).astype(BF16)

    w_scale = (IDX_HEADS ** -0.5) * (IDX_HEAD_DIM ** -0.5)
    for h in range(IDX_HEADS):
        col = misc_ref[:, MISC_IW + h:MISC_IW + h + 1] * w_scale
        wb_ref[h] = jnp.broadcast_to(col, (Q_BLOCK, LANES))

    heads_per_dot = 4

    def score_chunk(jj, carry):
        k0 = pl.multiple_of(jj * KCHUNK, KCHUNK)
        kc = ikr_ref[pl.ds(k0, KCHUNK), :]
        sc = jnp.zeros((Q_BLOCK, KCHUNK), F32)
        for g in range(IDX_HEADS // heads_per_dot):
            rows = slice(g * heads_per_dot * Q_BLOCK, (g + 1) * heads_per_dot * Q_BLOCK)
            d = lax.dot_general(iq2_ref[rows, :], kc, _NT, preferred_element_type=F32)
            for hh in range(heads_per_dot):
                w = wb_ref[g * heads_per_dot + hh]
                dh = jnp.maximum(d[hh * Q_BLOCK:(hh + 1) * Q_BLOCK, :], 0.0)
                sc = sc + dh * jnp.concatenate([w, w], axis=1)
        tq = t0 + lax.broadcasted_iota(jnp.int32, sc.shape, 0)
        sk = k0 + lax.broadcasted_iota(jnp.int32, sc.shape, 1)
        sc = jnp.where(sk <= tq, sc, NEG_INF)
        st_ref[pl.ds(k0, Q_BLOCK), :] = sc[:, :Q_BLOCK].T
        st_ref[pl.ds(k0 + Q_BLOCK, Q_BLOCK), :] = sc[:, Q_BLOCK:].T
        return carry

    lax.fori_loop(0, n_chunks, score_chunk, 0)

    @pl.when(n_chunks % (SEL_ROWS // KCHUNK) == 1)
    def _():
        pad0 = pl.multiple_of(n_chunks * KCHUNK, KCHUNK)
        st_ref[pl.ds(pad0, KCHUNK), :] = jnp.full((KCHUNK, Q_BLOCK), NEG_INF, F32)

    def key_to_f32(key):
        bits = key ^ ((key >> 31) & jnp.int32(0x7FFFFFFF))
        return lax.bitcast_convert_type(bits, F32)

    kf = float(topk)

    def select_static(rows):
        def count_ge(thr_row):
            thr_b = jnp.broadcast_to(thr_row, (SEL_SLAB, LANES))
            acc = jnp.zeros((SEL_SLAB, LANES), F32)
            for r in range(rows // SEL_SLAB):
                blk = st_ref[r * SEL_SLAB:(r + 1) * SEL_SLAB, :]
                acc = jnp.where(blk >= thr_b, acc + 1.0, acc)
            acc = jnp.sum(acc.reshape(SEL_SLAB // SUBLANES, SUBLANES, LANES), axis=0)
            return jnp.sum(acc, axis=0, keepdims=True)

        def select_pass(p, key):
            cand = key + jnp.left_shift(jnp.int32(1), 31 - p)
            return jnp.where(count_ge(key_to_f32(cand)) >= kf, cand, key)

        key0 = jnp.full((1, LANES), jnp.iinfo(jnp.int32).min, jnp.int32)
        thr = key_to_f32(lax.fori_loop(0, 32, select_pass, key0))
        thr_ref[0:1, :] = thr
        thr_ref[1:2, :] = count_ge(thr)
        madd_ref[0:rows, :] = jnp.where(st_ref[0:rows, :] >= thr, 0.0, NEG_INF)

    for n in range(1, seq // SEL_ROWS + 1):
        pl.when(i // (SEL_ROWS // Q_BLOCK) + 1 == n)(
            functools.partial(select_static, n * SEL_ROWS))

    thr = thr_ref[0:1, :]
    tie = jnp.where((thr_ref[1:2, :] > kf) & (thr > NEG_INF), 1.0, 0.0)

    @pl.when(jnp.max(tie) > 0.0)
    def _():
        def count(pred):
            def body(jj, acc):
                k0 = pl.multiple_of(jj * KCHUNK, KCHUNK)
                hit = jnp.where(pred(st_ref[pl.ds(k0, KCHUNK), :], k0), 1.0, 0.0)
                return acc + jnp.sum(hit.reshape(KCHUNK // SUBLANES, SUBLANES, LANES), axis=0)
            acc = lax.fori_loop(0, n_chunks, body, jnp.zeros((SUBLANES, LANES), F32))
            return jnp.sum(acc, axis=0, keepdims=True)

        def key_index(shape, k0):
            return k0 + lax.broadcasted_iota(jnp.int32, shape, 0)

        need = kf - count(lambda blk, k0: blk > thr)

        def index_pass(p, lo):
            cand = lo + jnp.left_shift(jnp.int32(1), 10 - p)
            n_eq = count(lambda blk, k0: (blk == thr) & (key_index(blk.shape, k0) < cand))
            return jnp.where(n_eq < need, cand, lo)

        lo = lax.fori_loop(0, 11, index_pass, jnp.zeros((1, LANES), jnp.int32))

        def rewrite(jj, carry):
            k0 = pl.multiple_of(jj * KCHUNK, KCHUNK)
            blk = st_ref[pl.ds(k0, KCHUNK), :]
            keep = (blk > thr) | ((blk == thr) & (key_index(blk.shape, k0) <= lo))
            madd_ref[pl.ds(k0, KCHUNK), :] = jnp.where(keep, 0.0, NEG_INF)
            return carry

        lax.fori_loop(0, n_chunks, rewrite, 0)

    att_scale = A_HEAD_DIM ** -0.5
    for h in range(A_HEADS):
        sl = slice(h * Q_BLOCK, (h + 1) * Q_BLOCK)
        q2_ref[sl, :] = (aq_ref[:, sl] * att_scale).astype(BF16)
    m_ref[...] = jnp.full_like(m_ref, NEG_INF)
    l_ref[...] = jnp.zeros_like(l_ref)
    acc_ref[...] = jnp.zeros_like(acc_ref)

    heads_per_att = 4

    def att_chunk(jj, carry):
        k0 = pl.multiple_of(jj * KCHUNK, KCHUNK)
        kc = akb_ref[pl.ds(k0, KCHUNK), :]
        vc = vaug_ref[pl.ds(k0, KCHUNK), :]
        sel = jnp.concatenate([madd_ref[pl.ds(k0, Q_BLOCK), :].T,
                               madd_ref[pl.ds(k0 + Q_BLOCK, Q_BLOCK), :].T], axis=1)
        tq = t0 + lax.broadcasted_iota(jnp.int32, sel.shape, 0)
        sk = k0 + lax.broadcasted_iota(jnp.int32, sel.shape, 1)
        sel = jnp.where(sk <= tq, sel, NEG_INF)
        bidx = jnp.minimum(i - 2 * jj, N_BIAS_TILES - 1)
        for g in range(A_HEADS // heads_per_att):
            rows = slice(g * heads_per_att * Q_BLOCK, (g + 1) * heads_per_att * Q_BLOCK)
            lg = lax.dot_general(q2_ref[rows, :], kc, _NT, preferred_element_type=F32)
            ps = []
            alphas = []
            for hh in range(heads_per_att):
                h = g * heads_per_att + hh
                hs = slice(h * Q_BLOCK, (h + 1) * Q_BLOCK)
                s = lg[hh * Q_BLOCK:(hh + 1) * Q_BLOCK, :] + bias_ref[bidx, h] + sel
                m_old = m_ref[hs, :]
                m_new = jnp.maximum(m_old, jnp.max(s, axis=1, keepdims=True))
                alphas.append(jnp.exp(m_old - m_new))
                m_ref[hs, :] = m_new
                p = jnp.exp(s - jnp.concatenate([m_new, m_new], axis=1))
                ps.append(p.astype(BF16))
            pv = jnp.dot(jnp.concatenate(ps, axis=0), vc, preferred_element_type=F32)
            for hh in range(heads_per_att):
                h = g * heads_per_att + hh
                hs = slice(h * Q_BLOCK, (h + 1) * Q_BLOCK)
                pv_h = pv[hh * Q_BLOCK:(hh + 1) * Q_BLOCK, :]
                acc_ref[hs, :] = alphas[hh] * acc_ref[hs, :] + pv_h[:, :A_HEAD_DIM]
                l_ref[hs, :] = alphas[hh] * l_ref[hs, :] + pv_h[:, A_HEAD_DIM:]
        return carry

    lax.fori_loop(0, n_chunks, att_chunk, 0)

    for h in range(A_HEADS):
        hs = slice(h * Q_BLOCK, (h + 1) * Q_BLOCK)
        o_ref[:, hs] = (acc_ref[hs, :] / l_ref[hs, :]).astype(BF16)


def _dsa(p, rope_a, bias_tiles, batch, seq):
    t = p.shape[0]
    nb = seq // Q_BLOCK
    topk = min(TOPK_MAX, seq // 4)
    assert topk <= KCHUNK and seq % SEL_ROWS == 0 and seq <= 2048
    qrow = lambda b, i: b * nb + i
    stat = pltpu.VMEM((A_HEADS * Q_BLOCK, LANES), F32)
    return pl.pallas_call(
        functools.partial(_dsa_body, topk=topk, seq=seq),
        grid=(batch, nb),
        in_specs=[
            pl.BlockSpec((Q_BLOCK, IDX_HEADS * IDX_HEAD_DIM), lambda b, i: (qrow(b, i), COL_IQ // 4096)),
            pl.BlockSpec((Q_BLOCK, A_HEADS * A_HEAD_DIM), lambda b, i: (qrow(b, i), COL_AQ // 2048)),
            pl.BlockSpec((seq, LANES), lambda b, i: (b, COL_AK // LANES)),
            pl.BlockSpec((seq, LANES), lambda b, i: (b, COL_AV // LANES)),
            pl.BlockSpec((Q_BLOCK, LANES), lambda b, i: (qrow(b, i), COL_IK // LANES)),
            pl.BlockSpec((Q_BLOCK, LANES), lambda b, i: (qrow(b, i), COL_MISC // LANES)),
            pl.BlockSpec((2, Q_BLOCK, LANES), lambda b, i: (0, qrow(b, i), 0)),
            pl.BlockSpec(bias_tiles.shape, lambda b, i: (0, 0, 0, 0)),
        ],
        out_specs=pl.BlockSpec((Q_BLOCK, A_HEADS * A_HEAD_DIM), lambda b, i: (qrow(b, i), 0)),
        out_shape=jax.ShapeDtypeStruct((t, A_HEADS * A_HEAD_DIM), BF16),
        scratch_shapes=[
            pltpu.VMEM((seq, IDX_HEAD_DIM), BF16),
            pltpu.VMEM((seq, A_HEAD_DIM), BF16),
            pltpu.VMEM((seq, 2 * LANES), BF16),
            pltpu.VMEM((IDX_HEADS * Q_BLOCK, IDX_HEAD_DIM), BF16),
            pltpu.VMEM((IDX_HEADS, Q_BLOCK, LANES), F32),
            pltpu.VMEM((seq, Q_BLOCK), F32),
            pltpu.VMEM((seq, Q_BLOCK), F32),
            pltpu.VMEM((A_HEADS * Q_BLOCK, A_HEAD_DIM), BF16),
            stat, stat, stat,
            pltpu.VMEM((SUBLANES, LANES), F32),
        ],
        compiler_params=_cparams(2),
        name="dsa_attention",
    )(p, p, p, p, p, p, rope_a, bias_tiles)


MLA_BLOCK = 512
MLA_HEADS_PER_STEP = 2


def _mla_body(q_ref, kv_ref, misc_ref, ropeq_ref, ropek_ref, o_ref, krr_ref, kc_ref, vaug_ref):
    hp = pl.program_id(1)
    qi = pl.program_id(2)
    head_w = QK_NOPE_DIM + LANES
    seq = kv_ref.shape[0]

    @pl.when((hp == 0) & (qi == 0))
    def _():
        krr_ref[...] = _rope_tile(misc_ref[...], ropek_ref[0], ropek_ref[1]).astype(BF16)

    @pl.when(qi == 0)
    def _():
        for e in range(MLA_HEADS_PER_STEP):
            c0 = e * head_w
            kc_ref[e, :, :QK_NOPE_DIM] = kv_ref[:, c0:c0 + QK_NOPE_DIM]
            kc_ref[e, :, QK_NOPE_DIM:] = krr_ref[...]
            vaug_ref[e, :, :V_HEAD_DIM] = kv_ref[:, c0 + QK_NOPE_DIM:c0 + head_w]
            vaug_ref[e, :, V_HEAD_DIM:] = jnp.ones((seq, LANES), BF16)

    scale = (QK_NOPE_DIM + QK_ROPE_DIM) ** -0.5
    qcs = []
    for e in range(MLA_HEADS_PER_STEP):
        c0 = e * head_w
        q_rope = _rope_tile(q_ref[:, c0 + QK_NOPE_DIM:c0 + head_w], ropeq_ref[0], ropeq_ref[1])
        qc = jnp.concatenate([q_ref[:, c0:c0 + QK_NOPE_DIM], q_rope], axis=1) * scale
        qcs.append(qc.astype(BF16))
    lane_tiles = MLA_BLOCK // LANES

    def step(j, carry, masked):
        k0 = pl.multiple_of(j * MLA_BLOCK, MLA_BLOCK)
        out = []
        for e in range(MLA_HEADS_PER_STEP):
            m_old, l_old, acc = carry[e]
            s = lax.dot_general(qcs[e], kc_ref[e, pl.ds(k0, MLA_BLOCK), :], _NT,
                                preferred_element_type=F32)
            if masked:
                tq = lax.broadcasted_iota(jnp.int32, s.shape, 0)
                sk = lax.broadcasted_iota(jnp.int32, s.shape, 1)
                s = jnp.where(sk <= tq, s, NEG_INF)
            m_new = jnp.maximum(m_old, jnp.max(s, axis=1, keepdims=True))
            alpha = jnp.exp(m_old - m_new)
            p = jnp.exp(s - jnp.concatenate([m_new] * lane_tiles, axis=1))
            pv = jnp.dot(p.astype(BF16), vaug_ref[e, pl.ds(k0, MLA_BLOCK), :],
                         preferred_element_type=F32)
            out.append((m_new, alpha * l_old + pv[:, V_HEAD_DIM:],
                        alpha * acc + pv[:, :V_HEAD_DIM]))
        return tuple(out)

    init = (jnp.full((MLA_BLOCK, LANES), NEG_INF, F32),
            jnp.zeros((MLA_BLOCK, LANES), F32),
            jnp.zeros((MLA_BLOCK, V_HEAD_DIM), F32))
    carry = lax.fori_loop(0, qi, functools.partial(step, masked=False),
                          (init,) * MLA_HEADS_PER_STEP)
    carry = step(qi, carry, masked=True)
    for e in range(MLA_HEADS_PER_STEP):
        _, l_fin, acc = carry[e]
        o_ref[:, e * V_HEAD_DIM:(e + 1) * V_HEAD_DIM] = (acc / l_fin).astype(BF16)


def _mla(q, kv, p, rope_b, batch, seq):
    t = q.shape[0]
    nq = seq // MLA_BLOCK
    head_w = QK_NOPE_DIM + LANES
    step_w = MLA_HEADS_PER_STEP * head_w
    qrow = lambda b, hp, qi: b * nq + qi
    return pl.pallas_call(
        _mla_body,
        grid=(batch, B_HEADS // MLA_HEADS_PER_STEP, nq),
        in_specs=[
            pl.BlockSpec((MLA_BLOCK, step_w), lambda b, hp, qi: (qrow(b, hp, qi), hp)),
            pl.BlockSpec((seq, step_w), lambda b, hp, qi: (b, hp)),
            pl.BlockSpec((seq, LANES), lambda b, hp, qi: (b, COL_MISC // LANES)),
            pl.BlockSpec((2, MLA_BLOCK, LANES), lambda b, hp, qi: (0, qrow(b, hp, qi), 0)),
            pl.BlockSpec((2, seq, LANES), lambda b, hp, qi: (0, b, 0)),
        ],
        out_specs=pl.BlockSpec((MLA_BLOCK, MLA_HEADS_PER_STEP * V_HEAD_DIM),
                               lambda b, hp, qi: (qrow(b, hp, qi), hp)),
        out_shape=jax.ShapeDtypeStruct((t, B_HEADS * V_HEAD_DIM), BF16),
        scratch_shapes=[
            pltpu.VMEM((seq, LANES), BF16),
            pltpu.VMEM((MLA_HEADS_PER_STEP, seq, head_w), BF16),
            pltpu.VMEM((MLA_HEADS_PER_STEP, seq, head_w), BF16)],
        compiler_params=_cparams(3),
        name="mla_attention",
    )(q, kv, p, rope_b, rope_b)


def _rope_tables(positions):
    half = IDX_ROPE_DIM // 2
    freqs = ROPE_THETA ** (-jnp.arange(half, dtype=F32) / half)
    ang = positions.astype(F32).reshape(-1, 1) * freqs
    cos, sin = jnp.cos(ang), jnp.sin(ang)
    one, zero = jnp.ones_like(cos), jnp.zeros_like(cos)
    cat = lambda parts: jnp.concatenate(parts, axis=1)
    rope_a = jnp.stack([cat([one, cos, one, cos]), cat([zero, -sin, zero, sin])])
    rope_b = jnp.stack([cat([cos, zero, cos, zero]), cat([-sin, zero, sin, zero])])
    return rope_a, rope_b


def _spread_rope_pairs(w, heads):
    rows = w.shape[0]
    g = IDX_ROPE_DIM // 2
    w5 = w.reshape(rows, heads, 2, 2, g)
    return jnp.transpose(w5, (0, 1, 3, 2, 4)).reshape(rows, heads * 4 * g)


def _reorder_w_in(w):
    splits = (A_HEADS * A_HEAD_DIM, A_HEAD_DIM, A_HEAD_DIM, IDX_HEADS * IDX_HEAD_DIM,
              IDX_HEAD_DIM, IDX_HEADS, Q_LORA_RANK, KV_LORA_RANK, QK_ROPE_DIM)
    offs = np.concatenate([[0], np.cumsum(splits)])
    aq, ak, av, iq, ik, iw, ql, kvl, kr = [w[:, int(offs[n]):int(offs[n + 1])] for n in range(9)]
    g = QK_ROPE_DIM // 2
    pad = jnp.zeros((w.shape[0], LANES - QK_ROPE_DIM - IDX_HEADS), w.dtype)
    misc = [kr[:, :g], iw, kr[:, g:], pad]
    out = jnp.concatenate([_spread_rope_pairs(iq, IDX_HEADS), aq, ql, kvl, ak, av,
                           _spread_rope_pairs(ik, 1)] + misc, axis=1).astype(BF16)
    assert out.shape[1] == P_WIDTH and MISC_IW == g
    return out


def _reorder_w_uq(w):
    r = w.shape[0]
    g = QK_ROPE_DIM // 2
    w3 = w.reshape(r, B_HEADS, QK_NOPE_DIM + QK_ROPE_DIM)
    zero = jnp.zeros((r, B_HEADS, g), w.dtype)
    w3 = jnp.concatenate([w3[:, :, :QK_NOPE_DIM], w3[:, :, QK_NOPE_DIM:QK_NOPE_DIM + g], zero,
                          w3[:, :, QK_NOPE_DIM + g:], zero], axis=2)
    return w3.reshape(r, B_HEADS * 2 * LANES)


def kernel(x, c, positions, w_ada, b_ada, w_in, rel_bias, q_norm_g, w_uq, kv_norm_g, w_ukv,
           w_o, ln1_g, ln1_b, w_gate, w_up, conv_w, conv_b, w_down, ln2_g, ln2_b):
    batch, seq, d = x.shape
    depth = w_ada.shape[0]
    t = batch * seq
    assert d == D_MODEL and batch <= SUBLANES and seq % 1024 == 0
    alpha = (2 * depth) ** 0.25

    rope_a, rope_b = _rope_tables(positions)
    bias_tiles = _bias_tiles(rel_bias)
    c8 = jnp.zeros((SUBLANES, d), F32).at[:batch].set(c)
    x2 = x.reshape(t, d)

    for l in range(depth):
        mod = _ada(c8, w_ada[l], b_ada[l])
        mod3 = mod[:batch].reshape(batch * 6, 1, d)

        u = _modulate(x2, mod3, seq, shift_idx=0, scale_idx=1)
        p = _matmul(u, _reorder_w_in(w_in[l]), F32, 1024, 512, "in_proj")
        y_a = _dsa(p, rope_a, bias_tiles, batch, seq)
        q_lat = _rmsnorm_cols(p, COL_QL, Q_LORA_RANK, q_norm_g[l], "q_rmsnorm")
        kv_lat = _rmsnorm_cols(p, COL_KVL, KV_LORA_RANK, kv_norm_g[l], "kv_rmsnorm")
        q = _matmul(q_lat, _reorder_w_uq(w_uq[l]), F32, 1024, 512, "q_up_proj")
        kv = _matmul(kv_lat, w_ukv[l], BF16, 1024, 512, "kv_up_proj")
        y_b = _mla(q, kv, p, rope_b, batch, seq)
        mix = _matmul_concat(y_a, y_b, w_o[l], 1024, 512, "out_proj")
        x2, u = _resln(x2, mix, mod3, seq, 2, ln1_g[l], ln1_b[l], alpha, mod_idx=(3, 4))

        hidden = _ffn_gate_up(u, w_gate[l], w_up[l], conv_w[l], conv_b[l], seq)
        y = _ffn_down(hidden, w_down[l])
        x2 = _resln(x2, y, mod3, seq, 5, ln2_g[l], ln2_b[l], alpha)

    return x2.reshape(batch, seq, d)
```

```python
import functools
import math

import jax
import jax.numpy as jnp
import numpy as np
from jax import lax
from jax.experimental import pallas as pl
from jax.experimental.pallas import tpu as pltpu

F32 = jnp.float32
BF16 = jnp.bfloat16

D_MODEL = 4096
A_HEAD_DIM = 128
A_HEADS = 16
IDX_HEADS = 32
IDX_HEAD_DIM = 128
IDX_ROPE_DIM = 64
TOPK_MAX = 256
V_HEAD_DIM = 128
B_HEADS = 16
Q_LORA_RANK = 1024
KV_LORA_RANK = 512
QK_NOPE_DIM = 128
QK_ROPE_DIM = 64
D_FF = 11008
CONV_WIDTH = 3
REL_BUCKETS = 32
REL_MAX_DIST = 128
ROPE_THETA = 10000.0
Q_BLOCK = 128
LN_EPS = 1e-5
RMS_EPS = 1e-6
NEG_INF = -1e30

LANES = 128
SUBLANES = 8
VMEM_LIMIT_BYTES = 56 * 1024 * 1024

P_WIDTH = 8192
COL_IQ = 0
COL_AQ = 4096
COL_QL = 6144
COL_KVL = 7168
COL_AK = 7680
COL_AV = 7808
COL_IK = 7936
COL_MISC = 8064
MISC_IW = 32

_NT = (((1,), (1,)), ((), ()))


def _cparams(n_axes):
    return pltpu.CompilerParams(
        dimension_semantics=("arbitrary",) * n_axes,
        vmem_limit_bytes=VMEM_LIMIT_BYTES)


def _ada_body(c_ref, w_ref, b_ref, o_ref):
    c = c_ref[...]
    act = (c / (1.0 + jnp.exp(-c))).astype(BF16)
    o_ref[...] = jnp.dot(act, w_ref[...].astype(BF16),
                         preferred_element_type=F32) + b_ref[...]


def _ada(c8, w, b):
    d, n = w.shape
    tn = 512
    return pl.pallas_call(
        _ada_body,
        grid=(n // tn,),
        in_specs=[pl.BlockSpec((SUBLANES, d), lambda j: (0, 0)),
                  pl.BlockSpec((d, tn), lambda j: (0, j)),
                  pl.BlockSpec((1, tn), lambda j: (0, j))],
        out_specs=pl.BlockSpec((SUBLANES, tn), lambda j: (0, j)),
        out_shape=jax.ShapeDtypeStruct((SUBLANES, n), F32),
        compiler_params=_cparams(1),
        name="ada_proj",
    )(c8, w, b.reshape(1, n))


def _modulate_body(x_ref, sh_ref, sc_ref, o_ref):
    o_ref[...] = (x_ref[...] * (1.0 + sc_ref[0]) + sh_ref[0]).astype(BF16)


def _modulate(x2, mod3, seq, shift_idx, scale_idx):
    t, d = x2.shape
    tm = 256
    per_seq = seq // tm
    return pl.pallas_call(
        _modulate_body,
        grid=(t // tm,),
        in_specs=[pl.BlockSpec((tm, d), lambda i: (i, 0)),
                  pl.BlockSpec((1, 1, d), lambda i: (6 * (i // per_seq) + shift_idx, 0, 0)),
                  pl.BlockSpec((1, 1, d), lambda i: (6 * (i // per_seq) + scale_idx, 0, 0))],
        out_specs=pl.BlockSpec((tm, d), lambda i: (i, 0)),
        out_shape=jax.ShapeDtypeStruct((t, d), BF16),
        compiler_params=_cparams(1),
        name="modulate",
    )(x2, mod3, mod3)


def _mm_body(a_ref, w_ref, o_ref):
    o_ref[...] = jnp.dot(a_ref[...], w_ref[...].astype(BF16),
                         preferred_element_type=F32).astype(o_ref.dtype)


def _matmul(a, w, out_dtype, tm, tn, name):
    m, k = a.shape
    n = w.shape[1]
    return pl.pallas_call(
        _mm_body,
        grid=(m // tm, n // tn),
        in_specs=[pl.BlockSpec((tm, k), lambda i, j: (i, 0)),
                  pl.BlockSpec((k, tn), lambda i, j: (0, j))],
        out_specs=pl.BlockSpec((tm, tn), lambda i, j: (i, j)),
        out_shape=jax.ShapeDtypeStruct((m, n), out_dtype),
        compiler_params=_cparams(2),
        name=name,
    )(a, w)


def _mm2_body(a0_ref, a1_ref, w0_ref, w1_ref, o_ref):
    acc = jnp.dot(a0_ref[...], w0_ref[...].astype(BF16), preferred_element_type=F32)
    acc = acc + jnp.dot(a1_ref[...], w1_ref[...].astype(BF16), preferred_element_type=F32)
    o_ref[...] = acc


def _matmul_concat(a0, a1, w, tm, tn, name):
    m, k0 = a0.shape
    k1 = a1.shape[1]
    assert k0 == k1
    n = w.shape[1]
    return pl.pallas_call(
        _mm2_body,
        grid=(m // tm, n // tn),
        in_specs=[pl.BlockSpec((tm, k0), lambda i, j: (i, 0)),
                  pl.BlockSpec((tm, k1), lambda i, j: (i, 0)),
                  pl.BlockSpec((k0, tn), lambda i, j: (0, j)),
                  pl.BlockSpec((k1, tn), lambda i, j: (1, j))],
        out_specs=pl.BlockSpec((tm, tn), lambda i, j: (i, j)),
        out_shape=jax.ShapeDtypeStruct((m, n), F32),
        compiler_params=_cparams(2),
        name=name,
    )(a0, a1, w, w)


def _rms_body(x_ref, g_ref, o_ref):
    x = x_ref[...]
    ms = jnp.mean(x * x, axis=-1, keepdims=True)
    o_ref[...] = (x * lax.rsqrt(ms + RMS_EPS) * g_ref[...]).astype(BF16)


def _rmsnorm_cols(p, col, width, gain, name):
    t = p.shape[0]
    tm = 512
    cb = col // width
    return pl.pallas_call(
        _rms_body,
        grid=(t // tm,),
        in_specs=[pl.BlockSpec((tm, width), lambda i: (i, cb)),
                  pl.BlockSpec((1, width), lambda i: (0, 0))],
        out_specs=pl.BlockSpec((tm, width), lambda i: (i, 0)),
        out_shape=jax.ShapeDtypeStruct((t, width), BF16),
        compiler_params=_cparams(1),
        name=name,
    )(p, gain.reshape(1, width))


def _resln_body(x_ref, y_ref, gate_ref, lg_ref, lb_ref, *rest, alpha, with_mod):
    z = alpha * x_ref[...] + (1.0 + gate_ref[0]) * y_ref[...]
    mu = jnp.mean(z, axis=-1, keepdims=True)
    zc = z - mu
    var = jnp.mean(zc * zc, axis=-1, keepdims=True)
    out = zc * lax.rsqrt(var + LN_EPS) * lg_ref[...] + lb_ref[...]
    if with_mod:
        sh_ref, sc_ref, o_ref, u_ref = rest
        u_ref[...] = (out * (1.0 + sc_ref[0]) + sh_ref[0]).astype(BF16)
    else:
        (o_ref,) = rest
    o_ref[...] = out


def _resln(x2, y, mod3, seq, gate_idx, ln_g, ln_b, alpha, mod_idx=None):
    t, d = x2.shape
    tm = 256
    per_seq = seq // tm
    with_mod = mod_idx is not None
    row = pl.BlockSpec((tm, d), lambda i: (i, 0))
    vec = pl.BlockSpec((1, d), lambda i: (0, 0))

    def mod_spec(idx):
        return pl.BlockSpec((1, 1, d), lambda i: (6 * (i // per_seq) + idx, 0, 0))

    in_specs = [row, row, mod_spec(gate_idx), vec, vec]
    args = [x2, y, mod3, ln_g.reshape(1, d), ln_b.reshape(1, d)]
    out_specs = row
    out_shape = jax.ShapeDtypeStruct((t, d), F32)
    if with_mod:
        in_specs += [mod_spec(mod_idx[0]), mod_spec(mod_idx[1])]
        args += [mod3, mod3]
        out_specs = [row, row]
        out_shape = [out_shape, jax.ShapeDtypeStruct((t, d), BF16)]
    return pl.pallas_call(
        functools.partial(_resln_body, alpha=alpha, with_mod=with_mod),
        grid=(t // tm,),
        in_specs=in_specs,
        out_specs=out_specs,
        out_shape=out_shape,
        compiler_params=_cparams(1),
        name="residual_ln",
    )(*args)


def _ffn_gu_body(a_ref, halo_ref, wg_ref, wu_ref, cw_ref, cb_ref, o_ref, *, tiles_per_seq):
    i = pl.program_id(0)
    a = a_ref[...]
    wg = wg_ref[...].astype(BF16)
    wu = wu_ref[...].astype(BF16)
    g = jnp.dot(a, wg, preferred_element_type=F32)
    up = jnp.dot(a, wu, preferred_element_type=F32)
    gh = jnp.dot(halo_ref[...], wg, preferred_element_type=F32)
    gh = jnp.where(i % tiles_per_seq == 0, 0.0, gh)
    prev1 = gh[SUBLANES - 1:SUBLANES, :]
    prev2 = gh[SUBLANES - 2:SUBLANES - 1, :]
    row = lax.broadcasted_iota(jnp.int32, g.shape, 0)
    g1 = jnp.where(row == 0, prev1, pltpu.roll(g, 1, 0))
    g2 = jnp.where(row == 0, prev2, jnp.where(row == 1, prev1, pltpu.roll(g, 2, 0)))
    cw = cw_ref[...]
    conv = cb_ref[...] + cw[0:1, :] * g2 + cw[1:2, :] * g1 + cw[2:3, :] * g
    o_ref[...] = (conv / (1.0 + jnp.exp(-conv)) * up).astype(BF16)


def _ffn_gate_up(u, w_gate, w_up, conv_w, conv_b, seq):
    t, d = u.shape
    f = w_gate.shape[1]
    tm, tn = 1024, 256
    halo_blocks = tm // SUBLANES
    return pl.pallas_call(
        functools.partial(_ffn_gu_body, tiles_per_seq=seq // tm),
        grid=(t // tm, f // tn),
        in_specs=[pl.BlockSpec((tm, d), lambda i, j: (i, 0)),
                  pl.BlockSpec((SUBLANES, d), lambda i, j: (jnp.maximum(i * halo_blocks - 1, 0), 0)),
                  pl.BlockSpec((d, tn), lambda i, j: (0, j)),
                  pl.BlockSpec((d, tn), lambda i, j: (0, j)),
                  pl.BlockSpec((CONV_WIDTH, tn), lambda i, j: (0, j)),
                  pl.BlockSpec((1, tn), lambda i, j: (0, j))],
        out_specs=pl.BlockSpec((tm, tn), lambda i, j: (i, j)),
        out_shape=jax.ShapeDtypeStruct((t, f), BF16),
        compiler_params=_cparams(2),
        name="ffn_gate_up",
    )(u, u, w_gate, w_up, conv_w, conv_b.reshape(1, f))


def _ffn_down_body(am_ref, at_ref, wm_ref, wt_ref, o_ref, acc_ref, *, n_main, n_tail):
    k = pl.program_id(2)

    @pl.when(k == 0)
    def _():
        acc_ref[...] = jnp.zeros_like(acc_ref)

    @pl.when(k < n_main)
    def _():
        acc_ref[...] += jnp.dot(am_ref[...], wm_ref[...].astype(BF16),
                                preferred_element_type=F32)

    @pl.when(k >= n_main)
    def _():
        acc_ref[...] += jnp.dot(at_ref[...], wt_ref[...].astype(BF16),
                                preferred_element_type=F32)

    @pl.when(k == n_main + n_tail - 1)
    def _():
        o_ref[...] = acc_ref[...]


def _ffn_down(h, w_down):
    t, f = h.shape
    d = w_down.shape[1]
    tm, tn, tk, tk_tail = 2048, 1024, 1024, 256
    n_main = f // tk
    n_tail = (f - n_main * tk) // tk_tail
    assert n_main * tk + n_tail * tk_tail == f and n_tail > 0
    tail0 = n_main * tk // tk_tail

    def main_k(k):
        return jnp.minimum(k, n_main - 1)

    def tail_k(k):
        return tail0 + jnp.maximum(k - n_main, 0)

    return pl.pallas_call(
        functools.partial(_ffn_down_body, n_main=n_main, n_tail=n_tail),
        grid=(t // tm, d // tn, n_main + n_tail),
        in_specs=[pl.BlockSpec((tm, tk), lambda i, j, k: (i, main_k(k))),
                  pl.BlockSpec((tm, tk_tail), lambda i, j, k: (i, tail_k(k))),
                  pl.BlockSpec((tk, tn), lambda i, j, k: (main_k(k), j)),
                  pl.BlockSpec((tk_tail, tn), lambda i, j, k: (tail_k(k), j))],
        out_specs=pl.BlockSpec((tm, tn), lambda i, j, k: (i, j)),
        out_shape=jax.ShapeDtypeStruct((t, d), F32),
        scratch_shapes=[pltpu.VMEM((tm, tn), F32)],
        compiler_params=_cparams(3),
        name="ffn_down",
    )(h, h, w_down, w_down)


ROPE_HALF = IDX_ROPE_DIM // 2
ROLLS_SPLIT = (LANES - ROPE_HALF, ROPE_HALF)
ROLLS_SPREAD = (LANES // 2,)


def _rope_tile(x, tabs, shifts):
    out = x * tabs[0]
    for n, shift in enumerate(shifts):
        out = out + pltpu.roll(x, shift, 1) * tabs[1 + n]
    return out


KCHUNK = 2 * Q_BLOCK
N_BIAS_TILES = 4
LOG2E = math.log2(math.e)


def _bias_tiles_body(rb_ref, o_ref):
    o = pl.program_id(0)
    tq = lax.broadcasted_iota(jnp.int32, (Q_BLOCK, KCHUNK), 0)
    sk = lax.broadcasted_iota(jnp.int32, (Q_BLOCK, KCHUNK), 1)
    n = jnp.maximum(o * Q_BLOCK + tq - sk, 0)
    max_exact = REL_BUCKETS // 2
    nf = jnp.maximum(n, 1).astype(F32)
    large = max_exact + (jnp.log(nf / max_exact) / math.log(REL_MAX_DIST / max_exact)
                         * (REL_BUCKETS - max_exact)).astype(jnp.int32)
    large = jnp.minimum(large, REL_BUCKETS - 1)
    bucket = jnp.where(n < max_exact, n, large)
    hits = [bucket == b for b in range(REL_BUCKETS - 1)]
    for h in range(A_HEADS):
        far = rb_ref[REL_BUCKETS - 1, h]
        tile = jnp.zeros((Q_BLOCK, KCHUNK), F32)
        for b in range(REL_BUCKETS - 1):
            tile = jnp.where(hits[b], (rb_ref[b, h] - far) * LOG2E, tile)
        o_ref[0, h] = tile


def _bias_tiles(rel_bias):
    return pl.pallas_call(
        _bias_tiles_body,
        grid=(N_BIAS_TILES,),
        in_specs=[pl.BlockSpec(memory_space=pltpu.SMEM)],
        out_specs=pl.BlockSpec((1, A_HEADS, Q_BLOCK, KCHUNK), lambda o: (o, 0, 0, 0)),
        out_shape=jax.ShapeDtypeStruct((N_BIAS_TILES, A_HEADS, Q_BLOCK, KCHUNK), F32),
        compiler_params=_cparams(1),
        name="rel_bias_tiles",
    )(rel_bias)


SEL_ROWS = 512
SEL_SLAB = 64


def _dsa_body(iq_ref, aq_ref, ak_ref, av_ref, ik_ref, misc_ref, rope_ref, bias_ref,
              o_ref,
              ikr_ref, kaug_ref, vaug_ref, iq2_ref, wb_ref, st_ref, q2_ref,
              m_ref, l_ref, acc_ref, thr_ref, lg0_ref, lg1_ref, alpha_ref, p_ref,
              *, topk, seq):
    i = pl.program_id(1)
    n_chunks = i // 2 + 1
    t0 = i * Q_BLOCK
    rope = (rope_ref[0], rope_ref[1], rope_ref[2])

    @pl.when(i == 0)
    def _():
        ikr_ref[...] = jnp.zeros_like(ikr_ref)
        kaug_ref[:, :A_HEAD_DIM] = ak_ref[...].astype(BF16)
        vaug_ref[:, :A_HEAD_DIM] = av_ref[...].astype(BF16)
        vaug_ref[:, A_HEAD_DIM:] = jnp.ones((seq, LANES), BF16)
        eye = jnp.where(lax.broadcasted_iota(jnp.int32, (Q_BLOCK, LANES), 0)
                        == lax.broadcasted_iota(jnp.int32, (Q_BLOCK, LANES), 1), 1.0, 0.0)
        for h in range(A_HEADS):
            q2_ref[h * Q_BLOCK:(h + 1) * Q_BLOCK, A_HEAD_DIM:] = eye.astype(BF16)

    ikr_ref[pl.ds(pl.multiple_of(t0, Q_BLOCK), Q_BLOCK), :] = _rope_tile(
        ik_ref[...], rope, ROLLS_SPLIT).astype(BF16)

    for h in range(IDX_HEADS):
        sl = slice(h * Q_BLOCK, (h + 1) * Q_BLOCK)
        iq2_ref[sl, :] = _rope_tile(iq_ref[:, sl], rope, ROLLS_SPLIT).astype(BF16)

    w_scale = (IDX_HEADS ** -0.5) * (IDX_HEAD_DIM ** -0.5)
    for h in range(IDX_HEADS):
        col = misc_ref[:, MISC_IW + h:MISC_IW + h + 1] * w_scale
        wb_ref[h] = jnp.broadcast_to(col, (Q_BLOCK, LANES))

    heads_per_dot = 4

    def score_chunk(jj, carry):
        k0 = pl.multiple_of(jj * KCHUNK, KCHUNK)
        kc = ikr_ref[pl.ds(k0, KCHUNK), :]
        sc = jnp.zeros((Q_BLOCK, KCHUNK), F32)
        for g in range(IDX_HEADS // heads_per_dot):
            rows = slice(g * heads_per_dot * Q_BLOCK, (g + 1) * heads_per_dot * Q_BLOCK)
            d = lax.dot_general(iq2_ref[rows, :], kc, _NT, preferred_element_type=F32)
            for hh in range(heads_per_dot):
                w = wb_ref[g * heads_per_dot + hh]
                dh = jnp.maximum(d[hh * Q_BLOCK:(hh + 1) * Q_BLOCK, :], 0.0)
                sc = sc + dh * jnp.concatenate([w, w], axis=1)
        tq = t0 + lax.broadcasted_iota(jnp.int32, sc.shape, 0)
        sk = k0 + lax.broadcasted_iota(jnp.int32, sc.shape, 1)
        sc = jnp.where(sk <= tq, sc, NEG_INF)
        st_ref[pl.ds(k0, Q_BLOCK), :] = sc[:, :Q_BLOCK].T
        st_ref[pl.ds(k0 + Q_BLOCK, Q_BLOCK), :] = sc[:, Q_BLOCK:].T
        return carry

    lax.fori_loop(0, n_chunks, score_chunk, 0)

    @pl.when(n_chunks % (SEL_ROWS // KCHUNK) == 1)
    def _():
        pad0 = pl.multiple_of(n_chunks * KCHUNK, KCHUNK)
        st_ref[pl.ds(pad0, KCHUNK), :] = jnp.full((KCHUNK, Q_BLOCK), NEG_INF, F32)

    def key_to_f32(key):
        bits = key ^ ((key >> 31) & jnp.int32(0x7FFFFFFF))
        return lax.bitcast_convert_type(bits, F32)

    kf = float(topk)

    def write_mask(k0, rows, keep_fn):
        blk = st_ref[pl.ds(k0, rows), :]
        sk = k0 + lax.broadcasted_iota(jnp.int32, blk.shape, 0)
        tq = t0 + lax.broadcasted_iota(jnp.int32, blk.shape, 1)
        keep = keep_fn(blk, sk) & (sk <= tq)
        kaug_ref[pl.ds(k0, rows), A_HEAD_DIM:] = jnp.where(keep, 0.0, NEG_INF).astype(BF16)

    def select_static(rows):
        def count_ge(thr_row):
            thr_b = jnp.broadcast_to(thr_row, (SEL_SLAB, LANES))
            acc = jnp.zeros((SEL_SLAB, LANES), F32)
            for r in range(rows // SEL_SLAB):
                blk = st_ref[r * SEL_SLAB:(r + 1) * SEL_SLAB, :]
                acc = jnp.where(blk >= thr_b, acc + 1.0, acc)
            acc = jnp.sum(acc.reshape(SEL_SLAB // SUBLANES, SUBLANES, LANES), axis=0)
            return jnp.sum(acc, axis=0, keepdims=True)

        def select_pass(p, key):
            cand = key + jnp.left_shift(jnp.int32(1), 31 - p)
            return jnp.where(count_ge(key_to_f32(cand)) >= kf, cand, key)

        key0 = jnp.full((1, LANES), jnp.iinfo(jnp.int32).min, jnp.int32)
        thr = key_to_f32(lax.fori_loop(0, 32, select_pass, key0))
        thr_ref[0:1, :] = thr
        thr_ref[1:2, :] = count_ge(thr)
        write_mask(0, rows, lambda blk, sk: blk >= thr)

    for n in range(1, seq // SEL_ROWS + 1):
        pl.when(i // (SEL_ROWS // Q_BLOCK) + 1 == n)(
            functools.partial(select_static, n * SEL_ROWS))

    thr = thr_ref[0:1, :]
    tie = jnp.where((thr_ref[1:2, :] > kf) & (thr > NEG_INF), 1.0, 0.0)

    @pl.when(jnp.max(tie) > 0.0)
    def _():
        def count(pred):
            def body(jj, acc):
                k0 = pl.multiple_of(jj * KCHUNK, KCHUNK)
                hit = jnp.where(pred(st_ref[pl.ds(k0, KCHUNK), :], k0), 1.0, 0.0)
                return acc + jnp.sum(hit.reshape(KCHUNK // SUBLANES, SUBLANES, LANES), axis=0)
            acc = lax.fori_loop(0, n_chunks, body, jnp.zeros((SUBLANES, LANES), F32))
            return jnp.sum(acc, axis=0, keepdims=True)

        def key_index(shape, k0):
            return k0 + lax.broadcasted_iota(jnp.int32, shape, 0)

        need = kf - count(lambda blk, k0: blk > thr)

        def index_pass(p, lo):
            cand = lo + jnp.left_shift(jnp.int32(1), 10 - p)
            n_eq = count(lambda blk, k0: (blk == thr) & (key_index(blk.shape, k0) < cand))
            return jnp.where(n_eq < need, cand, lo)

        lo = lax.fori_loop(0, 11, index_pass, jnp.zeros((1, LANES), jnp.int32))

        def rewrite(jj, carry):
            write_mask(pl.multiple_of(jj * KCHUNK, KCHUNK), KCHUNK,
                       lambda blk, sk: (blk > thr) | ((blk == thr) & (sk <= lo)))
            return carry

        lax.fori_loop(0, n_chunks, rewrite, 0)

    q_scale = (A_HEAD_DIM ** -0.5) * LOG2E
    for h in range(A_HEADS):
        sl = slice(h * Q_BLOCK, (h + 1) * Q_BLOCK)
        q2_ref[sl, :A_HEAD_DIM] = (aq_ref[:, sl] * q_scale).astype(BF16)
    m_ref[...] = jnp.full_like(m_ref, NEG_INF)
    l_ref[...] = jnp.zeros_like(l_ref)
    acc_ref[...] = jnp.zeros_like(acc_ref)

    heads_per_att = 2
    head_groups = A_HEADS // heads_per_att
    n_sel_chunks = (i // (SEL_ROWS // Q_BLOCK) + 1) * (SEL_ROWS // KCHUNK)

    def logits(jj, lg_ref):
        jc = jnp.minimum(jj, n_sel_chunks - 1)
        k0 = pl.multiple_of(jc * KCHUNK, KCHUNK)
        kc = kaug_ref[pl.ds(k0, KCHUNK), :]
        bidx = jnp.clip(i - 2 * jc, 0, N_BIAS_TILES - 1)
        for g in range(head_groups):
            rows = slice(g * heads_per_att * Q_BLOCK, (g + 1) * heads_per_att * Q_BLOCK)
            bias = jnp.concatenate([bias_ref[bidx, g * heads_per_att + hh]
                                    for hh in range(heads_per_att)], axis=0)
            lg_ref[rows, :] = bias + lax.dot_general(q2_ref[rows, :], kc, _NT,
                                                     preferred_element_type=F32)

    def softmax_pv(jj, lg_ref):
        k0 = pl.multiple_of(jj * KCHUNK, KCHUNK)
        for h in range(A_HEADS):
            hs = slice(h * Q_BLOCK, (h + 1) * Q_BLOCK)
            m_old = m_ref[hs, :]
            m_new = jnp.maximum(m_old, jnp.max(lg_ref[hs, :], axis=1, keepdims=True))
            alpha_ref[hs, :] = jnp.exp2(m_old - m_new)
            m_ref[hs, :] = m_new
        for h in range(A_HEADS):
            hs = slice(h * Q_BLOCK, (h + 1) * Q_BLOCK)
            m_new = m_ref[hs, :]
            p_ref[hs, :] = jnp.exp2(lg_ref[hs, :] - jnp.concatenate([m_new, m_new], axis=1)
                                    ).astype(BF16)
        vc = vaug_ref[pl.ds(k0, KCHUNK), :]
        for g in range(head_groups):
            rows = slice(g * heads_per_att * Q_BLOCK, (g + 1) * heads_per_att * Q_BLOCK)
            pv = jnp.dot(p_ref[rows, :], vc, preferred_element_type=F32)
            alpha = alpha_ref[rows, :]
            acc_ref[rows, :] = alpha * acc_ref[rows, :] + pv[:, :A_HEAD_DIM]
            l_ref[rows, :] = alpha * l_ref[rows, :] + pv[:, A_HEAD_DIM:]

    def att_pair(pj, carry):
        logits(2 * pj + 1, lg1_ref)
        softmax_pv(2 * pj, lg0_ref)
        logits(2 * pj + 2, lg0_ref)
        softmax_pv(2 * pj + 1, lg1_ref)
        return carry

    logits(0, lg0_ref)
    lax.fori_loop(0, n_sel_chunks // 2, att_pair, 0)

    for h in range(A_HEADS):
        hs = slice(h * Q_BLOCK, (h + 1) * Q_BLOCK)
        o_ref[:, hs] = (acc_ref[hs, :] / l_ref[hs, :]).astype(BF16)


def _dsa(p, rope_a, bias_tiles, batch, seq):
    t = p.shape[0]
    nb = seq // Q_BLOCK
    topk = min(TOPK_MAX, seq // 4)
    assert topk <= KCHUNK and seq % SEL_ROWS == 0 and seq <= 2048
    qrow = lambda b, i: b * nb + i
    stat = pltpu.VMEM((A_HEADS * Q_BLOCK, LANES), F32)
    return pl.pallas_call(
        functools.partial(_dsa_body, topk=topk, seq=seq),
        grid=(batch, nb),
        in_specs=[
            pl.BlockSpec((Q_BLOCK, IDX_HEADS * IDX_HEAD_DIM), lambda b, i: (qrow(b, i), COL_IQ // 4096)),
            pl.BlockSpec((Q_BLOCK, A_HEADS * A_HEAD_DIM), lambda b, i: (qrow(b, i), COL_AQ // 2048)),
            pl.BlockSpec((seq, LANES), lambda b, i: (b, COL_AK // LANES)),
            pl.BlockSpec((seq, LANES), lambda b, i: (b, COL_AV // LANES)),
            pl.BlockSpec((Q_BLOCK, LANES), lambda b, i: (qrow(b, i), COL_IK // LANES)),
            pl.BlockSpec((Q_BLOCK, LANES), lambda b, i: (qrow(b, i), COL_MISC // LANES)),
            pl.BlockSpec((3, Q_BLOCK, LANES), lambda b, i: (0, qrow(b, i), 0)),
            pl.BlockSpec(bias_tiles.shape, lambda b, i: (0, 0, 0, 0)),
        ],
        out_specs=pl.BlockSpec((Q_BLOCK, A_HEADS * A_HEAD_DIM), lambda b, i: (qrow(b, i), 0)),
        out_shape=jax.ShapeDtypeStruct((t, A_HEADS * A_HEAD_DIM), BF16),
        scratch_shapes=[
            pltpu.VMEM((seq, IDX_HEAD_DIM), BF16),
            pltpu.VMEM((seq, A_HEAD_DIM + Q_BLOCK), BF16),
            pltpu.VMEM((seq, 2 * LANES), BF16),
            pltpu.VMEM((IDX_HEADS * Q_BLOCK, IDX_HEAD_DIM), BF16),
            pltpu.VMEM((IDX_HEADS, Q_BLOCK, LANES), F32),
            pltpu.VMEM((seq, Q_BLOCK), F32),
            pltpu.VMEM((A_HEADS * Q_BLOCK, A_HEAD_DIM + Q_BLOCK), BF16),
            stat, stat, stat,
            pltpu.VMEM((SUBLANES, LANES), F32),
            pltpu.VMEM((A_HEADS * Q_BLOCK, KCHUNK), F32),
            pltpu.VMEM((A_HEADS * Q_BLOCK, KCHUNK), F32),
            stat,
            pltpu.VMEM((A_HEADS * Q_BLOCK, KCHUNK), BF16),
        ],
        compiler_params=_cparams(2),
        name="dsa_attention",
    )(p, p, p, p, p, p, rope_a, bias_tiles)


MLA_BLOCK = 512
MLA_HEADS_PER_STEP = 4


def _mla_body(q_ref, kv_ref, misc_ref, ropeq_ref, ropek_ref, o_ref, krr_ref, kc_ref, vaug_ref):
    hp = pl.program_id(1)
    qi = pl.program_id(2)
    head_w = QK_NOPE_DIM + LANES
    seq = kv_ref.shape[0]

    @pl.when((hp == 0) & (qi == 0))
    def _():
        krr_ref[...] = _rope_tile(misc_ref[...], (ropek_ref[0], ropek_ref[1]),
                                  ROLLS_SPREAD).astype(BF16)

    @pl.when(qi == 0)
    def _():
        for e in range(MLA_HEADS_PER_STEP):
            c0 = e * head_w
            kc_ref[e, :, :QK_NOPE_DIM] = kv_ref[:, c0:c0 + QK_NOPE_DIM]
            kc_ref[e, :, QK_NOPE_DIM:] = krr_ref[...]
            vaug_ref[e, :, :V_HEAD_DIM] = kv_ref[:, c0 + QK_NOPE_DIM:c0 + head_w]
            vaug_ref[e, :, V_HEAD_DIM:] = jnp.ones((seq, LANES), BF16)

    scale = (QK_NOPE_DIM + QK_ROPE_DIM) ** -0.5 * LOG2E
    qcs = []
    for e in range(MLA_HEADS_PER_STEP):
        c0 = e * head_w
        q_rope = _rope_tile(q_ref[:, c0 + QK_NOPE_DIM:c0 + head_w],
                            (ropeq_ref[0], ropeq_ref[1]), ROLLS_SPREAD)
        qc = jnp.concatenate([q_ref[:, c0:c0 + QK_NOPE_DIM], q_rope], axis=1) * scale
        qcs.append(qc.astype(BF16))
    lane_tiles = MLA_BLOCK // LANES

    def step(j, carry, masked):
        k0 = pl.multiple_of(j * MLA_BLOCK, MLA_BLOCK)
        out = []
        for e in range(MLA_HEADS_PER_STEP):
            m_old, l_old, acc = carry[e]
            s = lax.dot_general(qcs[e], kc_ref[e, pl.ds(k0, MLA_BLOCK), :], _NT,
                                preferred_element_type=F32)
            if masked:
                tq = lax.broadcasted_iota(jnp.int32, s.shape, 0)
                sk = lax.broadcasted_iota(jnp.int32, s.shape, 1)
                s = jnp.where(sk <= tq, s, NEG_INF)
            m_new = jnp.maximum(m_old, jnp.max(s, axis=1, keepdims=True))
            alpha = jnp.exp2(m_old - m_new)
            p = jnp.exp2(s - jnp.concatenate([m_new] * lane_tiles, axis=1))
            pv = jnp.dot(p.astype(BF16), vaug_ref[e, pl.ds(k0, MLA_BLOCK), :],
                         preferred_element_type=F32)
            out.append((m_new, alpha * l_old + pv[:, V_HEAD_DIM:],
                        alpha * acc + pv[:, :V_HEAD_DIM]))
        return tuple(out)

    init = (jnp.full((MLA_BLOCK, LANES), NEG_INF, F32),
            jnp.zeros((MLA_BLOCK, LANES), F32),
            jnp.zeros((MLA_BLOCK, V_HEAD_DIM), F32))
    carry = lax.fori_loop(0, qi, functools.partial(step, masked=False),
                          (init,) * MLA_HEADS_PER_STEP)
    carry = step(qi, carry, masked=True)
    for e in range(MLA_HEADS_PER_STEP):
        _, l_fin, acc = carry[e]
        o_ref[:, e * V_HEAD_DIM:(e + 1) * V_HEAD_DIM] = (acc / l_fin).astype(BF16)


def _mla(q, kv, p, rope_b, batch, seq):
    t = q.shape[0]
    nq = seq // MLA_BLOCK
    head_w = QK_NOPE_DIM + LANES
    step_w = MLA_HEADS_PER_STEP * head_w
    qrow = lambda b, hp, qi: b * nq + qi
    return pl.pallas_call(
        _mla_body,
        grid=(batch, B_HEADS // MLA_HEADS_PER_STEP, nq),
        in_specs=[
            pl.BlockSpec((MLA_BLOCK, step_w), lambda b, hp, qi: (qrow(b, hp, qi), hp)),
            pl.BlockSpec((seq, step_w), lambda b, hp, qi: (b, hp)),
            pl.BlockSpec((seq, LANES), lambda b, hp, qi: (b, COL_MISC // LANES)),
            pl.BlockSpec((2, MLA_BLOCK, LANES), lambda b, hp, qi: (0, qrow(b, hp, qi), 0)),
            pl.BlockSpec((2, seq, LANES), lambda b, hp, qi: (0, b, 0)),
        ],
        out_specs=pl.BlockSpec((MLA_BLOCK, MLA_HEADS_PER_STEP * V_HEAD_DIM),
                               lambda b, hp, qi: (qrow(b, hp, qi), hp)),
        out_shape=jax.ShapeDtypeStruct((t, B_HEADS * V_HEAD_DIM), BF16),
        scratch_shapes=[
            pltpu.VMEM((seq, LANES), BF16),
            pltpu.VMEM((MLA_HEADS_PER_STEP, seq, head_w), BF16),
            pltpu.VMEM((MLA_HEADS_PER_STEP, seq, head_w), BF16)],
        compiler_params=_cparams(3),
        name="mla_attention",
    )(q, kv, p, rope_b, rope_b)


def _rope_tables(positions):
    freqs = ROPE_THETA ** (-jnp.arange(ROPE_HALF, dtype=F32) / ROPE_HALF)
    ang = positions.astype(F32).reshape(-1, 1) * freqs
    cos, sin = jnp.cos(ang), jnp.sin(ang)
    one, zero = jnp.ones_like(cos), jnp.zeros_like(cos)
    cat = lambda parts: jnp.concatenate(parts, axis=1)
    rope_a = jnp.stack([cat([one, one, cos, cos]), cat([zero, zero, -sin, zero]),
                        cat([zero, zero, zero, sin])])
    rope_b = jnp.stack([cat([cos, zero, cos, zero]), cat([-sin, zero, sin, zero])])
    return rope_a, rope_b


def _reorder_w_in(w):
    splits = (A_HEADS * A_HEAD_DIM, A_HEAD_DIM, A_HEAD_DIM, IDX_HEADS * IDX_HEAD_DIM,
              IDX_HEAD_DIM, IDX_HEADS, Q_LORA_RANK, KV_LORA_RANK, QK_ROPE_DIM)
    offs = np.concatenate([[0], np.cumsum(splits)])
    aq, ak, av, iq, ik, iw, ql, kvl, kr = [w[:, int(offs[n]):int(offs[n + 1])] for n in range(9)]
    g = QK_ROPE_DIM // 2
    pad = jnp.zeros((w.shape[0], LANES - QK_ROPE_DIM - IDX_HEADS), w.dtype)
    misc = [kr[:, :g], iw, kr[:, g:], pad]
    out = jnp.concatenate([iq, aq, ql, kvl, ak, av, ik] + misc, axis=1).astype(BF16)
    assert out.shape[1] == P_WIDTH and MISC_IW == g
    return out


def _reorder_w_uq(w):
    r = w.shape[0]
    g = QK_ROPE_DIM // 2
    w3 = w.reshape(r, B_HEADS, QK_NOPE_DIM + QK_ROPE_DIM)
    zero = jnp.zeros((r, B_HEADS, g), w.dtype)
    w3 = jnp.concatenate([w3[:, :, :QK_NOPE_DIM], w3[:, :, QK_NOPE_DIM:QK_NOPE_DIM + g], zero,
                          w3[:, :, QK_NOPE_DIM + g:], zero], axis=2)
    return w3.reshape(r, B_HEADS * 2 * LANES)


def kernel(x, c, positions, w_ada, b_ada, w_in, rel_bias, q_norm_g, w_uq, kv_norm_g, w_ukv,
           w_o, ln1_g, ln1_b, w_gate, w_up, conv_w, conv_b, w_down, ln2_g, ln2_b):
    batch, seq, d = x.shape
    depth = w_ada.shape[0]
    t = batch * seq
    assert d == D_MODEL and batch <= SUBLANES and seq % 1024 == 0
    alpha = (2 * depth) ** 0.25

    rope_a, rope_b = _rope_tables(positions)
    bias_tiles = _bias_tiles(rel_bias)
    c8 = jnp.zeros((SUBLANES, d), F32).at[:batch].set(c)
    x2 = x.reshape(t, d)

    for l in range(depth):
        mod = _ada(c8, w_ada[l], b_ada[l])
        mod3 = mod[:batch].reshape(batch * 6, 1, d)

        u = _modulate(x2, mod3, seq, shift_idx=0, scale_idx=1)
        p = _matmul(u, _reorder_w_in(w_in[l]), F32, 1024, 512, "in_proj")
        y_a = _dsa(p, rope_a, bias_tiles, batch, seq)
        q_lat = _rmsnorm_cols(p, COL_QL, Q_LORA_RANK, q_norm_g[l], "q_rmsnorm")
        kv_lat = _rmsnorm_cols(p, COL_KVL, KV_LORA_RANK, kv_norm_g[l], "kv_rmsnorm")
        q = _matmul(q_lat, _reorder_w_uq(w_uq[l]), F32, 1024, 512, "q_up_proj")
        kv = _matmul(kv_lat, w_ukv[l], BF16, 1024, 512, "kv_up_proj")
        y_b = _mla(q, kv, p, rope_b, batch, seq)
        mix = _matmul_concat(y_a, y_b, w_o[l], 1024, 512, "out_proj")
        x2, u = _resln(x2, mix, mod3, seq, 2, ln1_g[l], ln1_b[l], alpha, mod_idx=(3, 4))

        hidden = _ffn_gate_up(u, w_gate[l], w_up[l], conv_w[l], conv_b[l], seq)
        y = _ffn_down(hidden, w_down[l])
        x2 = _resln(x2, y, mod3, seq, 5, ln2_g[l], ln2_b[l], alpha)

    return x2.reshape(batch, seq, d)
```

```python
import functools
import math

import jax
import jax.numpy as jnp
import numpy as np
from jax import lax
from jax.experimental import pallas as pl
from jax.experimental.pallas import tpu as pltpu

F32 = jnp.float32
BF16 = jnp.bfloat16

D_MODEL = 4096
A_HEAD_DIM = 128
A_HEADS = 16
IDX_HEADS = 32
IDX_HEAD_DIM = 128
IDX_ROPE_DIM = 64
TOPK_MAX = 256
V_HEAD_DIM = 128
B_HEADS = 16
Q_LORA_RANK = 1024
KV_LORA_RANK = 512
QK_NOPE_DIM = 128
QK_ROPE_DIM = 64
D_FF = 11008
CONV_WIDTH = 3
REL_BUCKETS = 32
REL_MAX_DIST = 128
ROPE_THETA = 10000.0
Q_BLOCK = 128
LN_EPS = 1e-5
RMS_EPS = 1e-6
NEG_INF = -1e30

LANES = 128
SUBLANES = 8
VMEM_LIMIT_BYTES = 56 * 1024 * 1024

IN_TILE = 256
COL_IQ = 0
COL_AQ = 4096
COL_QL = 6144
COL_KVL = 7168
COL_AK = 7680
COL_AV = 7808
COL_IK = 7936
COL_MISC = 8192
MISC_IW = 32
P_WIDTH = 8448
TAIL_WIDTH = Q_LORA_RANK + KV_LORA_RANK + IN_TILE

_NT = (((1,), (1,)), ((), ()))


def _cparams(n_axes):
    return pltpu.CompilerParams(
        dimension_semantics=("arbitrary",) * n_axes,
        vmem_limit_bytes=VMEM_LIMIT_BYTES)


def _ada_body(c_ref, w_ref, b_ref, o_ref):
    c = c_ref[...]
    act = (c / (1.0 + jnp.exp(-c))).astype(BF16)
    o_ref[...] = jnp.dot(act, w_ref[...].astype(BF16),
                         preferred_element_type=F32) + b_ref[...]


def _ada(c8, w, b):
    d, n = w.shape
    tn = 512
    return pl.pallas_call(
        _ada_body,
        grid=(n // tn,),
        in_specs=[pl.BlockSpec((SUBLANES, d), lambda j: (0, 0)),
                  pl.BlockSpec((d, tn), lambda j: (0, j)),
                  pl.BlockSpec((1, tn), lambda j: (0, j))],
        out_specs=pl.BlockSpec((SUBLANES, tn), lambda j: (0, j)),
        out_shape=jax.ShapeDtypeStruct((SUBLANES, n), F32),
        compiler_params=_cparams(1),
        name="ada_proj",
    )(c8, w, b.reshape(1, n))


def _modulate_body(x_ref, sh_ref, sc_ref, o_ref):
    o_ref[...] = (x_ref[...] * (1.0 + sc_ref[0]) + sh_ref[0]).astype(BF16)


def _modulate(x2, mod3, seq, shift_idx, scale_idx):
    t, d = x2.shape
    tm = 256
    per_seq = seq // tm
    return pl.pallas_call(
        _modulate_body,
        grid=(t // tm,),
        in_specs=[pl.BlockSpec((tm, d), lambda i: (i, 0)),
                  pl.BlockSpec((1, 1, d), lambda i: (6 * (i // per_seq) + shift_idx, 0, 0)),
                  pl.BlockSpec((1, 1, d), lambda i: (6 * (i // per_seq) + scale_idx, 0, 0))],
        out_specs=pl.BlockSpec((tm, d), lambda i: (i, 0)),
        out_shape=jax.ShapeDtypeStruct((t, d), BF16),
        compiler_params=_cparams(1),
        name="modulate",
    )(x2, mod3, mod3)


def _mm_body(a_ref, w_ref, o_ref):
    o_ref[...] = jnp.dot(a_ref[...], w_ref[...].astype(BF16),
                         preferred_element_type=F32).astype(o_ref.dtype)


def _matmul(a, w, out_dtype, tm, tn, name):
    m, k = a.shape
    n = w.shape[1]
    return pl.pallas_call(
        _mm_body,
        grid=(m // tm, n // tn),
        in_specs=[pl.BlockSpec((tm, k), lambda i, j: (i, 0)),
                  pl.BlockSpec((k, tn), lambda i, j: (0, j))],
        out_specs=pl.BlockSpec((tm, tn), lambda i, j: (i, j)),
        out_shape=jax.ShapeDtypeStruct((m, n), out_dtype),
        compiler_params=_cparams(2),
        name=name,
    )(a, w)


def _mm2_body(a0_ref, a1_ref, w0_ref, w1_ref, o_ref):
    acc = jnp.dot(a0_ref[...], w0_ref[...].astype(BF16), preferred_element_type=F32)
    acc = acc + jnp.dot(a1_ref[...], w1_ref[...].astype(BF16), preferred_element_type=F32)
    o_ref[...] = acc


def _matmul_concat(a0, a1, w, tm, tn, name):
    m, k0 = a0.shape
    k1 = a1.shape[1]
    assert k0 == k1
    n = w.shape[1]
    return pl.pallas_call(
        _mm2_body,
        grid=(m // tm, n // tn),
        in_specs=[pl.BlockSpec((tm, k0), lambda i, j: (i, 0)),
                  pl.BlockSpec((tm, k1), lambda i, j: (i, 0)),
                  pl.BlockSpec((k0, tn), lambda i, j: (0, j)),
                  pl.BlockSpec((k1, tn), lambda i, j: (1, j))],
        out_specs=pl.BlockSpec((tm, tn), lambda i, j: (i, j)),
        out_shape=jax.ShapeDtypeStruct((m, n), F32),
        compiler_params=_cparams(2),
        name=name,
    )(a0, a1, w, w)


def _rms_body(x_ref, g_ref, o_ref):
    x = x_ref[...]
    ms = jnp.mean(x * x, axis=-1, keepdims=True)
    o_ref[...] = (x * lax.rsqrt(ms + RMS_EPS) * g_ref[...]).astype(BF16)


def _rmsnorm_cols(p, col, width, gain, name):
    t = p.shape[0]
    tm = 512
    cb = col // width
    return pl.pallas_call(
        _rms_body,
        grid=(t // tm,),
        in_specs=[pl.BlockSpec((tm, width), lambda i: (i, cb)),
                  pl.BlockSpec((1, width), lambda i: (0, 0))],
        out_specs=pl.BlockSpec((tm, width), lambda i: (i, 0)),
        out_shape=jax.ShapeDtypeStruct((t, width), BF16),
        compiler_params=_cparams(1),
        name=name,
    )(p, gain.reshape(1, width))


def _resln_body(x_ref, y_ref, gate_ref, lg_ref, lb_ref, *rest, alpha, with_mod):
    z = alpha * x_ref[...] + (1.0 + gate_ref[0]) * y_ref[...]
    mu = jnp.mean(z, axis=-1, keepdims=True)
    zc = z - mu
    var = jnp.mean(zc * zc, axis=-1, keepdims=True)
    out = zc * lax.rsqrt(var + LN_EPS) * lg_ref[...] + lb_ref[...]
    if with_mod:
        sh_ref, sc_ref, o_ref, u_ref = rest
        u_ref[...] = (out * (1.0 + sc_ref[0]) + sh_ref[0]).astype(BF16)
    else:
        (o_ref,) = rest
    o_ref[...] = out


def _resln(x2, y, mod3, seq, gate_idx, ln_g, ln_b, alpha, mod_idx=None):
    t, d = x2.shape
    tm = 256
    per_seq = seq // tm
    with_mod = mod_idx is not None
    row = pl.BlockSpec((tm, d), lambda i: (i, 0))
    vec = pl.BlockSpec((1, d), lambda i: (0, 0))

    def mod_spec(idx):
        return pl.BlockSpec((1, 1, d), lambda i: (6 * (i // per_seq) + idx, 0, 0))

    in_specs = [row, row, mod_spec(gate_idx), vec, vec]
    args = [x2, y, mod3, ln_g.reshape(1, d), ln_b.reshape(1, d)]
    out_specs = row
    out_shape = jax.ShapeDtypeStruct((t, d), F32)
    if with_mod:
        in_specs += [mod_spec(mod_idx[0]), mod_spec(mod_idx[1])]
        args += [mod3, mod3]
        out_specs = [row, row]
        out_shape = [out_shape, jax.ShapeDtypeStruct((t, d), BF16)]
    return pl.pallas_call(
        functools.partial(_resln_body, alpha=alpha, with_mod=with_mod),
        grid=(t // tm,),
        in_specs=in_specs,
        out_specs=out_specs,
        out_shape=out_shape,
        compiler_params=_cparams(1),
        name="residual_ln",
    )(*args)


def _ffn_gu_body(a_ref, halo_ref, wg_ref, wu_ref, cw_ref, cb_ref, o_ref, *, tiles_per_seq):
    i = pl.program_id(0)
    a = a_ref[...]
    wg = wg_ref[...].astype(BF16)
    wu = wu_ref[...].astype(BF16)
    g = jnp.dot(a, wg, preferred_element_type=F32)
    up = jnp.dot(a, wu, preferred_element_type=F32)
    gh = jnp.dot(halo_ref[...], wg, preferred_element_type=F32)
    gh = jnp.where(i % tiles_per_seq == 0, 0.0, gh)
    prev1 = gh[SUBLANES - 1:SUBLANES, :]
    prev2 = gh[SUBLANES - 2:SUBLANES - 1, :]
    row = lax.broadcasted_iota(jnp.int32, g.shape, 0)
    g1 = jnp.where(row == 0, prev1, pltpu.roll(g, 1, 0))
    g2 = jnp.where(row == 0, prev2, jnp.where(row == 1, prev1, pltpu.roll(g, 2, 0)))
    cw = cw_ref[...]
    conv = cb_ref[...] + cw[0:1, :] * g2 + cw[1:2, :] * g1 + cw[2:3, :] * g
    o_ref[...] = (conv / (1.0 + jnp.exp(-conv)) * up).astype(BF16)


def _ffn_gate_up(u, w_gate, w_up, conv_w, conv_b, seq):
    t, d = u.shape
    f = w_gate.shape[1]
    tm, tn = 1024, 256
    halo_blocks = tm // SUBLANES
    return pl.pallas_call(
        functools.partial(_ffn_gu_body, tiles_per_seq=seq // tm),
        grid=(t // tm, f // tn),
        in_specs=[pl.BlockSpec((tm, d), lambda i, j: (i, 0)),
                  pl.BlockSpec((SUBLANES, d), lambda i, j: (jnp.maximum(i * halo_blocks - 1, 0), 0)),
                  pl.BlockSpec((d, tn), lambda i, j: (0, j)),
                  pl.BlockSpec((d, tn), lambda i, j: (0, j)),
                  pl.BlockSpec((CONV_WIDTH, tn), lambda i, j: (0, j)),
                  pl.BlockSpec((1, tn), lambda i, j: (0, j))],
        out_specs=pl.BlockSpec((tm, tn), lambda i, j: (i, j)),
        out_shape=jax.ShapeDtypeStruct((t, f), BF16),
        compiler_params=_cparams(2),
        name="ffn_gate_up",
    )(u, u, w_gate, w_up, conv_w, conv_b.reshape(1, f))


def _ffn_down_body(am_ref, at_ref, wm_ref, wt_ref, o_ref, acc_ref, *, n_main, n_tail):
    k = pl.program_id(2)

    @pl.when(k == 0)
    def _():
        acc_ref[...] = jnp.zeros_like(acc_ref)

    @pl.when(k < n_main)
    def _():
        acc_ref[...] += jnp.dot(am_ref[...], wm_ref[...].astype(BF16),
                                preferred_element_type=F32)

    @pl.when(k >= n_main)
    def _():
        acc_ref[...] += jnp.dot(at_ref[...], wt_ref[...].astype(BF16),
                                preferred_element_type=F32)

    @pl.when(k == n_main + n_tail - 1)
    def _():
        o_ref[...] = acc_ref[...]


def _ffn_down(h, w_down):
    t, f = h.shape
    d = w_down.shape[1]
    tm, tn, tk, tk_tail = 2048, 1024, 1024, 256
    n_main = f // tk
    n_tail = (f - n_main * tk) // tk_tail
    assert n_main * tk + n_tail * tk_tail == f and n_tail > 0
    tail0 = n_main * tk // tk_tail

    def main_k(k):
        return jnp.minimum(k, n_main - 1)

    def tail_k(k):
        return tail0 + jnp.maximum(k - n_main, 0)

    return pl.pallas_call(
        functools.partial(_ffn_down_body, n_main=n_main, n_tail=n_tail),
        grid=(t // tm, d // tn, n_main + n_tail),
        in_specs=[pl.BlockSpec((tm, tk), lambda i, j, k: (i, main_k(k))),
                  pl.BlockSpec((tm, tk_tail), lambda i, j, k: (i, tail_k(k))),
                  pl.BlockSpec((tk, tn), lambda i, j, k: (main_k(k), j)),
                  pl.BlockSpec((tk_tail, tn), lambda i, j, k: (tail_k(k), j))],
        out_specs=pl.BlockSpec((tm, tn), lambda i, j, k: (i, j)),
        out_shape=jax.ShapeDtypeStruct((t, d), F32),
        scratch_shapes=[pltpu.VMEM((tm, tn), F32)],
        compiler_params=_cparams(3),
        name="ffn_down",
    )(h, h, w_down, w_down)


ROPE_HALF = IDX_ROPE_DIM // 2
ROLLS_SPLIT = (LANES - ROPE_HALF, ROPE_HALF)
ROLLS_SPREAD = (LANES // 2,)


def _rope_tile(x, tabs, shifts):
    out = x * tabs[0]
    for n, shift in enumerate(shifts):
        out = out + pltpu.roll(x, shift, 1) * tabs[1 + n]
    return out


KCHUNK = 2 * Q_BLOCK
N_BIAS_TILES = 3
LOG2E = math.log2(math.e)


def _bias_tiles_body(rb_ref, o_ref):
    o = pl.program_id(0)
    tq = lax.broadcasted_iota(jnp.int32, (Q_BLOCK, KCHUNK), 0)
    sk = lax.broadcasted_iota(jnp.int32, (Q_BLOCK, KCHUNK), 1)
    n = jnp.maximum(o * Q_BLOCK + tq - sk, 0)
    max_exact = REL_BUCKETS // 2
    nf = jnp.maximum(n, 1).astype(F32)
    large = max_exact + (jnp.log(nf / max_exact) / math.log(REL_MAX_DIST / max_exact)
                         * (REL_BUCKETS - max_exact)).astype(jnp.int32)
    large = jnp.minimum(large, REL_BUCKETS - 1)
    bucket = jnp.where(n < max_exact, n, large)
    hits = [bucket == b for b in range(REL_BUCKETS - 1)]
    for h in range(A_HEADS):
        far = rb_ref[REL_BUCKETS - 1, h]
        tile = jnp.zeros((Q_BLOCK, KCHUNK), F32)
        for b in range(REL_BUCKETS - 1):
            tile = jnp.where(hits[b], (rb_ref[b, h] - far) * LOG2E, tile)
        o_ref[0, h] = tile


def _bias_tiles(rel_bias):
    return pl.pallas_call(
        _bias_tiles_body,
        grid=(N_BIAS_TILES,),
        in_specs=[pl.BlockSpec(memory_space=pltpu.SMEM)],
        out_specs=pl.BlockSpec((1, A_HEADS, Q_BLOCK, KCHUNK), lambda o: (o, 0, 0, 0)),
        out_shape=jax.ShapeDtypeStruct((N_BIAS_TILES, A_HEADS, Q_BLOCK, KCHUNK), F32),
        compiler_params=_cparams(1),
        name="rel_bias_tiles",
    )(rel_bias)


SEL_ROWS = 512
SEL_SLAB = 64


def _dsa_body(iq_ref, aq_ref, ak_ref, av_ref, ik_ref, misc_ref, rope_ref, bias_ref,
              o_ref,
              ikr_ref, akb_ref, vaug_ref, iq2_ref, wb_ref, st_ref, madd_ref, q2_ref,
              m_ref, l_ref, acc_ref, thr_ref,
              *, topk, seq):
    i = pl.program_id(1)
    n_chunks = i // 2 + 1
    t0 = i * Q_BLOCK
    rope = (rope_ref[0], rope_ref[1], rope_ref[2])

    @pl.when(i == 0)
    def _():
        ikr_ref[...] = jnp.zeros_like(ikr_ref)
        akb_ref[...] = ak_ref[...].astype(BF16)
        vaug_ref[:, :A_HEAD_DIM] = av_ref[...].astype(BF16)
        vaug_ref[:, A_HEAD_DIM:] = jnp.ones((seq, LANES), BF16)

    ikr_ref[pl.ds(pl.multiple_of(t0, Q_BLOCK), Q_BLOCK), :] = _rope_tile(
        ik_ref[...], rope, ROLLS_SPLIT).astype(BF16)

    for h in range(IDX_HEADS):
        sl = slice(h * Q_BLOCK, (h + 1) * Q_BLOCK)
        iq2_ref[sl, :] = _rope_tile(iq_ref[:, sl], rope, ROLLS_SPLIT).astype(BF16)

    w_scale = (IDX_HEADS ** -0.5) * (IDX_HEAD_DIM ** -0.5)
    for h in range(IDX_HEADS):
        col = misc_ref[:, MISC_IW + h:MISC_IW + h + 1] * w_scale
        wb_ref[h] = jnp.broadcast_to(col, (Q_BLOCK, LANES))

    heads_per_dot = 4

    def score_chunk(jj, carry):
        k0 = pl.multiple_of(jj * KCHUNK, KCHUNK)
        kc = ikr_ref[pl.ds(k0, KCHUNK), :]
        sc = jnp.zeros((Q_BLOCK, KCHUNK), F32)
        for g in range(IDX_HEADS // heads_per_dot):
            rows = slice(g * heads_per_dot * Q_BLOCK, (g + 1) * heads_per_dot * Q_BLOCK)
            d = lax.dot_general(iq2_ref[rows, :], kc, _NT, preferred_element_type=F32)
            for hh in range(heads_per_dot):
                w = wb_ref[g * heads_per_dot + hh]
                dh = jnp.maximum(d[hh * Q_BLOCK:(hh + 1) * Q_BLOCK, :], 0.0)
                sc = sc + dh * jnp.concatenate([w, w], axis=1)
        tq = t0 + lax.broadcasted_iota(jnp.int32, sc.shape, 0)
        sk = k0 + lax.broadcasted_iota(jnp.int32, sc.shape, 1)
        sc = jnp.where(sk <= tq, sc, NEG_INF)
        st_ref[pl.ds(k0, Q_BLOCK), :] = sc[:, :Q_BLOCK].T
        st_ref[pl.ds(k0 + Q_BLOCK, Q_BLOCK), :] = sc[:, Q_BLOCK:].T
        return carry

    lax.fori_loop(0, n_chunks, score_chunk, 0)

    @pl.when(n_chunks % (SEL_ROWS // KCHUNK) == 1)
    def _():
        pad0 = pl.multiple_of(n_chunks * KCHUNK, KCHUNK)
        st_ref[pl.ds(pad0, KCHUNK), :] = jnp.full((KCHUNK, Q_BLOCK), NEG_INF, F32)

    def key_to_f32(key):
        bits = key ^ ((key >> 31) & jnp.int32(0x7FFFFFFF))
        return lax.bitcast_convert_type(bits, F32)

    kf = float(topk)

    def write_mask(k0, rows, keep_fn):
        blk = st_ref[pl.ds(k0, rows), :]
        sk = k0 + lax.broadcasted_iota(jnp.int32, blk.shape, 0)
        tq = t0 + lax.broadcasted_iota(jnp.int32, blk.shape, 1)
        keep = keep_fn(blk, sk) & (sk <= tq)
        madd_ref[pl.ds(k0, rows), :] = jnp.where(keep, 0.0, NEG_INF)

    def select_static(rows):
        def count_ge(thr_row):
            thr_b = jnp.broadcast_to(thr_row, (SEL_SLAB, LANES))
            acc = jnp.zeros((SEL_SLAB, LANES), F32)
            for r in range(rows // SEL_SLAB):
                blk = st_ref[r * SEL_SLAB:(r + 1) * SEL_SLAB, :]
                acc = jnp.where(blk >= thr_b, acc + 1.0, acc)
            acc = jnp.sum(acc.reshape(SEL_SLAB // SUBLANES, SUBLANES, LANES), axis=0)
            return jnp.sum(acc, axis=0, keepdims=True)

        def select_pass(p, key):
            cand = key + jnp.left_shift(jnp.int32(1), 31 - p)
            return jnp.where(count_ge(key_to_f32(cand)) >= kf, cand, key)

        key0 = jnp.full((1, LANES), jnp.iinfo(jnp.int32).min, jnp.int32)
        thr = key_to_f32(lax.fori_loop(0, 32, select_pass, key0))
        thr_ref[0:1, :] = thr
        thr_ref[1:2, :] = count_ge(thr)
        write_mask(0, rows, lambda blk, sk: blk >= thr)

    for n in range(1, seq // SEL_ROWS + 1):
        pl.when(i // (SEL_ROWS // Q_BLOCK) + 1 == n)(
            functools.partial(select_static, n * SEL_ROWS))

    thr = thr_ref[0:1, :]
    tie = jnp.where((thr_ref[1:2, :] > kf) & (thr > NEG_INF), 1.0, 0.0)

    @pl.when(jnp.max(tie) > 0.0)
    def _():
        def count(pred):
            def body(jj, acc):
                k0 = pl.multiple_of(jj * KCHUNK, KCHUNK)
                hit = jnp.where(pred(st_ref[pl.ds(k0, KCHUNK), :], k0), 1.0, 0.0)
                return acc + jnp.sum(hit.reshape(KCHUNK // SUBLANES, SUBLANES, LANES), axis=0)
            acc = lax.fori_loop(0, n_chunks, body, jnp.zeros((SUBLANES, LANES), F32))
            return jnp.sum(acc, axis=0, keepdims=True)

        def key_index(shape, k0):
            return k0 + lax.broadcasted_iota(jnp.int32, shape, 0)

        need = kf - count(lambda blk, k0: blk > thr)

        def index_pass(p, lo):
            cand = lo + jnp.left_shift(jnp.int32(1), 10 - p)
            n_eq = count(lambda blk, k0: (blk == thr) & (key_index(blk.shape, k0) < cand))
            return jnp.where(n_eq < need, cand, lo)

        lo = lax.fori_loop(0, 11, index_pass, jnp.zeros((1, LANES), jnp.int32))

        def rewrite(jj, carry):
            write_mask(pl.multiple_of(jj * KCHUNK, KCHUNK), KCHUNK,
                       lambda blk, sk: (blk > thr) | ((blk == thr) & (sk <= lo)))
            return carry

        lax.fori_loop(0, n_chunks, rewrite, 0)

    q_scale = (A_HEAD_DIM ** -0.5) * LOG2E
    for h in range(A_HEADS):
        sl = slice(h * Q_BLOCK, (h + 1) * Q_BLOCK)
        q2_ref[sl, :] = (aq_ref[:, sl] * q_scale).astype(BF16)
    m_ref[...] = jnp.full_like(m_ref, NEG_INF)
    l_ref[...] = jnp.zeros_like(l_ref)
    acc_ref[...] = jnp.zeros_like(acc_ref)

    heads_per_att = 4

    def att_chunk(jj, carry, near):
        k0 = pl.multiple_of(jj * KCHUNK, KCHUNK)
        kc = akb_ref[pl.ds(k0, KCHUNK), :]
        vc = vaug_ref[pl.ds(k0, KCHUNK), :]
        sel = jnp.concatenate([madd_ref[pl.ds(k0, Q_BLOCK), :].T,
                               madd_ref[pl.ds(k0 + Q_BLOCK, Q_BLOCK), :].T], axis=1)
        for g in range(A_HEADS // heads_per_att):
            rows = slice(g * heads_per_att * Q_BLOCK, (g + 1) * heads_per_att * Q_BLOCK)
            lg = lax.dot_general(q2_ref[rows, :], kc, _NT, preferred_element_type=F32)
            ps = []
            alphas = []
            for hh in range(heads_per_att):
                h = g * heads_per_att + hh
                hs = slice(h * Q_BLOCK, (h + 1) * Q_BLOCK)
                s = lg[hh * Q_BLOCK:(hh + 1) * Q_BLOCK, :] + sel
                if near:
                    s = s + bias_ref[i - 2 * jj, h]
                m_old = m_ref[hs, :]
                m_new = jnp.maximum(m_old, jnp.max(s, axis=1, keepdims=True))
                alphas.append(jnp.exp2(m_old - m_new))
                m_ref[hs, :] = m_new
                p = jnp.exp2(s - jnp.concatenate([m_new, m_new], axis=1))
                ps.append(p.astype(BF16))
            pv = jnp.dot(jnp.concatenate(ps, axis=0), vc, preferred_element_type=F32)
            for hh in range(heads_per_att):
                h = g * heads_per_att + hh
                hs = slice(h * Q_BLOCK, (h + 1) * Q_BLOCK)
                pv_h = pv[hh * Q_BLOCK:(hh + 1) * Q_BLOCK, :]
                acc_ref[hs, :] = alphas[hh] * acc_ref[hs, :] + pv_h[:, :A_HEAD_DIM]
                l_ref[hs, :] = alphas[hh] * l_ref[hs, :] + pv_h[:, A_HEAD_DIM:]
        return carry

    first_near = jnp.maximum(i - N_BIAS_TILES + 2, 0) // 2
    lax.fori_loop(0, first_near, functools.partial(att_chunk, near=False), 0)
    lax.fori_loop(first_near, n_chunks, functools.partial(att_chunk, near=True), 0)

    for h in range(A_HEADS):
        hs = slice(h * Q_BLOCK, (h + 1) * Q_BLOCK)
        o_ref[:, hs] = (acc_ref[hs, :] / l_ref[hs, :]).astype(BF16)


def _dsa(p, rope_a, bias_tiles, batch, seq):
    t = p.shape[0]
    nb = seq // Q_BLOCK
    topk = min(TOPK_MAX, seq // 4)
    assert topk <= KCHUNK and seq % SEL_ROWS == 0 and seq <= 2048
    qrow = lambda b, i: b * nb + i
    stat = pltpu.VMEM((A_HEADS * Q_BLOCK, LANES), F32)
    return pl.pallas_call(
        functools.partial(_dsa_body, topk=topk, seq=seq),
        grid=(batch, nb),
        in_specs=[
            pl.BlockSpec((Q_BLOCK, IDX_HEADS * IDX_HEAD_DIM), lambda b, i: (qrow(b, i), COL_IQ // 4096)),
            pl.BlockSpec((Q_BLOCK, A_HEADS * A_HEAD_DIM), lambda b, i: (qrow(b, i), COL_AQ // 2048)),
            pl.BlockSpec((seq, LANES), lambda b, i: (b, COL_AK // LANES)),
            pl.BlockSpec((seq, LANES), lambda b, i: (b, COL_AV // LANES)),
            pl.BlockSpec((Q_BLOCK, LANES), lambda b, i: (qrow(b, i), COL_IK // LANES)),
            pl.BlockSpec((Q_BLOCK, LANES), lambda b, i: (qrow(b, i), COL_MISC // LANES)),
            pl.BlockSpec((3, Q_BLOCK, LANES), lambda b, i: (0, qrow(b, i), 0)),
            pl.BlockSpec(bias_tiles.shape, lambda b, i: (0, 0, 0, 0)),
        ],
        out_specs=pl.BlockSpec((Q_BLOCK, A_HEADS * A_HEAD_DIM), lambda b, i: (qrow(b, i), 0)),
        out_shape=jax.ShapeDtypeStruct((t, A_HEADS * A_HEAD_DIM), BF16),
        scratch_shapes=[
            pltpu.VMEM((seq, IDX_HEAD_DIM), BF16),
            pltpu.VMEM((seq, A_HEAD_DIM), BF16),
            pltpu.VMEM((seq, 2 * LANES), BF16),
            pltpu.VMEM((IDX_HEADS * Q_BLOCK, IDX_HEAD_DIM), BF16),
            pltpu.VMEM((IDX_HEADS, Q_BLOCK, LANES), F32),
            pltpu.VMEM((seq, Q_BLOCK), F32),
            pltpu.VMEM((seq, Q_BLOCK), F32),
            pltpu.VMEM((A_HEADS * Q_BLOCK, A_HEAD_DIM), BF16),
            stat, stat, stat,
            pltpu.VMEM((SUBLANES, LANES), F32),
        ],
        compiler_params=_cparams(2),
        name="dsa_attention",
    )(p, p, p, p, p, p, rope_a, bias_tiles)


MLA_BLOCK = 512
MLA_HEADS_PER_STEP = 4


def _mla_body(q_ref, kv_ref, misc_ref, ropeq_ref, ropek_ref, o_ref, krr_ref, kc_ref, vaug_ref):
    hp = pl.program_id(1)
    qi = pl.program_id(2)
    head_w = QK_NOPE_DIM + LANES
    seq = kv_ref.shape[0]

    @pl.when((hp == 0) & (qi == 0))
    def _():
        krr_ref[...] = _rope_tile(misc_ref[...], (ropek_ref[0], ropek_ref[1]),
                                  ROLLS_SPREAD).astype(BF16)

    @pl.when(qi == 0)
    def _():
        for e in range(MLA_HEADS_PER_STEP):
            c0 = e * head_w
            kc_ref[e, :, :QK_NOPE_DIM] = kv_ref[:, c0:c0 + QK_NOPE_DIM]
            kc_ref[e, :, QK_NOPE_DIM:] = krr_ref[...]
            vaug_ref[e, :, :V_HEAD_DIM] = kv_ref[:, c0 + QK_NOPE_DIM:c0 + head_w]
            vaug_ref[e, :, V_HEAD_DIM:] = jnp.ones((seq, LANES), BF16)

    scale = (QK_NOPE_DIM + QK_ROPE_DIM) ** -0.5 * LOG2E
    qcs = []
    for e in range(MLA_HEADS_PER_STEP):
        c0 = e * head_w
        q_rope = _rope_tile(q_ref[:, c0 + QK_NOPE_DIM:c0 + head_w].astype(F32),
                            (ropeq_ref[0], ropeq_ref[1]), ROLLS_SPREAD)
        qc = jnp.concatenate([q_ref[:, c0:c0 + QK_NOPE_DIM].astype(F32), q_rope], axis=1) * scale
        qcs.append(qc.astype(BF16))
    lane_tiles = MLA_BLOCK // LANES

    def step(j, carry, masked):
        k0 = pl.multiple_of(j * MLA_BLOCK, MLA_BLOCK)
        out = []
        for e in range(MLA_HEADS_PER_STEP):
            m_old, l_old, acc = carry[e]
            s = lax.dot_general(qcs[e], kc_ref[e, pl.ds(k0, MLA_BLOCK), :], _NT,
                                preferred_element_type=F32)
            if masked:
                tq = lax.broadcasted_iota(jnp.int32, s.shape, 0)
                sk = lax.broadcasted_iota(jnp.int32, s.shape, 1)
                s = jnp.where(sk <= tq, s, NEG_INF)
            m_new = jnp.maximum(m_old, jnp.max(s, axis=1, keepdims=True))
            alpha = jnp.exp2(m_old - m_new)
            p = jnp.exp2(s - jnp.concatenate([m_new] * lane_tiles, axis=1))
            pv = jnp.dot(p.astype(BF16), vaug_ref[e, pl.ds(k0, MLA_BLOCK), :],
                         preferred_element_type=F32)
            out.append((m_new, alpha * l_old + pv[:, V_HEAD_DIM:],
                        alpha * acc + pv[:, :V_HEAD_DIM]))
        return tuple(out)

    init = (jnp.full((MLA_BLOCK, LANES), NEG_INF, F32),
            jnp.zeros((MLA_BLOCK, LANES), F32),
            jnp.zeros((MLA_BLOCK, V_HEAD_DIM), F32))
    carry = lax.fori_loop(0, qi, functools.partial(step, masked=False),
                          (init,) * MLA_HEADS_PER_STEP)
    carry = step(qi, carry, masked=True)
    for e in range(MLA_HEADS_PER_STEP):
        _, l_fin, acc = carry[e]
        o_ref[:, e * V_HEAD_DIM:(e + 1) * V_HEAD_DIM] = (acc / l_fin).astype(BF16)


def _mla(q, kv, p, rope_b, batch, seq):
    t = q.shape[0]
    nq = seq // MLA_BLOCK
    head_w = QK_NOPE_DIM + LANES
    step_w = MLA_HEADS_PER_STEP * head_w
    qrow = lambda b, hp, qi: b * nq + qi
    return pl.pallas_call(
        _mla_body,
        grid=(batch, B_HEADS // MLA_HEADS_PER_STEP, nq),
        in_specs=[
            pl.BlockSpec((MLA_BLOCK, step_w), lambda b, hp, qi: (qrow(b, hp, qi), hp)),
            pl.BlockSpec((seq, step_w), lambda b, hp, qi: (b, hp)),
            pl.BlockSpec((seq, LANES), lambda b, hp, qi: (b, COL_MISC // LANES)),
            pl.BlockSpec((2, MLA_BLOCK, LANES), lambda b, hp, qi: (0, qrow(b, hp, qi), 0)),
            pl.BlockSpec((2, seq, LANES), lambda b, hp, qi: (0, b, 0)),
        ],
        out_specs=pl.BlockSpec((MLA_BLOCK, MLA_HEADS_PER_STEP * V_HEAD_DIM),
                               lambda b, hp, qi: (qrow(b, hp, qi), hp)),
        out_shape=jax.ShapeDtypeStruct((t, B_HEADS * V_HEAD_DIM), BF16),
        scratch_shapes=[
            pltpu.VMEM((seq, LANES), BF16),
            pltpu.VMEM((MLA_HEADS_PER_STEP, seq, head_w), BF16),
            pltpu.VMEM((MLA_HEADS_PER_STEP, seq, head_w), BF16)],
        compiler_params=_cparams(3),
        name="mla_attention",
    )(q, kv, p, rope_b, rope_b)


def _rope_tables(positions):
    freqs = ROPE_THETA ** (-jnp.arange(ROPE_HALF, dtype=F32) / ROPE_HALF)
    ang = positions.astype(F32).reshape(-1, 1) * freqs
    cos, sin = jnp.cos(ang), jnp.sin(ang)
    one, zero = jnp.ones_like(cos), jnp.zeros_like(cos)
    cat = lambda parts: jnp.concatenate(parts, axis=1)
    rope_a = jnp.stack([cat([one, one, cos, cos]), cat([zero, zero, -sin, zero]),
                        cat([zero, zero, zero, sin])])
    rope_b = jnp.stack([cat([cos, zero, cos, zero]), cat([-sin, zero, sin, zero])])
    return rope_a, rope_b


IN_SPLITS = (A_HEADS * A_HEAD_DIM, A_HEAD_DIM, A_HEAD_DIM, IDX_HEADS * IDX_HEAD_DIM,
             IDX_HEAD_DIM, IDX_HEADS, Q_LORA_RANK, KV_LORA_RANK, QK_ROPE_DIM)
IN_OFFS = [int(v) for v in np.concatenate([[0], np.cumsum(IN_SPLITS)])]


def _in_proj_tail_weight(w):
    iw, ql, kvl, kr = [w[:, IN_OFFS[n]:IN_OFFS[n + 1]] for n in range(5, 9)]
    g = QK_ROPE_DIM // 2
    pad = jnp.zeros((w.shape[0], TAIL_WIDTH - Q_LORA_RANK - KV_LORA_RANK - QK_ROPE_DIM - IDX_HEADS),
                    w.dtype)
    out = jnp.concatenate([ql, kvl, kr[:, :g], iw, kr[:, g:], pad], axis=1).astype(BF16)
    assert out.shape[1] == TAIL_WIDTH and MISC_IW == g
    return out


def _in_proj_tile_plan():
    aq0, ak0, iq0, ik0 = (IN_OFFS[n] // IN_TILE for n in (0, 1, 3, 4))
    assert all(IN_OFFS[n] % IN_TILE == 0 for n in (0, 1, 3, 4))
    plan = ([(0, iq0 + n) for n in range(IN_SPLITS[3] // IN_TILE)]
            + [(0, aq0 + n) for n in range(IN_SPLITS[0] // IN_TILE)]
            + [(1, n) for n in range((Q_LORA_RANK + KV_LORA_RANK) // IN_TILE)]
            + [(0, ak0), (0, ik0), (1, (Q_LORA_RANK + KV_LORA_RANK) // IN_TILE)])
    assert len(plan) * IN_TILE == P_WIDTH
    held = [0, 0]
    main_idx, tail_idx = [], []
    for src, tile in plan:
        held[src] = tile
        main_idx.append(held[0])
        tail_idx.append(held[1])
    as_i32 = lambda v: jnp.asarray(np.asarray(v, np.int32))
    return as_i32(main_idx), as_i32(tail_idx), as_i32([src for src, _ in plan])


def _in_proj_body(main_ref, tail_ref, src_ref, a_ref, wm_ref, wt_ref, o_ref):
    j = pl.program_id(1)

    @pl.when(src_ref[j] == 0)
    def _():
        o_ref[...] = jnp.dot(a_ref[...], wm_ref[...].astype(BF16), preferred_element_type=F32)

    @pl.when(src_ref[j] != 0)
    def _():
        o_ref[...] = jnp.dot(a_ref[...], wt_ref[...], preferred_element_type=F32)


def _in_proj(u, w):
    t, d = u.shape
    tm = 1024
    main_idx, tail_idx, src = _in_proj_tile_plan()
    grid_spec = pltpu.PrefetchScalarGridSpec(
        num_scalar_prefetch=3,
        grid=(t // tm, P_WIDTH // IN_TILE),
        in_specs=[pl.BlockSpec((tm, d), lambda i, j, mi, ti, sr: (i, 0)),
                  pl.BlockSpec((d, IN_TILE), lambda i, j, mi, ti, sr: (0, mi[j])),
                  pl.BlockSpec((d, IN_TILE), lambda i, j, mi, ti, sr: (0, ti[j]))],
        out_specs=pl.BlockSpec((tm, IN_TILE), lambda i, j, mi, ti, sr: (i, j)),
    )
    return pl.pallas_call(
        _in_proj_body,
        grid_spec=grid_spec,
        out_shape=jax.ShapeDtypeStruct((t, P_WIDTH), F32),
        compiler_params=_cparams(2),
        name="in_proj",
    )(main_idx, tail_idx, src, u, w, _in_proj_tail_weight(w))


def _reorder_w_uq(w):
    r = w.shape[0]
    g = QK_ROPE_DIM // 2
    w3 = w.reshape(r, B_HEADS, QK_NOPE_DIM + QK_ROPE_DIM)
    zero = jnp.zeros((r, B_HEADS, g), w.dtype)
    w3 = jnp.concatenate([w3[:, :, :QK_NOPE_DIM], w3[:, :, QK_NOPE_DIM:QK_NOPE_DIM + g], zero,
                          w3[:, :, QK_NOPE_DIM + g:], zero], axis=2)
    return w3.reshape(r, B_HEADS * 2 * LANES)


def kernel(x, c, positions, w_ada, b_ada, w_in, rel_bias, q_norm_g, w_uq, kv_norm_g, w_ukv,
           w_o, ln1_g, ln1_b, w_gate, w_up, conv_w, conv_b, w_down, ln2_g, ln2_b):
    batch, seq, d = x.shape
    depth = w_ada.shape[0]
    t = batch * seq
    assert d == D_MODEL and batch <= SUBLANES and seq % 1024 == 0
    alpha = (2 * depth) ** 0.25

    rope_a, rope_b = _rope_tables(positions)
    bias_tiles = _bias_tiles(rel_bias)
    c8 = jnp.zeros((SUBLANES, d), F32).at[:batch].set(c)
    x2 = x.reshape(t, d)

    for l in range(depth):
        mod = _ada(c8, w_ada[l], b_ada[l])
        mod3 = mod[:batch].reshape(batch * 6, 1, d)

        u = _modulate(x2, mod3, seq, shift_idx=0, scale_idx=1)
        p = _in_proj(u, w_in[l])
        y_a = _dsa(p, rope_a, bias_tiles, batch, seq)
        q_lat = _rmsnorm_cols(p, COL_QL, Q_LORA_RANK, q_norm_g[l], "q_rmsnorm")
        kv_lat = _rmsnorm_cols(p, COL_KVL, KV_LORA_RANK, kv_norm_g[l], "kv_rmsnorm")
        q = _matmul(q_lat, _reorder_w_uq(w_uq[l]), BF16, 1024, 512, "q_up_proj")
        kv = _matmul(kv_lat, w_ukv[l], BF16, 1024, 512, "kv_up_proj")
        y_b = _mla(q, kv, p, rope_b, batch, seq)
        mix = _matmul_concat(y_a, y_b, w_o[l], 1024, 512, "out_proj")
        x2, u = _resln(x2, mix, mod3, seq, 2, ln1_g[l], ln1_b[l], alpha, mod_idx=(3, 4))

        hidden = _ffn_gate_up(u, w_gate[l], w_up[l], conv_w[l], conv_b[l], seq)
        y = _ffn_down(hidden, w_down[l])
        x2 = _resln(x2, y, mod3, seq, 5, ln2_g[l], ln2_b[l], alpha)

    return x2.reshape(batch, seq, d)
```

```python
import functools
import math

import jax
import jax.numpy as jnp
import numpy as np
from jax import lax
from jax.experimental import pallas as pl
from jax.experimental.pallas import tpu as pltpu

F32 = jnp.float32
BF16 = jnp.bfloat16

D_MODEL = 4096
A_HEAD_DIM = 128
A_HEADS = 16
IDX_HEADS = 32
IDX_HEAD_DIM = 128
IDX_ROPE_DIM = 64
TOPK_MAX = 256
V_HEAD_DIM = 128
B_HEADS = 16
Q_LORA_RANK = 1024
KV_LORA_RANK = 512
QK_NOPE_DIM = 128
QK_ROPE_DIM = 64
D_FF = 11008
CONV_WIDTH = 3
REL_BUCKETS = 32
REL_MAX_DIST = 128
ROPE_THETA = 10000.0
Q_BLOCK = 128
LN_EPS = 1e-5
RMS_EPS = 1e-6
NEG_INF = -1e30

LANES = 128
SUBLANES = 8
VMEM_LIMIT_BYTES = 56 * 1024 * 1024

IN_TILE = 256
IN_STEP = 2 * IN_TILE
COL_IQ = 0
COL_AQ = 4096
COL_QL = 6144
COL_KVL = 7168
COL_AK = 7680
COL_AV = 7808
COL_IK = 7936
COL_MISC = 8192
MISC_IW = 32
P_WIDTH = 8704
TAIL_WIDTH = Q_LORA_RANK + KV_LORA_RANK + IN_STEP

_NT = (((1,), (1,)), ((), ()))


def _cparams(n_axes):
    return pltpu.CompilerParams(
        dimension_semantics=("arbitrary",) * n_axes,
        vmem_limit_bytes=VMEM_LIMIT_BYTES)


def _ada_body(c_ref, w_ref, b_ref, o_ref):
    c = c_ref[...]
    act = (c / (1.0 + jnp.exp(-c))).astype(BF16)
    o_ref[...] = jnp.dot(act, w_ref[...].astype(BF16),
                         preferred_element_type=F32) + b_ref[...]


def _ada(c8, w, b):
    d, n = w.shape
    tn = 512
    return pl.pallas_call(
        _ada_body,
        grid=(n // tn,),
        in_specs=[pl.BlockSpec((SUBLANES, d), lambda j: (0, 0)),
                  pl.BlockSpec((d, tn), lambda j: (0, j)),
                  pl.BlockSpec((1, tn), lambda j: (0, j))],
        out_specs=pl.BlockSpec((SUBLANES, tn), lambda j: (0, j)),
        out_shape=jax.ShapeDtypeStruct((SUBLANES, n), F32),
        compiler_params=_cparams(1),
        name="ada_proj",
    )(c8, w, b.reshape(1, n))


def _modulate_body(x_ref, sh_ref, sc_ref, o_ref):
    o_ref[...] = (x_ref[...] * (1.0 + sc_ref[0]) + sh_ref[0]).astype(BF16)


def _modulate(x2, mod3, seq, shift_idx, scale_idx):
    t, d = x2.shape
    tm = 256
    per_seq = seq // tm
    return pl.pallas_call(
        _modulate_body,
        grid=(t // tm,),
        in_specs=[pl.BlockSpec((tm, d), lambda i: (i, 0)),
                  pl.BlockSpec((1, 1, d), lambda i: (6 * (i // per_seq) + shift_idx, 0, 0)),
                  pl.BlockSpec((1, 1, d), lambda i: (6 * (i // per_seq) + scale_idx, 0, 0))],
        out_specs=pl.BlockSpec((tm, d), lambda i: (i, 0)),
        out_shape=jax.ShapeDtypeStruct((t, d), BF16),
        compiler_params=_cparams(1),
        name="modulate",
    )(x2, mod3, mod3)


def _mm_body(a_ref, w_ref, o_ref):
    o_ref[...] = jnp.dot(a_ref[...], w_ref[...].astype(BF16),
                         preferred_element_type=F32).astype(o_ref.dtype)


def _matmul(a, w, out_dtype, tm, tn, name):
    m, k = a.shape
    n = w.shape[1]
    return pl.pallas_call(
        _mm_body,
        grid=(m // tm, n // tn),
        in_specs=[pl.BlockSpec((tm, k), lambda i, j: (i, 0)),
                  pl.BlockSpec((k, tn), lambda i, j: (0, j))],
        out_specs=pl.BlockSpec((tm, tn), lambda i, j: (i, j)),
        out_shape=jax.ShapeDtypeStruct((m, n), out_dtype),
        compiler_params=_cparams(2),
        name=name,
    )(a, w)


def _mm2_body(a0_ref, a1_ref, w0_ref, w1_ref, o_ref):
    acc = jnp.dot(a0_ref[...], w0_ref[...].astype(BF16), preferred_element_type=F32)
    acc = acc + jnp.dot(a1_ref[...], w1_ref[...].astype(BF16), preferred_element_type=F32)
    o_ref[...] = acc


def _matmul_concat(a0, a1, w, tm, tn, name):
    m, k0 = a0.shape
    k1 = a1.shape[1]
    assert k0 == k1
    n = w.shape[1]
    return pl.pallas_call(
        _mm2_body,
        grid=(m // tm, n // tn),
        in_specs=[pl.BlockSpec((tm, k0), lambda i, j: (i, 0)),
                  pl.BlockSpec((tm, k1), lambda i, j: (i, 0)),
                  pl.BlockSpec((k0, tn), lambda i, j: (0, j)),
                  pl.BlockSpec((k1, tn), lambda i, j: (1, j))],
        out_specs=pl.BlockSpec((tm, tn), lambda i, j: (i, j)),
        out_shape=jax.ShapeDtypeStruct((m, n), F32),
        compiler_params=_cparams(2),
        name=name,
    )(a0, a1, w, w)


def _rms_body(x_ref, g_ref, o_ref):
    x = x_ref[...]
    ms = jnp.mean(x * x, axis=-1, keepdims=True)
    o_ref[...] = (x * lax.rsqrt(ms + RMS_EPS) * g_ref[...]).astype(BF16)


def _rmsnorm_cols(p, col, width, gain, name):
    t = p.shape[0]
    tm = 512
    cb = col // width
    return pl.pallas_call(
        _rms_body,
        grid=(t // tm,),
        in_specs=[pl.BlockSpec((tm, width), lambda i: (i, cb)),
                  pl.BlockSpec((1, width), lambda i: (0, 0))],
        out_specs=pl.BlockSpec((tm, width), lambda i: (i, 0)),
        out_shape=jax.ShapeDtypeStruct((t, width), BF16),
        compiler_params=_cparams(1),
        name=name,
    )(p, gain.reshape(1, width))


def _resln_body(x_ref, y_ref, gate_ref, lg_ref, lb_ref, *rest, alpha, with_mod):
    z = alpha * x_ref[...] + (1.0 + gate_ref[0]) * y_ref[...]
    mu = jnp.mean(z, axis=-1, keepdims=True)
    zc = z - mu
    var = jnp.mean(zc * zc, axis=-1, keepdims=True)
    out = zc * lax.rsqrt(var + LN_EPS) * lg_ref[...] + lb_ref[...]
    if with_mod:
        sh_ref, sc_ref, o_ref, u_ref = rest
        u_ref[...] = (out * (1.0 + sc_ref[0]) + sh_ref[0]).astype(BF16)
    else:
        (o_ref,) = rest
    o_ref[...] = out


def _resln(x2, y, mod3, seq, gate_idx, ln_g, ln_b, alpha, mod_idx=None):
    t, d = x2.shape
    tm = 256
    per_seq = seq // tm
    with_mod = mod_idx is not None
    row = pl.BlockSpec((tm, d), lambda i: (i, 0))
    vec = pl.BlockSpec((1, d), lambda i: (0, 0))

    def mod_spec(idx):
        return pl.BlockSpec((1, 1, d), lambda i: (6 * (i // per_seq) + idx, 0, 0))

    in_specs = [row, row, mod_spec(gate_idx), vec, vec]
    args = [x2, y, mod3, ln_g.reshape(1, d), ln_b.reshape(1, d)]
    out_specs = row
    out_shape = jax.ShapeDtypeStruct((t, d), F32)
    if with_mod:
        in_specs += [mod_spec(mod_idx[0]), mod_spec(mod_idx[1])]
        args += [mod3, mod3]
        out_specs = [row, row]
        out_shape = [out_shape, jax.ShapeDtypeStruct((t, d), BF16)]
    return pl.pallas_call(
        functools.partial(_resln_body, alpha=alpha, with_mod=with_mod),
        grid=(t // tm,),
        in_specs=in_specs,
        out_specs=out_specs,
        out_shape=out_shape,
        compiler_params=_cparams(1),
        name="residual_ln",
    )(*args)


def _ffn_gu_body(a_ref, halo_ref, wg_ref, wu_ref, cw_ref, cb_ref, o_ref, *, tiles_per_seq):
    i = pl.program_id(0)
    a = a_ref[...]
    wg = wg_ref[...].astype(BF16)
    wu = wu_ref[...].astype(BF16)
    g = jnp.dot(a, wg, preferred_element_type=F32)
    up = jnp.dot(a, wu, preferred_element_type=F32)
    gh = jnp.dot(halo_ref[...], wg, preferred_element_type=F32)
    gh = jnp.where(i % tiles_per_seq == 0, 0.0, gh)
    prev1 = gh[SUBLANES - 1:SUBLANES, :]
    prev2 = gh[SUBLANES - 2:SUBLANES - 1, :]
    row = lax.broadcasted_iota(jnp.int32, g.shape, 0)
    g1 = jnp.where(row == 0, prev1, pltpu.roll(g, 1, 0))
    g2 = jnp.where(row == 0, prev2, jnp.where(row == 1, prev1, pltpu.roll(g, 2, 0)))
    cw = cw_ref[...]
    conv = cb_ref[...] + cw[0:1, :] * g2 + cw[1:2, :] * g1 + cw[2:3, :] * g
    o_ref[...] = (conv / (1.0 + jnp.exp(-conv)) * up).astype(BF16)


def _ffn_gate_up(u, w_gate, w_up, conv_w, conv_b, seq):
    t, d = u.shape
    f = w_gate.shape[1]
    tm, tn = 1024, 256
    halo_blocks = tm // SUBLANES
    return pl.pallas_call(
        functools.partial(_ffn_gu_body, tiles_per_seq=seq // tm),
        grid=(t // tm, f // tn),
        in_specs=[pl.BlockSpec((tm, d), lambda i, j: (i, 0)),
                  pl.BlockSpec((SUBLANES, d), lambda i, j: (jnp.maximum(i * halo_blocks - 1, 0), 0)),
                  pl.BlockSpec((d, tn), lambda i, j: (0, j)),
                  pl.BlockSpec((d, tn), lambda i, j: (0, j)),
                  pl.BlockSpec((CONV_WIDTH, tn), lambda i, j: (0, j)),
                  pl.BlockSpec((1, tn), lambda i, j: (0, j))],
        out_specs=pl.BlockSpec((tm, tn), lambda i, j: (i, j)),
        out_shape=jax.ShapeDtypeStruct((t, f), BF16),
        compiler_params=_cparams(2),
        name="ffn_gate_up",
    )(u, u, w_gate, w_up, conv_w, conv_b.reshape(1, f))


def _ffn_down_body(am_ref, at_ref, wm_ref, wt_ref, o_ref, acc_ref, *, n_main, n_tail):
    k = pl.program_id(2)

    @pl.when(k == 0)
    def _():
        acc_ref[...] = jnp.zeros_like(acc_ref)

    @pl.when(k < n_main)
    def _():
        acc_ref[...] += jnp.dot(am_ref[...], wm_ref[...].astype(BF16),
                                preferred_element_type=F32)

    @pl.when(k >= n_main)
    def _():
        acc_ref[...] += jnp.dot(at_ref[...], wt_ref[...].astype(BF16),
                                preferred_element_type=F32)

    @pl.when(k == n_main + n_tail - 1)
    def _():
        o_ref[...] = acc_ref[...]


def _ffn_down(h, w_down):
    t, f = h.shape
    d = w_down.shape[1]
    tm, tn, tk, tk_tail = 2048, 1024, 1024, 256
    n_main = f // tk
    n_tail = (f - n_main * tk) // tk_tail
    assert n_main * tk + n_tail * tk_tail == f and n_tail > 0
    tail0 = n_main * tk // tk_tail

    def main_k(k):
        return jnp.minimum(k, n_main - 1)

    def tail_k(k):
        return tail0 + jnp.maximum(k - n_main, 0)

    return pl.pallas_call(
        functools.partial(_ffn_down_body, n_main=n_main, n_tail=n_tail),
        grid=(t // tm, d // tn, n_main + n_tail),
        in_specs=[pl.BlockSpec((tm, tk), lambda i, j, k: (i, main_k(k))),
                  pl.BlockSpec((tm, tk_tail), lambda i, j, k: (i, tail_k(k))),
                  pl.BlockSpec((tk, tn), lambda i, j, k: (main_k(k), j)),
                  pl.BlockSpec((tk_tail, tn), lambda i, j, k: (tail_k(k), j))],
        out_specs=pl.BlockSpec((tm, tn), lambda i, j, k: (i, j)),
        out_shape=jax.ShapeDtypeStruct((t, d), F32),
        scratch_shapes=[pltpu.VMEM((tm, tn), F32)],
        compiler_params=_cparams(3),
        name="ffn_down",
    )(h, h, w_down, w_down)


ROPE_HALF = IDX_ROPE_DIM // 2
ROLLS_SPLIT = (LANES - ROPE_HALF, ROPE_HALF)
ROLLS_SPREAD = (LANES // 2,)


def _rope_tile(x, tabs, shifts):
    out = x * tabs[0]
    for n, shift in enumerate(shifts):
        out = out + pltpu.roll(x, shift, 1) * tabs[1 + n]
    return out


KCHUNK = 2 * Q_BLOCK
N_BIAS_TILES = 3
LOG2E = math.log2(math.e)


def _bias_tiles_body(rb_ref, o_ref):
    o = pl.program_id(0)
    tq = lax.broadcasted_iota(jnp.int32, (Q_BLOCK, KCHUNK), 0)
    sk = lax.broadcasted_iota(jnp.int32, (Q_BLOCK, KCHUNK), 1)
    n = jnp.maximum(o * Q_BLOCK + tq - sk, 0)
    max_exact = REL_BUCKETS // 2
    nf = jnp.maximum(n, 1).astype(F32)
    large = max_exact + (jnp.log(nf / max_exact) / math.log(REL_MAX_DIST / max_exact)
                         * (REL_BUCKETS - max_exact)).astype(jnp.int32)
    large = jnp.minimum(large, REL_BUCKETS - 1)
    bucket = jnp.where(n < max_exact, n, large)
    hits = [bucket == b for b in range(REL_BUCKETS - 1)]
    for h in range(A_HEADS):
        far = rb_ref[REL_BUCKETS - 1, h]
        tile = jnp.zeros((Q_BLOCK, KCHUNK), F32)
        for b in range(REL_BUCKETS - 1):
            tile = jnp.where(hits[b], (rb_ref[b, h] - far) * LOG2E, tile)
        o_ref[0, h] = tile


def _bias_tiles(rel_bias):
    return pl.pallas_call(
        _bias_tiles_body,
        grid=(N_BIAS_TILES,),
        in_specs=[pl.BlockSpec(memory_space=pltpu.SMEM)],
        out_specs=pl.BlockSpec((1, A_HEADS, Q_BLOCK, KCHUNK), lambda o: (o, 0, 0, 0)),
        out_shape=jax.ShapeDtypeStruct((N_BIAS_TILES, A_HEADS, Q_BLOCK, KCHUNK), F32),
        compiler_params=_cparams(1),
        name="rel_bias_tiles",
    )(rel_bias)


SEL_ROWS = 512
SEL_SLAB = 64


def _dsa_body(iq_ref, aq_ref, ak_ref, av_ref, ik_ref, misc_ref, rope_ref, bias_ref,
              o_ref,
              ikr_ref, akb_ref, vaug_ref, iq2_ref, wb_ref, st_ref, madd_ref, q2_ref,
              m_ref, l_ref, acc_ref, thr_ref,
              *, topk, seq):
    i = pl.program_id(1)
    n_chunks = i // 2 + 1
    t0 = i * Q_BLOCK
    rope = (rope_ref[0], rope_ref[1], rope_ref[2])

    @pl.when(i == 0)
    def _():
        ikr_ref[...] = jnp.zeros_like(ikr_ref)
        akb_ref[...] = ak_ref[...].astype(BF16)
        vaug_ref[:, :A_HEAD_DIM] = av_ref[...].astype(BF16)
        vaug_ref[:, A_HEAD_DIM:] = jnp.ones((seq, LANES), BF16)

    ikr_ref[pl.ds(pl.multiple_of(t0, Q_BLOCK), Q_BLOCK), :] = _rope_tile(
        ik_ref[...], rope, ROLLS_SPLIT).astype(BF16)

    for h in range(IDX_HEADS):
        sl = slice(h * Q_BLOCK, (h + 1) * Q_BLOCK)
        iq2_ref[sl, :] = _rope_tile(iq_ref[:, sl], rope, ROLLS_SPLIT).astype(BF16)

    w_scale = (IDX_HEADS ** -0.5) * (IDX_HEAD_DIM ** -0.5)
    for h in range(IDX_HEADS):
        col = misc_ref[:, MISC_IW + h:MISC_IW + h + 1] * w_scale
        wb_ref[h] = jnp.broadcast_to(col, (Q_BLOCK, LANES))

    heads_per_dot = 4

    def score_chunk(jj, carry):
        k0 = pl.multiple_of(jj * KCHUNK, KCHUNK)
        kc = ikr_ref[pl.ds(k0, KCHUNK), :]
        sc = jnp.zeros((Q_BLOCK, KCHUNK), F32)
        for g in range(IDX_HEADS // heads_per_dot):
            rows = slice(g * heads_per_dot * Q_BLOCK, (g + 1) * heads_per_dot * Q_BLOCK)
            d = lax.dot_general(iq2_ref[rows, :], kc, _NT, preferred_element_type=F32)
            for hh in range(heads_per_dot):
                w = wb_ref[g * heads_per_dot + hh]
                dh = jnp.maximum(d[hh * Q_BLOCK:(hh + 1) * Q_BLOCK, :], 0.0)
                sc = sc + dh * jnp.concatenate([w, w], axis=1)
        tq = t0 + lax.broadcasted_iota(jnp.int32, sc.shape, 0)
        sk = k0 + lax.broadcasted_iota(jnp.int32, sc.shape, 1)
        sc = jnp.where(sk <= tq, sc, NEG_INF)
        st_ref[pl.ds(k0, Q_BLOCK), :] = sc[:, :Q_BLOCK].T
        st_ref[pl.ds(k0 + Q_BLOCK, Q_BLOCK), :] = sc[:, Q_BLOCK:].T
        return carry

    lax.fori_loop(0, n_chunks, score_chunk, 0)

    @pl.when(n_chunks % (SEL_ROWS // KCHUNK) == 1)
    def _():
        pad0 = pl.multiple_of(n_chunks * KCHUNK, KCHUNK)
        st_ref[pl.ds(pad0, KCHUNK), :] = jnp.full((KCHUNK, Q_BLOCK), NEG_INF, F32)

    def key_to_f32(key):
        bits = key ^ ((key >> 31) & jnp.int32(0x7FFFFFFF))
        return lax.bitcast_convert_type(bits, F32)

    kf = float(topk)

    def write_mask(k0, rows, keep_fn):
        blk = st_ref[pl.ds(k0, rows), :]
        sk = k0 + lax.broadcasted_iota(jnp.int32, blk.shape, 0)
        tq = t0 + lax.broadcasted_iota(jnp.int32, blk.shape, 1)
        keep = keep_fn(blk, sk) & (sk <= tq)
        madd_ref[pl.ds(k0, rows), :] = jnp.where(keep, 0.0, NEG_INF)

    def select_static(rows):
        def count_ge(thr_row):
            thr_b = jnp.broadcast_to(thr_row, (SEL_SLAB, LANES))
            acc = jnp.zeros((SEL_SLAB, LANES), F32)
            for r in range(rows // SEL_SLAB):
                blk = st_ref[r * SEL_SLAB:(r + 1) * SEL_SLAB, :]
                acc = jnp.where(blk >= thr_b, acc + 1.0, acc)
            acc = jnp.sum(acc.reshape(SEL_SLAB // SUBLANES, SUBLANES, LANES), axis=0)
            return jnp.sum(acc, axis=0, keepdims=True)

        def select_pass(p, key):
            cand = key + jnp.left_shift(jnp.int32(1), 31 - p)
            return jnp.where(count_ge(key_to_f32(cand)) >= kf, cand, key)

        key0 = jnp.full((1, LANES), jnp.iinfo(jnp.int32).min, jnp.int32)
        thr = key_to_f32(lax.fori_loop(0, 32, select_pass, key0))
        thr_ref[0:1, :] = thr
        thr_ref[1:2, :] = count_ge(thr)
        write_mask(0, rows, lambda blk, sk: blk >= thr)

    for n in range(1, seq // SEL_ROWS + 1):
        pl.when(i // (SEL_ROWS // Q_BLOCK) + 1 == n)(
            functools.partial(select_static, n * SEL_ROWS))

    thr = thr_ref[0:1, :]
    tie = jnp.where((thr_ref[1:2, :] > kf) & (thr > NEG_INF), 1.0, 0.0)

    @pl.when(jnp.max(tie) > 0.0)
    def _():
        def count(pred):
            def body(jj, acc):
                k0 = pl.multiple_of(jj * KCHUNK, KCHUNK)
                hit = jnp.where(pred(st_ref[pl.ds(k0, KCHUNK), :], k0), 1.0, 0.0)
                return acc + jnp.sum(hit.reshape(KCHUNK // SUBLANES, SUBLANES, LANES), axis=0)
            acc = lax.fori_loop(0, n_chunks, body, jnp.zeros((SUBLANES, LANES), F32))
            return jnp.sum(acc, axis=0, keepdims=True)

        def key_index(shape, k0):
            return k0 + lax.broadcasted_iota(jnp.int32, shape, 0)

        need = kf - count(lambda blk, k0: blk > thr)

        def index_pass(p, lo):
            cand = lo + jnp.left_shift(jnp.int32(1), 10 - p)
            n_eq = count(lambda blk, k0: (blk == thr) & (key_index(blk.shape, k0) < cand))
            return jnp.where(n_eq < need, cand, lo)

        lo = lax.fori_loop(0, 11, index_pass, jnp.zeros((1, LANES), jnp.int32))

        def rewrite(jj, carry):
            write_mask(pl.multiple_of(jj * KCHUNK, KCHUNK), KCHUNK,
                       lambda blk, sk: (blk > thr) | ((blk == thr) & (sk <= lo)))
            return carry

        lax.fori_loop(0, n_chunks, rewrite, 0)

    q_scale = (A_HEAD_DIM ** -0.5) * LOG2E
    for h in range(A_HEADS):
        sl = slice(h * Q_BLOCK, (h + 1) * Q_BLOCK)
        q2_ref[sl, :] = (aq_ref[:, sl] * q_scale).astype(BF16)
    m_ref[...] = jnp.full_like(m_ref, NEG_INF)
    l_ref[...] = jnp.zeros_like(l_ref)
    acc_ref[...] = jnp.zeros_like(acc_ref)

    heads_per_att = 4

    def att_chunk(jj, carry, near):
        k0 = pl.multiple_of(jj * KCHUNK, KCHUNK)
        kc = akb_ref[pl.ds(k0, KCHUNK), :]
        vc = vaug_ref[pl.ds(k0, KCHUNK), :]
        sel = jnp.concatenate([madd_ref[pl.ds(k0, Q_BLOCK), :].T,
                               madd_ref[pl.ds(k0 + Q_BLOCK, Q_BLOCK), :].T], axis=1)
        for g in range(A_HEADS // heads_per_att):
            rows = slice(g * heads_per_att * Q_BLOCK, (g + 1) * heads_per_att * Q_BLOCK)
            lg = lax.dot_general(q2_ref[rows, :], kc, _NT, preferred_element_type=F32)
            ps = []
            alphas = []
            for hh in range(heads_per_att):
                h = g * heads_per_att + hh
                hs = slice(h * Q_BLOCK, (h + 1) * Q_BLOCK)
                s = lg[hh * Q_BLOCK:(hh + 1) * Q_BLOCK, :] + sel
                if near:
                    s = s + bias_ref[i - 2 * jj, h]
                m_old = m_ref[hs, :]
                m_new = jnp.maximum(m_old, jnp.max(s, axis=1, keepdims=True))
                alphas.append(jnp.exp2(m_old - m_new))
                m_ref[hs, :] = m_new
                p = jnp.exp2(s - jnp.concatenate([m_new, m_new], axis=1))
                ps.append(p.astype(BF16))
            pv = jnp.dot(jnp.concatenate(ps, axis=0), vc, preferred_element_type=F32)
            for hh in range(heads_per_att):
                h = g * heads_per_att + hh
                hs = slice(h * Q_BLOCK, (h + 1) * Q_BLOCK)
                pv_h = pv[hh * Q_BLOCK:(hh + 1) * Q_BLOCK, :]
                acc_ref[hs, :] = alphas[hh] * acc_ref[hs, :] + pv_h[:, :A_HEAD_DIM]
                l_ref[hs, :] = alphas[hh] * l_ref[hs, :] + pv_h[:, A_HEAD_DIM:]
        return carry

    first_near = jnp.maximum(i - N_BIAS_TILES + 2, 0) // 2
    lax.fori_loop(0, first_near, functools.partial(att_chunk, near=False), 0)
    lax.fori_loop(first_near, n_chunks, functools.partial(att_chunk, near=True), 0)

    for h in range(A_HEADS):
        hs = slice(h * Q_BLOCK, (h + 1) * Q_BLOCK)
        o_ref[:, hs] = (acc_ref[hs, :] / l_ref[hs, :]).astype(BF16)


def _dsa(p, rope_a, bias_tiles, batch, seq):
    t = p.shape[0]
    nb = seq // Q_BLOCK
    topk = min(TOPK_MAX, seq // 4)
    assert topk <= KCHUNK and seq % SEL_ROWS == 0 and seq <= 2048
    qrow = lambda b, i: b * nb + i
    stat = pltpu.VMEM((A_HEADS * Q_BLOCK, LANES), F32)
    return pl.pallas_call(
        functools.partial(_dsa_body, topk=topk, seq=seq),
        grid=(batch, nb),
        in_specs=[
            pl.BlockSpec((Q_BLOCK, IDX_HEADS * IDX_HEAD_DIM), lambda b, i: (qrow(b, i), COL_IQ // 4096)),
            pl.BlockSpec((Q_BLOCK, A_HEADS * A_HEAD_DIM), lambda b, i: (qrow(b, i), COL_AQ // 2048)),
            pl.BlockSpec((seq, LANES), lambda b, i: (b, COL_AK // LANES)),
            pl.BlockSpec((seq, LANES), lambda b, i: (b, COL_AV // LANES)),
            pl.BlockSpec((Q_BLOCK, LANES), lambda b, i: (qrow(b, i), COL_IK // LANES)),
            pl.BlockSpec((Q_BLOCK, LANES), lambda b, i: (qrow(b, i), COL_MISC // LANES)),
            pl.BlockSpec((3, Q_BLOCK, LANES), lambda b, i: (0, qrow(b, i), 0)),
            pl.BlockSpec(bias_tiles.shape, lambda b, i: (0, 0, 0, 0)),
        ],
        out_specs=pl.BlockSpec((Q_BLOCK, A_HEADS * A_HEAD_DIM), lambda b, i: (qrow(b, i), 0)),
        out_shape=jax.ShapeDtypeStruct((t, A_HEADS * A_HEAD_DIM), BF16),
        scratch_shapes=[
            pltpu.VMEM((seq, IDX_HEAD_DIM), BF16),
            pltpu.VMEM((seq, A_HEAD_DIM), BF16),
            pltpu.VMEM((seq, 2 * LANES), BF16),
            pltpu.VMEM((IDX_HEADS * Q_BLOCK, IDX_HEAD_DIM), BF16),
            pltpu.VMEM((IDX_HEADS, Q_BLOCK, LANES), F32),
            pltpu.VMEM((seq, Q_BLOCK), F32),
            pltpu.VMEM((seq, Q_BLOCK), F32),
            pltpu.VMEM((A_HEADS * Q_BLOCK, A_HEAD_DIM), BF16),
            stat, stat, stat,
            pltpu.VMEM((SUBLANES, LANES), F32),
        ],
        compiler_params=_cparams(2),
        name="dsa_attention",
    )(p, p, p, p, p, p, rope_a, bias_tiles)


MLA_BLOCK = 512
MLA_HEADS_PER_STEP = 4


def _mla_body(q_ref, kv_ref, misc_ref, ropeq_ref, ropek_ref, o_ref, krr_ref, kc_ref, vaug_ref):
    hp = pl.program_id(1)
    qi = pl.program_id(2)
    head_w = QK_NOPE_DIM + LANES
    seq = kv_ref.shape[0]

    @pl.when((hp == 0) & (qi == 0))
    def _():
        krr_ref[...] = _rope_tile(misc_ref[...], (ropek_ref[0], ropek_ref[1]),
                                  ROLLS_SPREAD).astype(BF16)

    @pl.when(qi == 0)
    def _():
        for e in range(MLA_HEADS_PER_STEP):
            c0 = e * head_w
            kc_ref[e, :, :QK_NOPE_DIM] = kv_ref[:, c0:c0 + QK_NOPE_DIM]
            kc_ref[e, :, QK_NOPE_DIM:] = krr_ref[...]
            vaug_ref[e, :, :V_HEAD_DIM] = kv_ref[:, c0 + QK_NOPE_DIM:c0 + head_w]
            vaug_ref[e, :, V_HEAD_DIM:] = jnp.ones((seq, LANES), BF16)

    scale = (QK_NOPE_DIM + QK_ROPE_DIM) ** -0.5 * LOG2E
    qcs = []
    for e in range(MLA_HEADS_PER_STEP):
        c0 = e * head_w
        q_rope = _rope_tile(q_ref[:, c0 + QK_NOPE_DIM:c0 + head_w].astype(F32),
                            (ropeq_ref[0], ropeq_ref[1]), ROLLS_SPREAD)
        qc = jnp.concatenate([q_ref[:, c0:c0 + QK_NOPE_DIM].astype(F32), q_rope], axis=1) * scale
        qcs.append(qc.astype(BF16))
    lane_tiles = MLA_BLOCK // LANES

    def step(j, carry, masked):
        k0 = pl.multiple_of(j * MLA_BLOCK, MLA_BLOCK)
        out = []
        for e in range(MLA_HEADS_PER_STEP):
            m_old, l_old, acc = carry[e]
            s = lax.dot_general(qcs[e], kc_ref[e, pl.ds(k0, MLA_BLOCK), :], _NT,
                                preferred_element_type=F32)
            if masked:
                tq = lax.broadcasted_iota(jnp.int32, s.shape, 0)
                sk = lax.broadcasted_iota(jnp.int32, s.shape, 1)
                s = jnp.where(sk <= tq, s, NEG_INF)
            m_new = jnp.maximum(m_old, jnp.max(s, axis=1, keepdims=True))
            alpha = jnp.exp2(m_old - m_new)
            p = jnp.exp2(s - jnp.concatenate([m_new] * lane_tiles, axis=1))
            pv = jnp.dot(p.astype(BF16), vaug_ref[e, pl.ds(k0, MLA_BLOCK), :],
                         preferred_element_type=F32)
            out.append((m_new, alpha * l_old + pv[:, V_HEAD_DIM:],
                        alpha * acc + pv[:, :V_HEAD_DIM]))
        return tuple(out)

    init = (jnp.full((MLA_BLOCK, LANES), NEG_INF, F32),
            jnp.zeros((MLA_BLOCK, LANES), F32),
            jnp.zeros((MLA_BLOCK, V_HEAD_DIM), F32))
    carry = lax.fori_loop(0, qi, functools.partial(step, masked=False),
                          (init,) * MLA_HEADS_PER_STEP)
    carry = step(qi, carry, masked=True)
    for e in range(MLA_HEADS_PER_STEP):
        _, l_fin, acc = carry[e]
        o_ref[:, e * V_HEAD_DIM:(e + 1) * V_HEAD_DIM] = (acc / l_fin).astype(BF16)


def _mla(q, kv, p, rope_b, batch, seq):
    t = q.shape[0]
    nq = seq // MLA_BLOCK
    head_w = QK_NOPE_DIM + LANES
    step_w = MLA_HEADS_PER_STEP * head_w
    qrow = lambda b, hp, qi: b * nq + qi
    return pl.pallas_call(
        _mla_body,
        grid=(batch, B_HEADS // MLA_HEADS_PER_STEP, nq),
        in_specs=[
            pl.BlockSpec((MLA_BLOCK, step_w), lambda b, hp, qi: (qrow(b, hp, qi), hp)),
            pl.BlockSpec((seq, step_w), lambda b, hp, qi: (b, hp)),
            pl.BlockSpec((seq, LANES), lambda b, hp, qi: (b, COL_MISC // LANES)),
            pl.BlockSpec((2, MLA_BLOCK, LANES), lambda b, hp, qi: (0, qrow(b, hp, qi), 0)),
            pl.BlockSpec((2, seq, LANES), lambda b, hp, qi: (0, b, 0)),
        ],
        out_specs=pl.BlockSpec((MLA_BLOCK, MLA_HEADS_PER_STEP * V_HEAD_DIM),
                               lambda b, hp, qi: (qrow(b, hp, qi), hp)),
        out_shape=jax.ShapeDtypeStruct((t, B_HEADS * V_HEAD_DIM), BF16),
        scratch_shapes=[
            pltpu.VMEM((seq, LANES), BF16),
            pltpu.VMEM((MLA_HEADS_PER_STEP, seq, head_w), BF16),
            pltpu.VMEM((MLA_HEADS_PER_STEP, seq, head_w), BF16)],
        compiler_params=_cparams(3),
        name="mla_attention",
    )(q, kv, p, rope_b, rope_b)


def _rope_tables(positions):
    freqs = ROPE_THETA ** (-jnp.arange(ROPE_HALF, dtype=F32) / ROPE_HALF)
    ang = positions.astype(F32).reshape(-1, 1) * freqs
    cos, sin = jnp.cos(ang), jnp.sin(ang)
    one, zero = jnp.ones_like(cos), jnp.zeros_like(cos)
    cat = lambda parts: jnp.concatenate(parts, axis=1)
    rope_a = jnp.stack([cat([one, one, cos, cos]), cat([zero, zero, -sin, zero]),
                        cat([zero, zero, zero, sin])])
    rope_b = jnp.stack([cat([cos, zero, cos, zero]), cat([-sin, zero, sin, zero])])
    return rope_a, rope_b


IN_SPLITS = (A_HEADS * A_HEAD_DIM, A_HEAD_DIM, A_HEAD_DIM, IDX_HEADS * IDX_HEAD_DIM,
             IDX_HEAD_DIM, IDX_HEADS, Q_LORA_RANK, KV_LORA_RANK, QK_ROPE_DIM)
IN_OFFS = [int(v) for v in np.concatenate([[0], np.cumsum(IN_SPLITS)])]


def _in_proj_tail_weight(w):
    iw, ql, kvl, kr = [w[:, IN_OFFS[n]:IN_OFFS[n + 1]] for n in range(5, 9)]
    g = QK_ROPE_DIM // 2
    pad = jnp.zeros((w.shape[0], TAIL_WIDTH - Q_LORA_RANK - KV_LORA_RANK - QK_ROPE_DIM - IDX_HEADS),
                    w.dtype)
    out = jnp.concatenate([ql, kvl, kr[:, :g], iw, kr[:, g:], pad], axis=1)
    assert out.shape[1] == TAIL_WIDTH and MISC_IW == g
    return lax.optimization_barrier(out).astype(BF16)


def _in_proj_step_plan():
    aq0, ak0, iq0, ik0 = (IN_OFFS[n] // IN_TILE for n in (0, 1, 3, 4))
    assert all(IN_OFFS[n] % IN_TILE == 0 for n in (0, 1, 3, 4))
    main = lambda t0, n: [(0, (t0 + 2 * k, t0 + 2 * k + 1)) for k in range(n // IN_STEP)]
    latent_steps = (Q_LORA_RANK + KV_LORA_RANK) // IN_STEP
    plan = (main(iq0, IN_SPLITS[3]) + main(aq0, IN_SPLITS[0])
            + [(1, k) for k in range(latent_steps)]
            + [(0, (ak0, ik0)), (1, latent_steps)])
    assert len(plan) * IN_STEP == P_WIDTH
    held_main, held_tail = (0, 0), 0
    idx_a, idx_b, idx_t = [], [], []
    for src, where in plan:
        if src == 0:
            held_main = where
        else:
            held_tail = where
        idx_a.append(held_main[0])
        idx_b.append(held_main[1])
        idx_t.append(held_tail)
    as_i32 = lambda v: jnp.asarray(np.asarray(v, np.int32))
    return as_i32(idx_a), as_i32(idx_b), as_i32(idx_t), as_i32([src for src, _ in plan])


def _in_proj_body(ia_ref, ib_ref, it_ref, src_ref, a_ref, wa_ref, wb_ref, wt_ref, o_ref):
    j = pl.program_id(1)

    @pl.when(src_ref[j] == 0)
    def _():
        a = a_ref[...]
        o_ref[:, :IN_TILE] = jnp.dot(a, wa_ref[...].astype(BF16), preferred_element_type=F32)
        o_ref[:, IN_TILE:] = jnp.dot(a, wb_ref[...].astype(BF16), preferred_element_type=F32)

    @pl.when(src_ref[j] != 0)
    def _():
        o_ref[...] = jnp.dot(a_ref[...], wt_ref[...], preferred_element_type=F32)


def _in_proj(u, w):
    t, d = u.shape
    tm = 1024
    idx_a, idx_b, idx_t, src = _in_proj_step_plan()
    grid_spec = pltpu.PrefetchScalarGridSpec(
        num_scalar_prefetch=4,
        grid=(t // tm, P_WIDTH // IN_STEP),
        in_specs=[pl.BlockSpec((tm, d), lambda i, j, ia, ib, it, sr: (i, 0)),
                  pl.BlockSpec((d, IN_TILE), lambda i, j, ia, ib, it, sr: (0, ia[j])),
                  pl.BlockSpec((d, IN_TILE), lambda i, j, ia, ib, it, sr: (0, ib[j])),
                  pl.BlockSpec((d, IN_STEP), lambda i, j, ia, ib, it, sr: (0, it[j]))],
        out_specs=pl.BlockSpec((tm, IN_STEP), lambda i, j, ia, ib, it, sr: (i, j)),
    )
    return pl.pallas_call(
        _in_proj_body,
        grid_spec=grid_spec,
        out_shape=jax.ShapeDtypeStruct((t, P_WIDTH), F32),
        compiler_params=_cparams(2),
        name="in_proj",
    )(idx_a, idx_b, idx_t, src, u, w, w, _in_proj_tail_weight(w))


def _reorder_w_uq(w):
    r = w.shape[0]
    g = QK_ROPE_DIM // 2
    w3 = w.reshape(r, B_HEADS, QK_NOPE_DIM + QK_ROPE_DIM)
    zero = jnp.zeros((r, B_HEADS, g), w.dtype)
    w3 = jnp.concatenate([w3[:, :, :QK_NOPE_DIM], w3[:, :, QK_NOPE_DIM:QK_NOPE_DIM + g], zero,
                          w3[:, :, QK_NOPE_DIM + g:], zero], axis=2)
    return w3.reshape(r, B_HEADS * 2 * LANES)


def kernel(x, c, positions, w_ada, b_ada, w_in, rel_bias, q_norm_g, w_uq, kv_norm_g, w_ukv,
           w_o, ln1_g, ln1_b, w_gate, w_up, conv_w, conv_b, w_down, ln2_g, ln2_b):
    batch, seq, d = x.shape
    depth = w_ada.shape[0]
    t = batch * seq
    assert d == D_MODEL and batch <= SUBLANES and seq % 1024 == 0
    alpha = (2 * depth) ** 0.25

    rope_a, rope_b = _rope_tables(positions)
    bias_tiles = _bias_tiles(rel_bias)
    c8 = jnp.zeros((SUBLANES, d), F32).at[:batch].set(c)
    x2 = x.reshape(t, d)

    for l in range(depth):
        mod = _ada(c8, w_ada[l], b_ada[l])
        mod3 = mod[:batch].reshape(batch * 6, 1, d)

        u = _modulate(x2, mod3, seq, shift_idx=0, scale_idx=1)
        p = _in_proj(u, w_in[l])
        y_a = _dsa(p, rope_a, bias_tiles, batch, seq)
        q_lat = _rmsnorm_cols(p, COL_QL, Q_LORA_RANK, q_norm_g[l], "q_rmsnorm")
        kv_lat = _rmsnorm_cols(p, COL_KVL, KV_LORA_RANK, kv_norm_g[l], "kv_rmsnorm")
        q = _matmul(q_lat, _reorder_w_uq(w_uq[l]), BF16, 1024, 512, "q_up_proj")
        kv = _matmul(kv_lat, w_ukv[l], BF16, 1024, 512, "kv_up_proj")
        y_b = _mla(q, kv, p, rope_b, batch, seq)
        mix = _matmul_concat(y_a, y_b, w_o[l], 1024, 512, "out_proj")
        x2, u = _resln(x2, mix, mod3, seq, 2, ln1_g[l], ln1_b[l], alpha, mod_idx=(3, 4))

        hidden = _ffn_gate_up(u, w_gate[l], w_up[l], conv_w[l], conv_b[l], seq)
        y = _ffn_down(hidden, w_down[l])
        x2 = _resln(x2, y, mod3, seq, 5, ln2_g[l], ln2_b[l], alpha)

    return x2.reshape(batch, seq, d)
```

```python
import functools
import math

import jax
import jax.numpy as jnp
import numpy as np
from jax import lax
from jax.experimental import pallas as pl
from jax.experimental.pallas import tpu as pltpu

F32 = jnp.float32
BF16 = jnp.bfloat16

D_MODEL = 4096
A_HEAD_DIM = 128
A_HEADS = 16
IDX_HEADS = 32
IDX_HEAD_DIM = 128
IDX_ROPE_DIM = 64
TOPK_MAX = 256
V_HEAD_DIM = 128
B_HEADS = 16
Q_LORA_RANK = 1024
KV_LORA_RANK = 512
QK_NOPE_DIM = 128
QK_ROPE_DIM = 64
D_FF = 11008
CONV_WIDTH = 3
REL_BUCKETS = 32
REL_MAX_DIST = 128
ROPE_THETA = 10000.0
Q_BLOCK = 128
LN_EPS = 1e-5
RMS_EPS = 1e-6
NEG_INF = -1e30

LANES = 128
SUBLANES = 8
VMEM_LIMIT_BYTES = 56 * 1024 * 1024

IN_TILE = 256
IN_STEP = 2 * IN_TILE
COL_IQ = 0
COL_AQ = 4096
COL_AK = 6144
COL_AV = 6272
COL_IK = 6400
MAIN_WIDTH = 6656
COL_QL = 0
COL_KVL = 1024
COL_MISC = 1536
MISC_IW = 32
TAIL_WIDTH = 2048

_NT = (((1,), (1,)), ((), ()))


def _cparams(n_axes):
    return pltpu.CompilerParams(
        dimension_semantics=("arbitrary",) * n_axes,
        vmem_limit_bytes=VMEM_LIMIT_BYTES)


def _ada_body(c_ref, w_ref, b_ref, o_ref):
    c = c_ref[...]
    act = (c / (1.0 + jnp.exp(-c))).astype(BF16)
    o_ref[...] = jnp.dot(act, w_ref[...].astype(BF16),
                         preferred_element_type=F32) + b_ref[...]


def _ada(c8, w, b):
    d, n = w.shape
    tn = 512
    return pl.pallas_call(
        _ada_body,
        grid=(n // tn,),
        in_specs=[pl.BlockSpec((SUBLANES, d), lambda j: (0, 0)),
                  pl.BlockSpec((d, tn), lambda j: (0, j)),
                  pl.BlockSpec((1, tn), lambda j: (0, j))],
        out_specs=pl.BlockSpec((SUBLANES, tn), lambda j: (0, j)),
        out_shape=jax.ShapeDtypeStruct((SUBLANES, n), F32),
        compiler_params=_cparams(1),
        name="ada_proj",
    )(c8, w, b.reshape(1, n))


def _modulate_body(x_ref, sh_ref, sc_ref, o_ref):
    o_ref[...] = (x_ref[...] * (1.0 + sc_ref[0]) + sh_ref[0]).astype(BF16)


def _modulate(x2, mod3, seq, shift_idx, scale_idx):
    t, d = x2.shape
    tm = 256
    per_seq = seq // tm
    return pl.pallas_call(
        _modulate_body,
        grid=(t // tm,),
        in_specs=[pl.BlockSpec((tm, d), lambda i: (i, 0)),
                  pl.BlockSpec((1, 1, d), lambda i: (6 * (i // per_seq) + shift_idx, 0, 0)),
                  pl.BlockSpec((1, 1, d), lambda i: (6 * (i // per_seq) + scale_idx, 0, 0))],
        out_specs=pl.BlockSpec((tm, d), lambda i: (i, 0)),
        out_shape=jax.ShapeDtypeStruct((t, d), BF16),
        compiler_params=_cparams(1),
        name="modulate",
    )(x2, mod3, mod3)


def _mm_body(a_ref, w_ref, o_ref):
    o_ref[...] = jnp.dot(a_ref[...], w_ref[...].astype(BF16),
                         preferred_element_type=F32).astype(o_ref.dtype)


def _matmul(a, w, out_dtype, tm, tn, name):
    m, k = a.shape
    n = w.shape[1]
    return pl.pallas_call(
        _mm_body,
        grid=(m // tm, n // tn),
        in_specs=[pl.BlockSpec((tm, k), lambda i, j: (i, 0)),
                  pl.BlockSpec((k, tn), lambda i, j: (0, j))],
        out_specs=pl.BlockSpec((tm, tn), lambda i, j: (i, j)),
        out_shape=jax.ShapeDtypeStruct((m, n), out_dtype),
        compiler_params=_cparams(2),
        name=name,
    )(a, w)


def _mm2_body(a0_ref, a1_ref, w0_ref, w1_ref, o_ref):
    acc = jnp.dot(a0_ref[...], w0_ref[...].astype(BF16), preferred_element_type=F32)
    acc = acc + jnp.dot(a1_ref[...], w1_ref[...].astype(BF16), preferred_element_type=F32)
    o_ref[...] = acc


def _matmul_concat(a0, a1, w, tm, tn, name):
    m, k0 = a0.shape
    k1 = a1.shape[1]
    assert k0 == k1
    n = w.shape[1]
    return pl.pallas_call(
        _mm2_body,
        grid=(m // tm, n // tn),
        in_specs=[pl.BlockSpec((tm, k0), lambda i, j: (i, 0)),
                  pl.BlockSpec((tm, k1), lambda i, j: (i, 0)),
                  pl.BlockSpec((k0, tn), lambda i, j: (0, j)),
                  pl.BlockSpec((k1, tn), lambda i, j: (1, j))],
        out_specs=pl.BlockSpec((tm, tn), lambda i, j: (i, j)),
        out_shape=jax.ShapeDtypeStruct((m, n), F32),
        compiler_params=_cparams(2),
        name=name,
    )(a0, a1, w, w)


def _rms_body(x_ref, g_ref, o_ref):
    x = x_ref[...]
    ms = jnp.mean(x * x, axis=-1, keepdims=True)
    o_ref[...] = (x * lax.rsqrt(ms + RMS_EPS) * g_ref[...]).astype(BF16)


def _rmsnorm_cols(p, col, width, gain, name):
    t = p.shape[0]
    tm = 512
    cb = col // width
    return pl.pallas_call(
        _rms_body,
        grid=(t // tm,),
        in_specs=[pl.BlockSpec((tm, width), lambda i: (i, cb)),
                  pl.BlockSpec((1, width), lambda i: (0, 0))],
        out_specs=pl.BlockSpec((tm, width), lambda i: (i, 0)),
        out_shape=jax.ShapeDtypeStruct((t, width), BF16),
        compiler_params=_cparams(1),
        name=name,
    )(p, gain.reshape(1, width))


def _resln_body(x_ref, y_ref, gate_ref, lg_ref, lb_ref, *rest, alpha, with_mod):
    z = alpha * x_ref[...] + (1.0 + gate_ref[0]) * y_ref[...]
    mu = jnp.mean(z, axis=-1, keepdims=True)
    zc = z - mu
    var = jnp.mean(zc * zc, axis=-1, keepdims=True)
    out = zc * lax.rsqrt(var + LN_EPS) * lg_ref[...] + lb_ref[...]
    if with_mod:
        sh_ref, sc_ref, o_ref, u_ref = rest
        u_ref[...] = (out * (1.0 + sc_ref[0]) + sh_ref[0]).astype(BF16)
    else:
        (o_ref,) = rest
    o_ref[...] = out


def _resln(x2, y, mod3, seq, gate_idx, ln_g, ln_b, alpha, mod_idx=None):
    t, d = x2.shape
    tm = 256
    per_seq = seq // tm
    with_mod = mod_idx is not None
    row = pl.BlockSpec((tm, d), lambda i: (i, 0))
    vec = pl.BlockSpec((1, d), lambda i: (0, 0))

    def mod_spec(idx):
        return pl.BlockSpec((1, 1, d), lambda i: (6 * (i // per_seq) + idx, 0, 0))

    in_specs = [row, row, mod_spec(gate_idx), vec, vec]
    args = [x2, y, mod3, ln_g.reshape(1, d), ln_b.reshape(1, d)]
    out_specs = row
    out_shape = jax.ShapeDtypeStruct((t, d), F32)
    if with_mod:
        in_specs += [mod_spec(mod_idx[0]), mod_spec(mod_idx[1])]
        args += [mod3, mod3]
        out_specs = [row, row]
        out_shape = [out_shape, jax.ShapeDtypeStruct((t, d), BF16)]
    return pl.pallas_call(
        functools.partial(_resln_body, alpha=alpha, with_mod=with_mod),
        grid=(t // tm,),
        in_specs=in_specs,
        out_specs=out_specs,
        out_shape=out_shape,
        compiler_params=_cparams(1),
        name="residual_ln",
    )(*args)


def _ffn_gu_body(a_ref, halo_ref, wg_ref, wu_ref, cw_ref, cb_ref, o_ref, *, tiles_per_seq):
    i = pl.program_id(0)
    a = a_ref[...]
    wg = wg_ref[...].astype(BF16)
    wu = wu_ref[...].astype(BF16)
    g = jnp.dot(a, wg, preferred_element_type=F32)
    up = jnp.dot(a, wu, preferred_element_type=F32)
    gh = jnp.dot(halo_ref[...], wg, preferred_element_type=F32)
    gh = jnp.where(i % tiles_per_seq == 0, 0.0, gh)
    prev1 = gh[SUBLANES - 1:SUBLANES, :]
    prev2 = gh[SUBLANES - 2:SUBLANES - 1, :]
    row = lax.broadcasted_iota(jnp.int32, g.shape, 0)
    g1 = jnp.where(row == 0, prev1, pltpu.roll(g, 1, 0))
    g2 = jnp.where(row == 0, prev2, jnp.where(row == 1, prev1, pltpu.roll(g, 2, 0)))
    cw = cw_ref[...]
    conv = cb_ref[...] + cw[0:1, :] * g2 + cw[1:2, :] * g1 + cw[2:3, :] * g
    o_ref[...] = (conv / (1.0 + jnp.exp(-conv)) * up).astype(BF16)


def _ffn_gate_up(u, w_gate, w_up, conv_w, conv_b, seq):
    t, d = u.shape
    f = w_gate.shape[1]
    tm, tn = 1024, 256
    halo_blocks = tm // SUBLANES
    return pl.pallas_call(
        functools.partial(_ffn_gu_body, tiles_per_seq=seq // tm),
        grid=(t // tm, f // tn),
        in_specs=[pl.BlockSpec((tm, d), lambda i, j: (i, 0)),
                  pl.BlockSpec((SUBLANES, d), lambda i, j: (jnp.maximum(i * halo_blocks - 1, 0), 0)),
                  pl.BlockSpec((d, tn), lambda i, j: (0, j)),
                  pl.BlockSpec((d, tn), lambda i, j: (0, j)),
                  pl.BlockSpec((CONV_WIDTH, tn), lambda i, j: (0, j)),
                  pl.BlockSpec((1, tn), lambda i, j: (0, j))],
        out_specs=pl.BlockSpec((tm, tn), lambda i, j: (i, j)),
        out_shape=jax.ShapeDtypeStruct((t, f), BF16),
        compiler_params=_cparams(2),
        name="ffn_gate_up",
    )(u, u, w_gate, w_up, conv_w, conv_b.reshape(1, f))


def _ffn_down_body(am_ref, at_ref, wm_ref, wt_ref, o_ref, acc_ref, *, n_main, n_tail):
    k = pl.program_id(2)

    @pl.when(k == 0)
    def _():
        acc_ref[...] = jnp.zeros_like(acc_ref)

    @pl.when(k < n_main)
    def _():
        acc_ref[...] += jnp.dot(am_ref[...], wm_ref[...].astype(BF16),
                                preferred_element_type=F32)

    @pl.when(k >= n_main)
    def _():
        acc_ref[...] += jnp.dot(at_ref[...], wt_ref[...].astype(BF16),
                                preferred_element_type=F32)

    @pl.when(k == n_main + n_tail - 1)
    def _():
        o_ref[...] = acc_ref[...]


def _ffn_down(h, w_down):
    t, f = h.shape
    d = w_down.shape[1]
    tm, tn, tk, tk_tail = 2048, 1024, 1024, 256
    n_main = f // tk
    n_tail = (f - n_main * tk) // tk_tail
    assert n_main * tk + n_tail * tk_tail == f and n_tail > 0
    tail0 = n_main * tk // tk_tail

    def main_k(k):
        return jnp.minimum(k, n_main - 1)

    def tail_k(k):
        return tail0 + jnp.maximum(k - n_main, 0)

    return pl.pallas_call(
        functools.partial(_ffn_down_body, n_main=n_main, n_tail=n_tail),
        grid=(t // tm, d // tn, n_main + n_tail),
        in_specs=[pl.BlockSpec((tm, tk), lambda i, j, k: (i, main_k(k))),
                  pl.BlockSpec((tm, tk_tail), lambda i, j, k: (i, tail_k(k))),
                  pl.BlockSpec((tk, tn), lambda i, j, k: (main_k(k), j)),
                  pl.BlockSpec((tk_tail, tn), lambda i, j, k: (tail_k(k), j))],
        out_specs=pl.BlockSpec((tm, tn), lambda i, j, k: (i, j)),
        out_shape=jax.ShapeDtypeStruct((t, d), F32),
        scratch_shapes=[pltpu.VMEM((tm, tn), F32)],
        compiler_params=_cparams(3),
        name="ffn_down",
    )(h, h, w_down, w_down)


ROPE_HALF = IDX_ROPE_DIM // 2
ROLLS_SPLIT = (LANES - ROPE_HALF, ROPE_HALF)
ROLLS_SPREAD = (LANES // 2,)


def _rope_tile(x, tabs, shifts):
    out = x * tabs[0]
    for n, shift in enumerate(shifts):
        out = out + pltpu.roll(x, shift, 1) * tabs[1 + n]
    return out


KCHUNK = 2 * Q_BLOCK
N_BIAS_TILES = 3
LOG2E = math.log2(math.e)


def _bias_tiles_body(rb_ref, o_ref):
    o = pl.program_id(0)
    tq = lax.broadcasted_iota(jnp.int32, (Q_BLOCK, KCHUNK), 0)
    sk = lax.broadcasted_iota(jnp.int32, (Q_BLOCK, KCHUNK), 1)
    n = jnp.maximum(o * Q_BLOCK + tq - sk, 0)
    max_exact = REL_BUCKETS // 2
    nf = jnp.maximum(n, 1).astype(F32)
    large = max_exact + (jnp.log(nf / max_exact) / math.log(REL_MAX_DIST / max_exact)
                         * (REL_BUCKETS - max_exact)).astype(jnp.int32)
    large = jnp.minimum(large, REL_BUCKETS - 1)
    bucket = jnp.where(n < max_exact, n, large)
    hits = [bucket == b for b in range(REL_BUCKETS - 1)]
    for h in range(A_HEADS):
        far = rb_ref[REL_BUCKETS - 1, h]
        tile = jnp.zeros((Q_BLOCK, KCHUNK), F32)
        for b in range(REL_BUCKETS - 1):
            tile = jnp.where(hits[b], (rb_ref[b, h] - far) * LOG2E, tile)
        o_ref[0, h] = tile


def _bias_tiles(rel_bias):
    return pl.pallas_call(
        _bias_tiles_body,
        grid=(N_BIAS_TILES,),
        in_specs=[pl.BlockSpec(memory_space=pltpu.SMEM)],
        out_specs=pl.BlockSpec((1, A_HEADS, Q_BLOCK, KCHUNK), lambda o: (o, 0, 0, 0)),
        out_shape=jax.ShapeDtypeStruct((N_BIAS_TILES, A_HEADS, Q_BLOCK, KCHUNK), F32),
        compiler_params=_cparams(1),
        name="rel_bias_tiles",
    )(rel_bias)


SEL_ROWS = 512
SEL_SLAB = 64


def _dsa_body(iq_ref, aq_ref, ak_ref, av_ref, ik_ref, misc_ref, rope_ref, bias_ref,
              o_ref,
              ikr_ref, akb_ref, vaug_ref, iq2_ref, wb_ref, st_ref, madd_ref, q2_ref,
              m_ref, l_ref, acc_ref, thr_ref,
              *, topk, seq):
    i = pl.program_id(1)
    n_chunks = i // 2 + 1
    t0 = i * Q_BLOCK
    rope = (rope_ref[0], rope_ref[1], rope_ref[2])

    @pl.when(i == 0)
    def _():
        ikr_ref[...] = jnp.zeros_like(ikr_ref)
        akb_ref[...] = ak_ref[...].astype(BF16)
        vaug_ref[:, :A_HEAD_DIM] = av_ref[...].astype(BF16)
        vaug_ref[:, A_HEAD_DIM:] = jnp.ones((seq, LANES), BF16)

    ikr_ref[pl.ds(pl.multiple_of(t0, Q_BLOCK), Q_BLOCK), :] = _rope_tile(
        ik_ref[...], rope, ROLLS_SPLIT).astype(BF16)

    for h in range(IDX_HEADS):
        sl = slice(h * Q_BLOCK, (h + 1) * Q_BLOCK)
        iq2_ref[sl, :] = _rope_tile(iq_ref[:, sl], rope, ROLLS_SPLIT).astype(BF16)

    w_scale = (IDX_HEADS ** -0.5) * (IDX_HEAD_DIM ** -0.5)
    for h in range(IDX_HEADS):
        col = misc_ref[:, MISC_IW + h:MISC_IW + h + 1] * w_scale
        wb_ref[h] = jnp.broadcast_to(col, (Q_BLOCK, LANES))

    heads_per_dot = 4

    def score_chunk(jj, carry):
        k0 = pl.multiple_of(jj * KCHUNK, KCHUNK)
        kc = ikr_ref[pl.ds(k0, KCHUNK), :]
        sc = jnp.zeros((Q_BLOCK, KCHUNK), F32)
        for g in range(IDX_HEADS // heads_per_dot):
            rows = slice(g * heads_per_dot * Q_BLOCK, (g + 1) * heads_per_dot * Q_BLOCK)
            d = lax.dot_general(iq2_ref[rows, :], kc, _NT, preferred_element_type=F32)
            for hh in range(heads_per_dot):
                w = wb_ref[g * heads_per_dot + hh]
                dh = jnp.maximum(d[hh * Q_BLOCK:(hh + 1) * Q_BLOCK, :], 0.0)
                sc = sc + dh * jnp.concatenate([w, w], axis=1)
        tq = t0 + lax.broadcasted_iota(jnp.int32, sc.shape, 0)
        sk = k0 + lax.broadcasted_iota(jnp.int32, sc.shape, 1)
        sc = jnp.where(sk <= tq, sc, NEG_INF)
        st_ref[pl.ds(k0, Q_BLOCK), :] = sc[:, :Q_BLOCK].T
        st_ref[pl.ds(k0 + Q_BLOCK, Q_BLOCK), :] = sc[:, Q_BLOCK:].T
        return carry

    lax.fori_loop(0, n_chunks, score_chunk, 0)

    @pl.when(n_chunks % (SEL_ROWS // KCHUNK) == 1)
    def _():
        pad0 = pl.multiple_of(n_chunks * KCHUNK, KCHUNK)
        st_ref[pl.ds(pad0, KCHUNK), :] = jnp.full((KCHUNK, Q_BLOCK), NEG_INF, F32)

    def key_to_f32(key):
        bits = key ^ ((key >> 31) & jnp.int32(0x7FFFFFFF))
        return lax.bitcast_convert_type(bits, F32)

    kf = float(topk)

    def write_mask(k0, rows, keep_fn):
        blk = st_ref[pl.ds(k0, rows), :]
        sk = k0 + lax.broadcasted_iota(jnp.int32, blk.shape, 0)
        tq = t0 + lax.broadcasted_iota(jnp.int32, blk.shape, 1)
        keep = keep_fn(blk, sk) & (sk <= tq)
        madd_ref[pl.ds(k0, rows), :] = jnp.where(keep, 0.0, NEG_INF)

    def select_static(rows):
        def count_ge(thr_row):
            thr_b = jnp.broadcast_to(thr_row, (SEL_SLAB, LANES))
            acc = jnp.zeros((SEL_SLAB, LANES), F32)
            for r in range(rows // SEL_SLAB):
                blk = st_ref[r * SEL_SLAB:(r + 1) * SEL_SLAB, :]
                acc = jnp.where(blk >= thr_b, acc + 1.0, acc)
            acc = jnp.sum(acc.reshape(SEL_SLAB // SUBLANES, SUBLANES, LANES), axis=0)
            return jnp.sum(acc, axis=0, keepdims=True)

        def select_pass(p, key):
            cand = key + jnp.left_shift(jnp.int32(1), 31 - p)
            return jnp.where(count_ge(key_to_f32(cand)) >= kf, cand, key)

        key0 = jnp.full((1, LANES), jnp.iinfo(jnp.int32).min, jnp.int32)
        thr = key_to_f32(lax.fori_loop(0, 32, select_pass, key0))
        thr_ref[0:1, :] = thr
        thr_ref[1:2, :] = count_ge(thr)
        write_mask(0, rows, lambda blk, sk: blk >= thr)

    for n in range(1, seq // SEL_ROWS + 1):
        pl.when(i // (SEL_ROWS // Q_BLOCK) + 1 == n)(
            functools.partial(select_static, n * SEL_ROWS))

    thr = thr_ref[0:1, :]
    tie = jnp.where((thr_ref[1:2, :] > kf) & (thr > NEG_INF), 1.0, 0.0)

    @pl.when(jnp.max(tie) > 0.0)
    def _():
        def count(pred):
            def body(jj, acc):
                k0 = pl.multiple_of(jj * KCHUNK, KCHUNK)
                hit = jnp.where(pred(st_ref[pl.ds(k0, KCHUNK), :], k0), 1.0, 0.0)
                return acc + jnp.sum(hit.reshape(KCHUNK // SUBLANES, SUBLANES, LANES), axis=0)
            acc = lax.fori_loop(0, n_chunks, body, jnp.zeros((SUBLANES, LANES), F32))
            return jnp.sum(acc, axis=0, keepdims=True)

        def key_index(shape, k0):
            return k0 + lax.broadcasted_iota(jnp.int32, shape, 0)

        need = kf - count(lambda blk, k0: blk > thr)

        def index_pass(p, lo):
            cand = lo + jnp.left_shift(jnp.int32(1), 10 - p)
            n_eq = count(lambda blk, k0: (blk == thr) & (key_index(blk.shape, k0) < cand))
            return jnp.where(n_eq < need, cand, lo)

        lo = lax.fori_loop(0, 11, index_pass, jnp.zeros((1, LANES), jnp.int32))

        def rewrite(jj, carry):
            write_mask(pl.multiple_of(jj * KCHUNK, KCHUNK), KCHUNK,
                       lambda blk, sk: (blk > thr) | ((blk == thr) & (sk <= lo)))
            return carry

        lax.fori_loop(0, n_chunks, rewrite, 0)

    q_scale = (A_HEAD_DIM ** -0.5) * LOG2E
    for h in range(A_HEADS):
        sl = slice(h * Q_BLOCK, (h + 1) * Q_BLOCK)
        q2_ref[sl, :] = (aq_ref[:, sl] * q_scale).astype(BF16)
    m_ref[...] = jnp.full_like(m_ref, NEG_INF)
    l_ref[...] = jnp.zeros_like(l_ref)
    acc_ref[...] = jnp.zeros_like(acc_ref)

    heads_per_att = 4

    def att_chunk(jj, carry, near):
        k0 = pl.multiple_of(jj * KCHUNK, KCHUNK)
        kc = akb_ref[pl.ds(k0, KCHUNK), :]
        vc = vaug_ref[pl.ds(k0, KCHUNK), :]
        sel = jnp.concatenate([madd_ref[pl.ds(k0, Q_BLOCK), :].T,
                               madd_ref[pl.ds(k0 + Q_BLOCK, Q_BLOCK), :].T], axis=1)
        for g in range(A_HEADS // heads_per_att):
            rows = slice(g * heads_per_att * Q_BLOCK, (g + 1) * heads_per_att * Q_BLOCK)
            lg = lax.dot_general(q2_ref[rows, :], kc, _NT, preferred_element_type=F32)
            ps = []
            alphas = []
            for hh in range(heads_per_att):
                h = g * heads_per_att + hh
                hs = slice(h * Q_BLOCK, (h + 1) * Q_BLOCK)
                s = lg[hh * Q_BLOCK:(hh + 1) * Q_BLOCK, :] + sel
                if near:
                    s = s + bias_ref[i - 2 * jj, h]
                m_old = m_ref[hs, :]
                m_new = jnp.maximum(m_old, jnp.max(s, axis=1, keepdims=True))
                alphas.append(jnp.exp2(m_old - m_new))
                m_ref[hs, :] = m_new
                p = jnp.exp2(s - jnp.concatenate([m_new, m_new], axis=1))
                ps.append(p.astype(BF16))
            pv = jnp.dot(jnp.concatenate(ps, axis=0), vc, preferred_element_type=F32)
            for hh in range(heads_per_att):
                h = g * heads_per_att + hh
                hs = slice(h * Q_BLOCK, (h + 1) * Q_BLOCK)
                pv_h = pv[hh * Q_BLOCK:(hh + 1) * Q_BLOCK, :]
                acc_ref[hs, :] = alphas[hh] * acc_ref[hs, :] + pv_h[:, :A_HEAD_DIM]
                l_ref[hs, :] = alphas[hh] * l_ref[hs, :] + pv_h[:, A_HEAD_DIM:]
        return carry

    first_near = jnp.maximum(i - N_BIAS_TILES + 2, 0) // 2
    lax.fori_loop(0, first_near, functools.partial(att_chunk, near=False), 0)
    lax.fori_loop(first_near, n_chunks, functools.partial(att_chunk, near=True), 0)

    for h in range(A_HEADS):
        hs = slice(h * Q_BLOCK, (h + 1) * Q_BLOCK)
        o_ref[:, hs] = (acc_ref[hs, :] / l_ref[hs, :]).astype(BF16)


def _dsa(p, p_tail, rope_a, bias_tiles, batch, seq):
    t = p.shape[0]
    nb = seq // Q_BLOCK
    topk = min(TOPK_MAX, seq // 4)
    assert topk <= KCHUNK and seq % SEL_ROWS == 0 and seq <= 2048
    qrow = lambda b, i: b * nb + i
    stat = pltpu.VMEM((A_HEADS * Q_BLOCK, LANES), F32)
    return pl.pallas_call(
        functools.partial(_dsa_body, topk=topk, seq=seq),
        grid=(batch, nb),
        in_specs=[
            pl.BlockSpec((Q_BLOCK, IDX_HEADS * IDX_HEAD_DIM), lambda b, i: (qrow(b, i), COL_IQ // 4096)),
            pl.BlockSpec((Q_BLOCK, A_HEADS * A_HEAD_DIM), lambda b, i: (qrow(b, i), COL_AQ // 2048)),
            pl.BlockSpec((seq, LANES), lambda b, i: (b, COL_AK // LANES)),
            pl.BlockSpec((seq, LANES), lambda b, i: (b, COL_AV // LANES)),
            pl.BlockSpec((Q_BLOCK, LANES), lambda b, i: (qrow(b, i), COL_IK // LANES)),
            pl.BlockSpec((Q_BLOCK, LANES), lambda b, i: (qrow(b, i), COL_MISC // LANES)),
            pl.BlockSpec((3, Q_BLOCK, LANES), lambda b, i: (0, qrow(b, i), 0)),
            pl.BlockSpec(bias_tiles.shape, lambda b, i: (0, 0, 0, 0)),
        ],
        out_specs=pl.BlockSpec((Q_BLOCK, A_HEADS * A_HEAD_DIM), lambda b, i: (qrow(b, i), 0)),
        out_shape=jax.ShapeDtypeStruct((t, A_HEADS * A_HEAD_DIM), BF16),
        scratch_shapes=[
            pltpu.VMEM((seq, IDX_HEAD_DIM), BF16),
            pltpu.VMEM((seq, A_HEAD_DIM), BF16),
            pltpu.VMEM((seq, 2 * LANES), BF16),
            pltpu.VMEM((IDX_HEADS * Q_BLOCK, IDX_HEAD_DIM), BF16),
            pltpu.VMEM((IDX_HEADS, Q_BLOCK, LANES), F32),
            pltpu.VMEM((seq, Q_BLOCK), F32),
            pltpu.VMEM((seq, Q_BLOCK), F32),
            pltpu.VMEM((A_HEADS * Q_BLOCK, A_HEAD_DIM), BF16),
            stat, stat, stat,
            pltpu.VMEM((SUBLANES, LANES), F32),
        ],
        compiler_params=_cparams(2),
        name="dsa_attention",
    )(p, p, p, p, p, p_tail, rope_a, bias_tiles)


MLA_BLOCK = 512
MLA_HEADS_PER_STEP = 4


def _mla_body(q_ref, kv_ref, misc_ref, ropeq_ref, ropek_ref, o_ref, krr_ref, kc_ref, vaug_ref):
    hp = pl.program_id(1)
    qi = pl.program_id(2)
    head_w = QK_NOPE_DIM + LANES
    seq = kv_ref.shape[0]

    @pl.when((hp == 0) & (qi == 0))
    def _():
        krr_ref[...] = _rope_tile(misc_ref[...], (ropek_ref[0], ropek_ref[1]),
                                  ROLLS_SPREAD).astype(BF16)

    @pl.when(qi == 0)
    def _():
        for e in range(MLA_HEADS_PER_STEP):
            c0 = e * head_w
            kc_ref[e, :, :QK_NOPE_DIM] = kv_ref[:, c0:c0 + QK_NOPE_DIM]
            kc_ref[e, :, QK_NOPE_DIM:] = krr_ref[...]
            vaug_ref[e, :, :V_HEAD_DIM] = kv_ref[:, c0 + QK_NOPE_DIM:c0 + head_w]
            vaug_ref[e, :, V_HEAD_DIM:] = jnp.ones((seq, LANES), BF16)

    scale = (QK_NOPE_DIM + QK_ROPE_DIM) ** -0.5 * LOG2E
    qcs = []
    for e in range(MLA_HEADS_PER_STEP):
        c0 = e * head_w
        q_rope = _rope_tile(q_ref[:, c0 + QK_NOPE_DIM:c0 + head_w].astype(F32),
                            (ropeq_ref[0], ropeq_ref[1]), ROLLS_SPREAD)
        qc = jnp.concatenate([q_ref[:, c0:c0 + QK_NOPE_DIM].astype(F32), q_rope], axis=1) * scale
        qcs.append(qc.astype(BF16))
    lane_tiles = MLA_BLOCK // LANES

    def step(j, carry, masked):
        k0 = pl.multiple_of(j * MLA_BLOCK, MLA_BLOCK)
        out = []
        for e in range(MLA_HEADS_PER_STEP):
            m_old, l_old, acc = carry[e]
            s = lax.dot_general(qcs[e], kc_ref[e, pl.ds(k0, MLA_BLOCK), :], _NT,
                                preferred_element_type=F32)
            if masked:
                tq = lax.broadcasted_iota(jnp.int32, s.shape, 0)
                sk = lax.broadcasted_iota(jnp.int32, s.shape, 1)
                s = jnp.where(sk <= tq, s, NEG_INF)
            m_new = jnp.maximum(m_old, jnp.max(s, axis=1, keepdims=True))
            alpha = jnp.exp2(m_old - m_new)
            p = jnp.exp2(s - jnp.concatenate([m_new] * lane_tiles, axis=1))
            pv = jnp.dot(p.astype(BF16), vaug_ref[e, pl.ds(k0, MLA_BLOCK), :],
                         preferred_element_type=F32)
            out.append((m_new, alpha * l_old + pv[:, V_HEAD_DIM:],
                        alpha * acc + pv[:, :V_HEAD_DIM]))
        return tuple(out)

    init = (jnp.full((MLA_BLOCK, LANES), NEG_INF, F32),
            jnp.zeros((MLA_BLOCK, LANES), F32),
            jnp.zeros((MLA_BLOCK, V_HEAD_DIM), F32))
    carry = lax.fori_loop(0, qi, functools.partial(step, masked=False),
                          (init,) * MLA_HEADS_PER_STEP)
    carry = step(qi, carry, masked=True)
    for e in range(MLA_HEADS_PER_STEP):
        _, l_fin, acc = carry[e]
        o_ref[:, e * V_HEAD_DIM:(e + 1) * V_HEAD_DIM] = (acc / l_fin).astype(BF16)


def _mla(q, kv, p, rope_b, batch, seq):
    t = q.shape[0]
    nq = seq // MLA_BLOCK
    head_w = QK_NOPE_DIM + LANES
    step_w = MLA_HEADS_PER_STEP * head_w
    qrow = lambda b, hp, qi: b * nq + qi
    return pl.pallas_call(
        _mla_body,
        grid=(batch, B_HEADS // MLA_HEADS_PER_STEP, nq),
        in_specs=[
            pl.BlockSpec((MLA_BLOCK, step_w), lambda b, hp, qi: (qrow(b, hp, qi), hp)),
            pl.BlockSpec((seq, step_w), lambda b, hp, qi: (b, hp)),
            pl.BlockSpec((seq, LANES), lambda b, hp, qi: (b, COL_MISC // LANES)),
            pl.BlockSpec((2, MLA_BLOCK, LANES), lambda b, hp, qi: (0, qrow(b, hp, qi), 0)),
            pl.BlockSpec((2, seq, LANES), lambda b, hp, qi: (0, b, 0)),
        ],
        out_specs=pl.BlockSpec((MLA_BLOCK, MLA_HEADS_PER_STEP * V_HEAD_DIM),
                               lambda b, hp, qi: (qrow(b, hp, qi), hp)),
        out_shape=jax.ShapeDtypeStruct((t, B_HEADS * V_HEAD_DIM), BF16),
        scratch_shapes=[
            pltpu.VMEM((seq, LANES), BF16),
            pltpu.VMEM((MLA_HEADS_PER_STEP, seq, head_w), BF16),
            pltpu.VMEM((MLA_HEADS_PER_STEP, seq, head_w), BF16)],
        compiler_params=_cparams(3),
        name="mla_attention",
    )(q, kv, p, rope_b, rope_b)


def _rope_tables(positions):
    freqs = ROPE_THETA ** (-jnp.arange(ROPE_HALF, dtype=F32) / ROPE_HALF)
    ang = positions.astype(F32).reshape(-1, 1) * freqs
    cos, sin = jnp.cos(ang), jnp.sin(ang)
    one, zero = jnp.ones_like(cos), jnp.zeros_like(cos)
    cat = lambda parts: jnp.concatenate(parts, axis=1)
    rope_a = jnp.stack([cat([one, one, cos, cos]), cat([zero, zero, -sin, zero]),
                        cat([zero, zero, zero, sin])])
    rope_b = jnp.stack([cat([cos, zero, cos, zero]), cat([-sin, zero, sin, zero])])
    return rope_a, rope_b


IN_SPLITS = (A_HEADS * A_HEAD_DIM, A_HEAD_DIM, A_HEAD_DIM, IDX_HEADS * IDX_HEAD_DIM,
             IDX_HEAD_DIM, IDX_HEADS, Q_LORA_RANK, KV_LORA_RANK, QK_ROPE_DIM)
IN_OFFS = [int(v) for v in np.concatenate([[0], np.cumsum(IN_SPLITS)])]


def _in_proj_tail_weight(wt):
    iw, ql, kvl, kr = [wt[IN_OFFS[n]:IN_OFFS[n + 1]] for n in range(5, 9)]
    g = QK_ROPE_DIM // 2
    pad = jnp.zeros((TAIL_WIDTH - Q_LORA_RANK - KV_LORA_RANK - QK_ROPE_DIM - IDX_HEADS,
                     wt.shape[1]), wt.dtype)
    out = jnp.concatenate([ql, kvl, kr[:g], iw, kr[g:], pad], axis=0)
    assert out.shape[0] == TAIL_WIDTH and MISC_IW == g
    assert (COL_QL, COL_KVL, COL_MISC) == (0, Q_LORA_RANK, Q_LORA_RANK + KV_LORA_RANK)
    return out


def _in_proj_step_plan():
    aq0, ak0, iq0, ik0 = (IN_OFFS[n] // IN_TILE for n in (0, 1, 3, 4))
    assert all(IN_OFFS[n] % IN_TILE == 0 for n in (0, 1, 3, 4))
    pairs = lambda t0, n: [(t0 + 2 * k, t0 + 2 * k + 1) for k in range(n // IN_STEP)]
    plan = pairs(iq0, IN_SPLITS[3]) + pairs(aq0, IN_SPLITS[0]) + [(ak0, ik0)]
    assert len(plan) * IN_STEP == MAIN_WIDTH
    as_i32 = lambda v: jnp.asarray(np.asarray(v, np.int32))
    return as_i32([p[0] for p in plan]), as_i32([p[1] for p in plan])


def _in_proj_body(ia_ref, ib_ref, a_ref, wa_ref, wb_ref, o_ref):
    a = a_ref[...]
    o_ref[:, :IN_TILE] = lax.dot_general(a, wa_ref[...].astype(BF16), _NT,
                                         preferred_element_type=F32)
    o_ref[:, IN_TILE:] = lax.dot_general(a, wb_ref[...].astype(BF16), _NT,
                                         preferred_element_type=F32)


def _in_proj_main(u, wt):
    t, d = u.shape
    tm = 1024
    idx_a, idx_b = _in_proj_step_plan()
    grid_spec = pltpu.PrefetchScalarGridSpec(
        num_scalar_prefetch=2,
        grid=(t // tm, MAIN_WIDTH // IN_STEP),
        in_specs=[pl.BlockSpec((tm, d), lambda i, j, ia, ib: (i, 0)),
                  pl.BlockSpec((IN_TILE, d), lambda i, j, ia, ib: (ia[j], 0)),
                  pl.BlockSpec((IN_TILE, d), lambda i, j, ia, ib: (ib[j], 0))],
        out_specs=pl.BlockSpec((tm, IN_STEP), lambda i, j, ia, ib: (i, j)),
    )
    return pl.pallas_call(
        _in_proj_body,
        grid_spec=grid_spec,
        out_shape=jax.ShapeDtypeStruct((t, MAIN_WIDTH), F32),
        compiler_params=_cparams(2),
        name="in_proj",
    )(idx_a, idx_b, u, wt, wt)


def _mm_nt_body(a_ref, w_ref, o_ref):
    o_ref[...] = lax.dot_general(a_ref[...], w_ref[...].astype(BF16), _NT,
                                 preferred_element_type=F32).astype(o_ref.dtype)


def _matmul_nt(a, wt, out_dtype, tm, tn, name):
    m, k = a.shape
    n = wt.shape[0]
    return pl.pallas_call(
        _mm_nt_body,
        grid=(m // tm, n // tn),
        in_specs=[pl.BlockSpec((tm, k), lambda i, j: (i, 0)),
                  pl.BlockSpec((tn, k), lambda i, j: (j, 0))],
        out_specs=pl.BlockSpec((tm, tn), lambda i, j: (i, j)),
        out_shape=jax.ShapeDtypeStruct((m, n), out_dtype),
        compiler_params=_cparams(2),
        name=name,
    )(a, wt)


def _reorder_w_uq(w):
    r = w.shape[0]
    g = QK_ROPE_DIM // 2
    w3 = w.reshape(r, B_HEADS, QK_NOPE_DIM + QK_ROPE_DIM)
    zero = jnp.zeros((r, B_HEADS, g), w.dtype)
    w3 = jnp.concatenate([w3[:, :, :QK_NOPE_DIM], w3[:, :, QK_NOPE_DIM:QK_NOPE_DIM + g], zero,
                          w3[:, :, QK_NOPE_DIM + g:], zero], axis=2)
    return w3.reshape(r, B_HEADS * 2 * LANES)


def kernel(x, c, positions, w_ada, b_ada, w_in, rel_bias, q_norm_g, w_uq, kv_norm_g, w_ukv,
           w_o, ln1_g, ln1_b, w_gate, w_up, conv_w, conv_b, w_down, ln2_g, ln2_b):
    batch, seq, d = x.shape
    depth = w_ada.shape[0]
    t = batch * seq
    assert d == D_MODEL and batch <= SUBLANES and seq % 1024 == 0
    alpha = (2 * depth) ** 0.25

    rope_a, rope_b = _rope_tables(positions)
    bias_tiles = _bias_tiles(rel_bias)
    c8 = jnp.zeros((SUBLANES, d), F32).at[:batch].set(c)
    x2 = x.reshape(t, d)

    for l in range(depth):
        mod = _ada(c8, w_ada[l], b_ada[l])
        mod3 = mod[:batch].reshape(batch * 6, 1, d)

        u = _modulate(x2, mod3, seq, shift_idx=0, scale_idx=1)
        w_in_t = jnp.swapaxes(w_in[l], 0, 1)
        p = _in_proj_main(u, w_in_t)
        p_tail = _matmul_nt(u, _in_proj_tail_weight(w_in_t), F32, 1024, 512, "in_proj_tail")
        y_a = _dsa(p, p_tail, rope_a, bias_tiles, batch, seq)
        q_lat = _rmsnorm_cols(p_tail, COL_QL, Q_LORA_RANK, q_norm_g[l], "q_rmsnorm")
        kv_lat = _rmsnorm_cols(p_tail, COL_KVL, KV_LORA_RANK, kv_norm_g[l], "kv_rmsnorm")
        q = _matmul(q_lat, _reorder_w_uq(w_uq[l]), BF16, 1024, 512, "q_up_proj")
        kv = _matmul(kv_lat, w_ukv[l], BF16, 1024, 512, "kv_up_proj")
        y_b = _mla(q, kv, p_tail, rope_b, batch, seq)
        mix = _matmul_concat(y_a, y_b, w_o[l], 1024, 512, "out_proj")
        x2, u = _resln(x2, mix, mod3, seq, 2, ln1_g[l], ln1_b[l], alpha, mod_idx=(3, 4))

        hidden = _ffn_gate_up(u, w_gate[l], w_up[l], conv_w[l], conv_b[l], seq)
        y = _ffn_down(hidden, w_down[l])
        x2 = _resln(x2, y, mod3, seq, 5, ln2_g[l], ln2_b[l], alpha)

    return x2.reshape(batch, seq, d)
```

```python
import functools
import math

import jax
import jax.numpy as jnp
import numpy as np
from jax import lax
from jax.experimental import pallas as pl
from jax.experimental.pallas import tpu as pltpu

F32 = jnp.float32
BF16 = jnp.bfloat16

D_MODEL = 4096
A_HEAD_DIM = 128
A_HEADS = 16
IDX_HEADS = 32
IDX_HEAD_DIM = 128
IDX_ROPE_DIM = 64
TOPK_MAX = 256
V_HEAD_DIM = 128
B_HEADS = 16
Q_LORA_RANK = 1024
KV_LORA_RANK = 512
QK_NOPE_DIM = 128
QK_ROPE_DIM = 64
D_FF = 11008
CONV_WIDTH = 3
REL_BUCKETS = 32
REL_MAX_DIST = 128
ROPE_THETA = 10000.0
Q_BLOCK = 128
LN_EPS = 1e-5
RMS_EPS = 1e-6
NEG_INF = -1e30

LANES = 128
SUBLANES = 8
VMEM_LIMIT_BYTES = 56 * 1024 * 1024

IN_TILE = 256
IN_STEP = 2 * IN_TILE
COL_IQ = 0
COL_AQ = 4096
COL_AK = 6144
COL_AV = 6272
COL_IK = 6400
MAIN_WIDTH = 6656
COL_QL = 0
COL_KVL = 1024
COL_MISC = 1536
MISC_IW = 32
TAIL_WIDTH = 2048

_NT = (((1,), (1,)), ((), ()))


def _cparams(n_axes):
    return pltpu.CompilerParams(
        dimension_semantics=("arbitrary",) * n_axes,
        vmem_limit_bytes=VMEM_LIMIT_BYTES)


def _ada_body(c_ref, w_ref, b_ref, o_ref):
    c = c_ref[...]
    act = (c / (1.0 + jnp.exp(-c))).astype(BF16)
    o_ref[...] = jnp.dot(act, w_ref[...].astype(BF16),
                         preferred_element_type=F32) + b_ref[...]


def _ada(c8, w, b):
    d, n = w.shape
    tn = 512
    return pl.pallas_call(
        _ada_body,
        grid=(n // tn,),
        in_specs=[pl.BlockSpec((SUBLANES, d), lambda j: (0, 0)),
                  pl.BlockSpec((d, tn), lambda j: (0, j)),
                  pl.BlockSpec((1, tn), lambda j: (0, j))],
        out_specs=pl.BlockSpec((SUBLANES, tn), lambda j: (0, j)),
        out_shape=jax.ShapeDtypeStruct((SUBLANES, n), F32),
        compiler_params=_cparams(1),
        name="ada_proj",
    )(c8, w, b.reshape(1, n))


def _modulate_body(x_ref, sh_ref, sc_ref, o_ref):
    o_ref[...] = (x_ref[...] * (1.0 + sc_ref[0]) + sh_ref[0]).astype(BF16)


def _modulate(x2, mod3, seq, shift_idx, scale_idx):
    t, d = x2.shape
    tm = 256
    per_seq = seq // tm
    return pl.pallas_call(
        _modulate_body,
        grid=(t // tm,),
        in_specs=[pl.BlockSpec((tm, d), lambda i: (i, 0)),
                  pl.BlockSpec((1, 1, d), lambda i: (6 * (i // per_seq) + shift_idx, 0, 0)),
                  pl.BlockSpec((1, 1, d), lambda i: (6 * (i // per_seq) + scale_idx, 0, 0))],
        out_specs=pl.BlockSpec((tm, d), lambda i: (i, 0)),
        out_shape=jax.ShapeDtypeStruct((t, d), BF16),
        compiler_params=_cparams(1),
        name="modulate",
    )(x2, mod3, mod3)


def _mm_body(a_ref, w_ref, o_ref):
    o_ref[...] = jnp.dot(a_ref[...], w_ref[...].astype(BF16),
                         preferred_element_type=F32).astype(o_ref.dtype)


def _matmul(a, w, out_dtype, tm, tn, name):
    m, k = a.shape
    n = w.shape[1]
    return pl.pallas_call(
        _mm_body,
        grid=(m // tm, n // tn),
        in_specs=[pl.BlockSpec((tm, k), lambda i, j: (i, 0)),
                  pl.BlockSpec((k, tn), lambda i, j: (0, j))],
        out_specs=pl.BlockSpec((tm, tn), lambda i, j: (i, j)),
        out_shape=jax.ShapeDtypeStruct((m, n), out_dtype),
        compiler_params=_cparams(2),
        name=name,
    )(a, w)


def _mm2_body(a0_ref, a1_ref, w0_ref, w1_ref, o_ref):
    acc = jnp.dot(a0_ref[...], w0_ref[...].astype(BF16), preferred_element_type=F32)
    acc = acc + jnp.dot(a1_ref[...], w1_ref[...].astype(BF16), preferred_element_type=F32)
    o_ref[...] = acc


def _matmul_concat(a0, a1, w, tm, tn, name):
    m, k0 = a0.shape
    k1 = a1.shape[1]
    assert k0 == k1
    n = w.shape[1]
    return pl.pallas_call(
        _mm2_body,
        grid=(m // tm, n // tn),
        in_specs=[pl.BlockSpec((tm, k0), lambda i, j: (i, 0)),
                  pl.BlockSpec((tm, k1), lambda i, j: (i, 0)),
                  pl.BlockSpec((k0, tn), lambda i, j: (0, j)),
                  pl.BlockSpec((k1, tn), lambda i, j: (1, j))],
        out_specs=pl.BlockSpec((tm, tn), lambda i, j: (i, j)),
        out_shape=jax.ShapeDtypeStruct((m, n), F32),
        compiler_params=_cparams(2),
        name=name,
    )(a0, a1, w, w)


def _rms_mm_body(x_ref, g_ref, w_ref, o_ref):
    x = x_ref[...]
    ms = jnp.mean(x * x, axis=-1, keepdims=True)
    a = (x * lax.rsqrt(ms + RMS_EPS) * g_ref[...]).astype(BF16)
    o_ref[...] = jnp.dot(a, w_ref[...].astype(BF16),
                         preferred_element_type=F32).astype(o_ref.dtype)


def _rmsnorm_matmul(p, col, gain, w, tm, tn, name):
    t = p.shape[0]
    width, n = w.shape
    cb = col // width
    assert col % width == 0
    return pl.pallas_call(
        _rms_mm_body,
        grid=(t // tm, n // tn),
        in_specs=[pl.BlockSpec((tm, width), lambda i, j: (i, cb)),
                  pl.BlockSpec((1, width), lambda i, j: (0, 0)),
                  pl.BlockSpec((width, tn), lambda i, j: (0, j))],
        out_specs=pl.BlockSpec((tm, tn), lambda i, j: (i, j)),
        out_shape=jax.ShapeDtypeStruct((t, n), BF16),
        compiler_params=_cparams(2),
        name=name,
    )(p, gain.reshape(1, width), w)


def _resln_body(x_ref, y_ref, gate_ref, lg_ref, lb_ref, *rest, alpha, with_mod):
    z = alpha * x_ref[...] + (1.0 + gate_ref[0]) * y_ref[...]
    mu = jnp.mean(z, axis=-1, keepdims=True)
    zc = z - mu
    var = jnp.mean(zc * zc, axis=-1, keepdims=True)
    out = zc * lax.rsqrt(var + LN_EPS) * lg_ref[...] + lb_ref[...]
    if with_mod:
        sh_ref, sc_ref, o_ref, u_ref = rest
        u_ref[...] = (out * (1.0 + sc_ref[0]) + sh_ref[0]).astype(BF16)
    else:
        (o_ref,) = rest
    o_ref[...] = out


def _resln(x2, y, mod3, seq, gate_idx, ln_g, ln_b, alpha, mod_idx=None):
    t, d = x2.shape
    tm = 256
    per_seq = seq // tm
    with_mod = mod_idx is not None
    row = pl.BlockSpec((tm, d), lambda i: (i, 0))
    vec = pl.BlockSpec((1, d), lambda i: (0, 0))

    def mod_spec(idx):
        return pl.BlockSpec((1, 1, d), lambda i: (6 * (i // per_seq) + idx, 0, 0))

    in_specs = [row, row, mod_spec(gate_idx), vec, vec]
    args = [x2, y, mod3, ln_g.reshape(1, d), ln_b.reshape(1, d)]
    out_specs = row
    out_shape = jax.ShapeDtypeStruct((t, d), F32)
    if with_mod:
        in_specs += [mod_spec(mod_idx[0]), mod_spec(mod_idx[1])]
        args += [mod3, mod3]
        out_specs = [row, row]
        out_shape = [out_shape, jax.ShapeDtypeStruct((t, d), BF16)]
    return pl.pallas_call(
        functools.partial(_resln_body, alpha=alpha, with_mod=with_mod),
        grid=(t // tm,),
        in_specs=in_specs,
        out_specs=out_specs,
        out_shape=out_shape,
        compiler_params=_cparams(1),
        name="residual_ln",
    )(*args)


def _ffn_gu_body(a_ref, halo_ref, wg_ref, wu_ref, cw_ref, cb_ref, o_ref, wg_bf_ref, wu_bf_ref,
                 *, tiles_per_seq):
    i = pl.program_id(1)

    @pl.when(i == 0)
    def _():
        wg_bf_ref[...] = wg_ref[...].astype(BF16)
        wu_bf_ref[...] = wu_ref[...].astype(BF16)

    a = a_ref[...]
    wg = wg_bf_ref[...]
    wu = wu_bf_ref[...]
    g = jnp.dot(a, wg, preferred_element_type=F32)
    up = jnp.dot(a, wu, preferred_element_type=F32)
    gh = jnp.dot(halo_ref[...], wg, preferred_element_type=F32)
    gh = jnp.where(i % tiles_per_seq == 0, 0.0, gh)
    prev1 = gh[SUBLANES - 1:SUBLANES, :]
    prev2 = gh[SUBLANES - 2:SUBLANES - 1, :]
    row = lax.broadcasted_iota(jnp.int32, g.shape, 0)
    g1 = jnp.where(row == 0, prev1, pltpu.roll(g, 1, 0))
    g2 = jnp.where(row == 0, prev2, jnp.where(row == 1, prev1, pltpu.roll(g, 2, 0)))
    cw = cw_ref[...]
    conv = cb_ref[...] + cw[0:1, :] * g2 + cw[1:2, :] * g1 + cw[2:3, :] * g
    o_ref[...] = (conv / (1.0 + jnp.exp(-conv)) * up).astype(BF16)


def _ffn_gate_up(u, w_gate, w_up, conv_w, conv_b, seq):
    t, d = u.shape
    f = w_gate.shape[1]
    tm, tn = 1024, 256
    halo_blocks = tm // SUBLANES
    return pl.pallas_call(
        functools.partial(_ffn_gu_body, tiles_per_seq=seq // tm),
        grid=(f // tn, t // tm),
        in_specs=[pl.BlockSpec((tm, d), lambda j, i: (i, 0)),
                  pl.BlockSpec((SUBLANES, d), lambda j, i: (jnp.maximum(i * halo_blocks - 1, 0), 0)),
                  pl.BlockSpec((d, tn), lambda j, i: (0, j)),
                  pl.BlockSpec((d, tn), lambda j, i: (0, j)),
                  pl.BlockSpec((CONV_WIDTH, tn), lambda j, i: (0, j)),
                  pl.BlockSpec((1, tn), lambda j, i: (0, j))],
        out_specs=pl.BlockSpec((tm, tn), lambda j, i: (i, j)),
        out_shape=jax.ShapeDtypeStruct((t, f), BF16),
        scratch_shapes=[pltpu.VMEM((d, tn), BF16), pltpu.VMEM((d, tn), BF16)],
        compiler_params=_cparams(2),
        name="ffn_gate_up",
    )(u, u, w_gate, w_up, conv_w, conv_b.reshape(1, f))


def _ffn_down_body(am_ref, at_ref, wm_ref, wt_ref, o_ref, acc_ref, *, n_main, n_tail):
    k = pl.program_id(2)

    @pl.when(k == 0)
    def _():
        acc_ref[...] = jnp.zeros_like(acc_ref)

    @pl.when(k < n_main)
    def _():
        acc_ref[...] += jnp.dot(am_ref[...], wm_ref[...].astype(BF16),
                                preferred_element_type=F32)

    @pl.when(k >= n_main)
    def _():
        acc_ref[...] += jnp.dot(at_ref[...], wt_ref[...].astype(BF16),
                                preferred_element_type=F32)

    @pl.when(k == n_main + n_tail - 1)
    def _():
        o_ref[...] = acc_ref[...]


def _ffn_down(h, w_down):
    t, f = h.shape
    d = w_down.shape[1]
    tm, tn, tk, tk_tail = 2048, 1024, 1024, 256
    n_main = f // tk
    n_tail = (f - n_main * tk) // tk_tail
    assert n_main * tk + n_tail * tk_tail == f and n_tail > 0
    tail0 = n_main * tk // tk_tail

    def main_k(k):
        return jnp.minimum(k, n_main - 1)

    def tail_k(k):
        return tail0 + jnp.maximum(k - n_main, 0)

    return pl.pallas_call(
        functools.partial(_ffn_down_body, n_main=n_main, n_tail=n_tail),
        grid=(t // tm, d // tn, n_main + n_tail),
        in_specs=[pl.BlockSpec((tm, tk), lambda i, j, k: (i, main_k(k))),
                  pl.BlockSpec((tm, tk_tail), lambda i, j, k: (i, tail_k(k))),
                  pl.BlockSpec((tk, tn), lambda i, j, k: (main_k(k), j)),
                  pl.BlockSpec((tk_tail, tn), lambda i, j, k: (tail_k(k), j))],
        out_specs=pl.BlockSpec((tm, tn), lambda i, j, k: (i, j)),
        out_shape=jax.ShapeDtypeStruct((t, d), F32),
        scratch_shapes=[pltpu.VMEM((tm, tn), F32)],
        compiler_params=_cparams(3),
        name="ffn_down",
    )(h, h, w_down, w_down)


ROPE_HALF = IDX_ROPE_DIM // 2
ROLLS_SPLIT = (LANES - ROPE_HALF, ROPE_HALF)
ROLLS_SPREAD = (LANES // 2,)


def _rope_tile(x, tabs, shifts):
    out = x * tabs[0]
    for n, shift in enumerate(shifts):
        out = out + pltpu.roll(x, shift, 1) * tabs[1 + n]
    return out


KCHUNK = 2 * Q_BLOCK
N_BIAS_TILES = 3
LOG2E = math.log2(math.e)


def _bias_tiles_body(rb_ref, o_ref):
    o = pl.program_id(0)
    tq = lax.broadcasted_iota(jnp.int32, (Q_BLOCK, KCHUNK), 0)
    sk = lax.broadcasted_iota(jnp.int32, (Q_BLOCK, KCHUNK), 1)
    n = jnp.maximum(o * Q_BLOCK + tq - sk, 0)
    max_exact = REL_BUCKETS // 2
    nf = jnp.maximum(n, 1).astype(F32)
    large = max_exact + (jnp.log(nf / max_exact) / math.log(REL_MAX_DIST / max_exact)
                         * (REL_BUCKETS - max_exact)).astype(jnp.int32)
    large = jnp.minimum(large, REL_BUCKETS - 1)
    bucket = jnp.where(n < max_exact, n, large)
    hits = [bucket == b for b in range(REL_BUCKETS - 1)]
    for h in range(A_HEADS):
        far = rb_ref[REL_BUCKETS - 1, h]
        tile = jnp.zeros((Q_BLOCK, KCHUNK), F32)
        for b in range(REL_BUCKETS - 1):
            tile = jnp.where(hits[b], (rb_ref[b, h] - far) * LOG2E, tile)
        o_ref[0, h] = tile


def _bias_tiles(rel_bias):
    return pl.pallas_call(
        _bias_tiles_body,
        grid=(N_BIAS_TILES,),
        in_specs=[pl.BlockSpec(memory_space=pltpu.SMEM)],
        out_specs=pl.BlockSpec((1, A_HEADS, Q_BLOCK, KCHUNK), lambda o: (o, 0, 0, 0)),
        out_shape=jax.ShapeDtypeStruct((N_BIAS_TILES, A_HEADS, Q_BLOCK, KCHUNK), F32),
        compiler_params=_cparams(1),
        name="rel_bias_tiles",
    )(rel_bias)


SEL_ROWS = 512
SEL_SLAB = 64


def _dsa_body(iq_ref, aq_ref, ak_ref, av_ref, ik_ref, misc_ref, rope_ref, bias_ref,
              o_ref,
              ikr_ref, akb_ref, vaug_ref, iq2_ref, wb_ref, st_ref, madd_ref, q2_ref,
              m_ref, l_ref, acc_ref, thr_ref,
              *, topk, seq):
    i = pl.program_id(1)
    n_chunks = i // 2 + 1
    t0 = i * Q_BLOCK
    rope = (rope_ref[0], rope_ref[1], rope_ref[2])

    @pl.when(i == 0)
    def _():
        ikr_ref[...] = jnp.zeros_like(ikr_ref)
        akb_ref[...] = ak_ref[...].astype(BF16)
        vaug_ref[:, :A_HEAD_DIM] = av_ref[...].astype(BF16)
        vaug_ref[:, A_HEAD_DIM:] = jnp.ones((seq, LANES), BF16)

    ikr_ref[pl.ds(pl.multiple_of(t0, Q_BLOCK), Q_BLOCK), :] = _rope_tile(
        ik_ref[...], rope, ROLLS_SPLIT).astype(BF16)

    for h in range(IDX_HEADS):
        sl = slice(h * Q_BLOCK, (h + 1) * Q_BLOCK)
        iq2_ref[sl, :] = _rope_tile(iq_ref[:, sl], rope, ROLLS_SPLIT).astype(BF16)

    w_scale = (IDX_HEADS ** -0.5) * (IDX_HEAD_DIM ** -0.5)
    for h in range(IDX_HEADS):
        col = misc_ref[:, MISC_IW + h:MISC_IW + h + 1] * w_scale
        wb_ref[h] = jnp.broadcast_to(col, (Q_BLOCK, LANES))

    heads_per_dot = 4

    def score_chunk(jj, carry):
        k0 = pl.multiple_of(jj * KCHUNK, KCHUNK)
        kc = ikr_ref[pl.ds(k0, KCHUNK), :]
        sc = jnp.zeros((Q_BLOCK, KCHUNK), F32)
        for g in range(IDX_HEADS // heads_per_dot):
            rows = slice(g * heads_per_dot * Q_BLOCK, (g + 1) * heads_per_dot * Q_BLOCK)
            d = lax.dot_general(iq2_ref[rows, :], kc, _NT, preferred_element_type=F32)
            for hh in range(heads_per_dot):
                w = wb_ref[g * heads_per_dot + hh]
                dh = jnp.maximum(d[hh * Q_BLOCK:(hh + 1) * Q_BLOCK, :], 0.0)
                sc = sc + dh * jnp.concatenate([w, w], axis=1)
        tq = t0 + lax.broadcasted_iota(jnp.int32, sc.shape, 0)
        sk = k0 + lax.broadcasted_iota(jnp.int32, sc.shape, 1)
        sc = jnp.where(sk <= tq, sc, NEG_INF)
        st_ref[pl.ds(k0, Q_BLOCK), :] = sc[:, :Q_BLOCK].T
        st_ref[pl.ds(k0 + Q_BLOCK, Q_BLOCK), :] = sc[:, Q_BLOCK:].T
        return carry

    lax.fori_loop(0, n_chunks, score_chunk, 0)

    @pl.when(n_chunks % (SEL_ROWS // KCHUNK) == 1)
    def _():
        pad0 = pl.multiple_of(n_chunks * KCHUNK, KCHUNK)
        st_ref[pl.ds(pad0, KCHUNK), :] = jnp.full((KCHUNK, Q_BLOCK), NEG_INF, F32)

    def key_to_f32(key):
        bits = key ^ ((key >> 31) & jnp.int32(0x7FFFFFFF))
        return lax.bitcast_convert_type(bits, F32)

    kf = float(topk)

    def write_mask(k0, rows, keep_fn):
        blk = st_ref[pl.ds(k0, rows), :]
        sk = k0 + lax.broadcasted_iota(jnp.int32, blk.shape, 0)
        tq = t0 + lax.broadcasted_iota(jnp.int32, blk.shape, 1)
        keep = keep_fn(blk, sk) & (sk <= tq)
        madd_ref[pl.ds(k0, rows), :] = jnp.where(keep, 0.0, NEG_INF)

    def select_static(rows):
        def count_ge(thr_row):
            thr_b = jnp.broadcast_to(thr_row, (SEL_SLAB, LANES))
            acc = jnp.zeros((SEL_SLAB, LANES), F32)
            for r in range(rows // SEL_SLAB):
                blk = st_ref[r * SEL_SLAB:(r + 1) * SEL_SLAB, :]
                acc = jnp.where(blk >= thr_b, acc + 1.0, acc)
            acc = jnp.sum(acc.reshape(SEL_SLAB // SUBLANES, SUBLANES, LANES), axis=0)
            return jnp.sum(acc, axis=0, keepdims=True)

        def select_pass(p, key):
            cand = key + jnp.left_shift(jnp.int32(1), 31 - p)
            return jnp.where(count_ge(key_to_f32(cand)) >= kf, cand, key)

        key0 = jnp.full((1, LANES), jnp.iinfo(jnp.int32).min, jnp.int32)
        thr = key_to_f32(lax.fori_loop(0, 32, select_pass, key0))
        thr_ref[0:1, :] = thr
        thr_ref[1:2, :] = count_ge(thr)
        write_mask(0, rows, lambda blk, sk: blk >= thr)

    for n in range(1, seq // SEL_ROWS + 1):
        pl.when(i // (SEL_ROWS // Q_BLOCK) + 1 == n)(
            functools.partial(select_static, n * SEL_ROWS))

    thr = thr_ref[0:1, :]
    tie = jnp.where((thr_ref[1:2, :] > kf) & (thr > NEG_INF), 1.0, 0.0)

    @pl.when(jnp.max(tie) > 0.0)
    def _():
        def count(pred):
            def body(jj, acc):
                k0 = pl.multiple_of(jj * KCHUNK, KCHUNK)
                hit = jnp.where(pred(st_ref[pl.ds(k0, KCHUNK), :], k0), 1.0, 0.0)
                return acc + jnp.sum(hit.reshape(KCHUNK // SUBLANES, SUBLANES, LANES), axis=0)
            acc = lax.fori_loop(0, n_chunks, body, jnp.zeros((SUBLANES, LANES), F32))
            return jnp.sum(acc, axis=0, keepdims=True)

        def key_index(shape, k0):
            return k0 + lax.broadcasted_iota(jnp.int32, shape, 0)

        need = kf - count(lambda blk, k0: blk > thr)

        def index_pass(p, lo):
            cand = lo + jnp.left_shift(jnp.int32(1), 10 - p)
            n_eq = count(lambda blk, k0: (blk == thr) & (key_index(blk.shape, k0) < cand))
            return jnp.where(n_eq < need, cand, lo)

        lo = lax.fori_loop(0, 11, index_pass, jnp.zeros((1, LANES), jnp.int32))

        def rewrite(jj, carry):
            write_mask(pl.multiple_of(jj * KCHUNK, KCHUNK), KCHUNK,
                       lambda blk, sk: (blk > thr) | ((blk == thr) & (sk <= lo)))
            return carry

        lax.fori_loop(0, n_chunks, rewrite, 0)

    q_scale = (A_HEAD_DIM ** -0.5) * LOG2E
    for h in range(A_HEADS):
        sl = slice(h * Q_BLOCK, (h + 1) * Q_BLOCK)
        q2_ref[sl, :] = (aq_ref[:, sl] * q_scale).astype(BF16)
    m_ref[...] = jnp.full_like(m_ref, NEG_INF)
    l_ref[...] = jnp.zeros_like(l_ref)
    acc_ref[...] = jnp.zeros_like(acc_ref)

    heads_per_att = 4

    def att_chunk(jj, carry, near):
        k0 = pl.multiple_of(jj * KCHUNK, KCHUNK)
        kc = akb_ref[pl.ds(k0, KCHUNK), :]
        vc = vaug_ref[pl.ds(k0, KCHUNK), :]
        sel = jnp.concatenate([madd_ref[pl.ds(k0, Q_BLOCK), :].T,
                               madd_ref[pl.ds(k0 + Q_BLOCK, Q_BLOCK), :].T], axis=1)
        for g in range(A_HEADS // heads_per_att):
            rows = slice(g * heads_per_att * Q_BLOCK, (g + 1) * heads_per_att * Q_BLOCK)
            lg = lax.dot_general(q2_ref[rows, :], kc, _NT, preferred_element_type=F32)
            ps = []
            alphas = []
            for hh in range(heads_per_att):
                h = g * heads_per_att + hh
                hs = slice(h * Q_BLOCK, (h + 1) * Q_BLOCK)
                s = lg[hh * Q_BLOCK:(hh + 1) * Q_BLOCK, :] + sel
                if near:
                    s = s + bias_ref[i - 2 * jj, h]
                m_old = m_ref[hs, :]
                m_new = jnp.maximum(m_old, jnp.max(s, axis=1, keepdims=True))
                alphas.append(jnp.exp2(m_old - m_new))
                m_ref[hs, :] = m_new
                p = jnp.exp2(s - jnp.concatenate([m_new, m_new], axis=1))
                ps.append(p.astype(BF16))
            pv = jnp.dot(jnp.concatenate(ps, axis=0), vc, preferred_element_type=F32)
            for hh in range(heads_per_att):
                h = g * heads_per_att + hh
                hs = slice(h * Q_BLOCK, (h + 1) * Q_BLOCK)
                pv_h = pv[hh * Q_BLOCK:(hh + 1) * Q_BLOCK, :]
                acc_ref[hs, :] = alphas[hh] * acc_ref[hs, :] + pv_h[:, :A_HEAD_DIM]
                l_ref[hs, :] = alphas[hh] * l_ref[hs, :] + pv_h[:, A_HEAD_DIM:]
        return carry

    first_near = jnp.maximum(i - N_BIAS_TILES + 2, 0) // 2
    lax.fori_loop(0, first_near, functools.partial(att_chunk, near=False), 0)
    lax.fori_loop(first_near, n_chunks, functools.partial(att_chunk, near=True), 0)

    for h in range(A_HEADS):
        hs = slice(h * Q_BLOCK, (h + 1) * Q_BLOCK)
        o_ref[:, hs] = (acc_ref[hs, :] / l_ref[hs, :]).astype(BF16)


def _dsa(p, p_tail, rope_a, bias_tiles, batch, seq):
    t = p.shape[0]
    nb = seq // Q_BLOCK
    topk = min(TOPK_MAX, seq // 4)
    assert topk <= KCHUNK and seq % SEL_ROWS == 0 and seq <= 2048
    qrow = lambda b, i: b * nb + i
    stat = pltpu.VMEM((A_HEADS * Q_BLOCK, LANES), F32)
    return pl.pallas_call(
        functools.partial(_dsa_body, topk=topk, seq=seq),
        grid=(batch, nb),
        in_specs=[
            pl.BlockSpec((Q_BLOCK, IDX_HEADS * IDX_HEAD_DIM), lambda b, i: (qrow(b, i), COL_IQ // 4096)),
            pl.BlockSpec((Q_BLOCK, A_HEADS * A_HEAD_DIM), lambda b, i: (qrow(b, i), COL_AQ // 2048)),
            pl.BlockSpec((seq, LANES), lambda b, i: (b, COL_AK // LANES)),
            pl.BlockSpec((seq, LANES), lambda b, i: (b, COL_AV // LANES)),
            pl.BlockSpec((Q_BLOCK, LANES), lambda b, i: (qrow(b, i), COL_IK // LANES)),
            pl.BlockSpec((Q_BLOCK, LANES), lambda b, i: (qrow(b, i), COL_MISC // LANES)),
            pl.BlockSpec((3, Q_BLOCK, LANES), lambda b, i: (0, qrow(b, i), 0)),
            pl.BlockSpec(bias_tiles.shape, lambda b, i: (0, 0, 0, 0)),
        ],
        out_specs=pl.BlockSpec((Q_BLOCK, A_HEADS * A_HEAD_DIM), lambda b, i: (qrow(b, i), 0)),
        out_shape=jax.ShapeDtypeStruct((t, A_HEADS * A_HEAD_DIM), BF16),
        scratch_shapes=[
            pltpu.VMEM((seq, IDX_HEAD_DIM), BF16),
            pltpu.VMEM((seq, A_HEAD_DIM), BF16),
            pltpu.VMEM((seq, 2 * LANES), BF16),
            pltpu.VMEM((IDX_HEADS * Q_BLOCK, IDX_HEAD_DIM), BF16),
            pltpu.VMEM((IDX_HEADS, Q_BLOCK, LANES), F32),
            pltpu.VMEM((seq, Q_BLOCK), F32),
            pltpu.VMEM((seq, Q_BLOCK), F32),
            pltpu.VMEM((A_HEADS * Q_BLOCK, A_HEAD_DIM), BF16),
            stat, stat, stat,
            pltpu.VMEM((SUBLANES, LANES), F32),
        ],
        compiler_params=_cparams(2),
        name="dsa_attention",
    )(p, p, p, p, p, p_tail, rope_a, bias_tiles)


MLA_BLOCK = 512
MLA_HEADS_PER_STEP = 4


def _mla_body(q_ref, kv_ref, misc_ref, ropeq_ref, ropek_ref, o_ref, krr_ref, kc_ref, vaug_ref):
    hp = pl.program_id(1)
    qi = pl.program_id(2)
    head_w = QK_NOPE_DIM + LANES
    seq = kv_ref.shape[0]

    @pl.when((hp == 0) & (qi == 0))
    def _():
        krr_ref[...] = _rope_tile(misc_ref[...], (ropek_ref[0], ropek_ref[1]),
                                  ROLLS_SPREAD).astype(BF16)

    @pl.when(qi == 0)
    def _():
        for e in range(MLA_HEADS_PER_STEP):
            c0 = e * head_w
            kc_ref[e, :, :QK_NOPE_DIM] = kv_ref[:, c0:c0 + QK_NOPE_DIM]
            kc_ref[e, :, QK_NOPE_DIM:] = krr_ref[...]
            vaug_ref[e, :, :V_HEAD_DIM] = kv_ref[:, c0 + QK_NOPE_DIM:c0 + head_w]
            vaug_ref[e, :, V_HEAD_DIM:] = jnp.ones((seq, LANES), BF16)

    scale = (QK_NOPE_DIM + QK_ROPE_DIM) ** -0.5 * LOG2E
    qcs = []
    for e in range(MLA_HEADS_PER_STEP):
        c0 = e * head_w
        q_rope = _rope_tile(q_ref[:, c0 + QK_NOPE_DIM:c0 + head_w].astype(F32),
                            (ropeq_ref[0], ropeq_ref[1]), ROLLS_SPREAD)
        qc = jnp.concatenate([q_ref[:, c0:c0 + QK_NOPE_DIM].astype(F32), q_rope], axis=1) * scale
        qcs.append(qc.astype(BF16))
    lane_tiles = MLA_BLOCK // LANES

    def step(j, carry, masked):
        k0 = pl.multiple_of(j * MLA_BLOCK, MLA_BLOCK)
        out = []
        for e in range(MLA_HEADS_PER_STEP):
            m_old, l_old, acc = carry[e]
            s = lax.dot_general(qcs[e], kc_ref[e, pl.ds(k0, MLA_BLOCK), :], _NT,
                                preferred_element_type=F32)
            if masked:
                tq = lax.broadcasted_iota(jnp.int32, s.shape, 0)
                sk = lax.broadcasted_iota(jnp.int32, s.shape, 1)
                s = jnp.where(sk <= tq, s, NEG_INF)
            m_new = jnp.maximum(m_old, jnp.max(s, axis=1, keepdims=True))
            alpha = jnp.exp2(m_old - m_new)
            p = jnp.exp2(s - jnp.concatenate([m_new] * lane_tiles, axis=1))
            pv = jnp.dot(p.astype(BF16), vaug_ref[e, pl.ds(k0, MLA_BLOCK), :],
                         preferred_element_type=F32)
            out.append((m_new, alpha * l_old + pv[:, V_HEAD_DIM:],
                        alpha * acc + pv[:, :V_HEAD_DIM]))
        return tuple(out)

    init = (jnp.full((MLA_BLOCK, LANES), NEG_INF, F32),
            jnp.zeros((MLA_BLOCK, LANES), F32),
            jnp.zeros((MLA_BLOCK, V_HEAD_DIM), F32))
    carry = lax.fori_loop(0, qi, functools.partial(step, masked=False),
                          (init,) * MLA_HEADS_PER_STEP)
    carry = step(qi, carry, masked=True)
    for e in range(MLA_HEADS_PER_STEP):
        _, l_fin, acc = carry[e]
        o_ref[:, e * V_HEAD_DIM:(e + 1) * V_HEAD_DIM] = (acc / l_fin).astype(BF16)


def _mla(q, kv, p, rope_b, batch, seq):
    t = q.shape[0]
    nq = seq // MLA_BLOCK
    head_w = QK_NOPE_DIM + LANES
    step_w = MLA_HEADS_PER_STEP * head_w
    qrow = lambda b, hp, qi: b * nq + qi
    return pl.pallas_call(
        _mla_body,
        grid=(batch, B_HEADS // MLA_HEADS_PER_STEP, nq),
        in_specs=[
            pl.BlockSpec((MLA_BLOCK, step_w), lambda b, hp, qi: (qrow(b, hp, qi), hp)),
            pl.BlockSpec((seq, step_w), lambda b, hp, qi: (b, hp)),
            pl.BlockSpec((seq, LANES), lambda b, hp, qi: (b, COL_MISC // LANES)),
            pl.BlockSpec((2, MLA_BLOCK, LANES), lambda b, hp, qi: (0, qrow(b, hp, qi), 0)),
            pl.BlockSpec((2, seq, LANES), lambda b, hp, qi: (0, b, 0)),
        ],
        out_specs=pl.BlockSpec((MLA_BLOCK, MLA_HEADS_PER_STEP * V_HEAD_DIM),
                               lambda b, hp, qi: (qrow(b, hp, qi), hp)),
        out_shape=jax.ShapeDtypeStruct((t, B_HEADS * V_HEAD_DIM), BF16),
        scratch_shapes=[
            pltpu.VMEM((seq, LANES), BF16),
            pltpu.VMEM((MLA_HEADS_PER_STEP, seq, head_w), BF16),
            pltpu.VMEM((MLA_HEADS_PER_STEP, seq, head_w), BF16)],
        compiler_params=_cparams(3),
        name="mla_attention",
    )(q, kv, p, rope_b, rope_b)


def _rope_tables(positions):
    freqs = ROPE_THETA ** (-jnp.arange(ROPE_HALF, dtype=F32) / ROPE_HALF)
    ang = positions.astype(F32).reshape(-1, 1) * freqs
    cos, sin = jnp.cos(ang), jnp.sin(ang)
    one, zero = jnp.ones_like(cos), jnp.zeros_like(cos)
    cat = lambda parts: jnp.concatenate(parts, axis=1)
    rope_a = jnp.stack([cat([one, one, cos, cos]), cat([zero, zero, -sin, zero]),
                        cat([zero, zero, zero, sin])])
    rope_b = jnp.stack([cat([cos, zero, cos, zero]), cat([-sin, zero, sin, zero])])
    return rope_a, rope_b


IN_SPLITS = (A_HEADS * A_HEAD_DIM, A_HEAD_DIM, A_HEAD_DIM, IDX_HEADS * IDX_HEAD_DIM,
             IDX_HEAD_DIM, IDX_HEADS, Q_LORA_RANK, KV_LORA_RANK, QK_ROPE_DIM)
IN_OFFS = [int(v) for v in np.concatenate([[0], np.cumsum(IN_SPLITS)])]


def _in_proj_tail_weight(wt):
    iw, ql, kvl, kr = [wt[IN_OFFS[n]:IN_OFFS[n + 1]] for n in range(5, 9)]
    g = QK_ROPE_DIM // 2
    pad = jnp.zeros((TAIL_WIDTH - Q_LORA_RANK - KV_LORA_RANK - QK_ROPE_DIM - IDX_HEADS,
                     wt.shape[1]), wt.dtype)
    out = jnp.concatenate([ql, kvl, kr[:g], iw, kr[g:], pad], axis=0)
    assert out.shape[0] == TAIL_WIDTH and MISC_IW == g
    assert (COL_QL, COL_KVL, COL_MISC) == (0, Q_LORA_RANK, Q_LORA_RANK + KV_LORA_RANK)
    return out


def _in_proj_step_plan():
    aq0, ak0, iq0, ik0 = (IN_OFFS[n] // IN_TILE for n in (0, 1, 3, 4))
    assert all(IN_OFFS[n] % IN_TILE == 0 for n in (0, 1, 3, 4))
    pairs = lambda t0, n: [(t0 + 2 * k, t0 + 2 * k + 1) for k in range(n // IN_STEP)]
    plan = pairs(iq0, IN_SPLITS[3]) + pairs(aq0, IN_SPLITS[0]) + [(ak0, ik0)]
    assert len(plan) * IN_STEP == MAIN_WIDTH
    as_i32 = lambda v: jnp.asarray(np.asarray(v, np.int32))
    return as_i32([p[0] for p in plan]), as_i32([p[1] for p in plan])


def _in_proj_body(ia_ref, ib_ref, a_ref, wa_ref, wb_ref, o_ref):
    a = a_ref[...]
    o_ref[:, :IN_TILE] = lax.dot_general(a, wa_ref[...].astype(BF16), _NT,
                                         preferred_element_type=F32)
    o_ref[:, IN_TILE:] = lax.dot_general(a, wb_ref[...].astype(BF16), _NT,
                                         preferred_element_type=F32)


def _in_proj_main(u, wt):
    t, d = u.shape
    tm = 1024
    idx_a, idx_b = _in_proj_step_plan()
    grid_spec = pltpu.PrefetchScalarGridSpec(
        num_scalar_prefetch=2,
        grid=(t // tm, MAIN_WIDTH // IN_STEP),
        in_specs=[pl.BlockSpec((tm, d), lambda i, j, ia, ib: (i, 0)),
                  pl.BlockSpec((IN_TILE, d), lambda i, j, ia, ib: (ia[j], 0)),
                  pl.BlockSpec((IN_TILE, d), lambda i, j, ia, ib: (ib[j], 0))],
        out_specs=pl.BlockSpec((tm, IN_STEP), lambda i, j, ia, ib: (i, j)),
    )
    return pl.pallas_call(
        _in_proj_body,
        grid_spec=grid_spec,
        out_shape=jax.ShapeDtypeStruct((t, MAIN_WIDTH), F32),
        compiler_params=_cparams(2),
        name="in_proj",
    )(idx_a, idx_b, u, wt, wt)


def _mm_nt_body(a_ref, w_ref, o_ref):
    o_ref[...] = lax.dot_general(a_ref[...], w_ref[...].astype(BF16), _NT,
                                 preferred_element_type=F32).astype(o_ref.dtype)


def _matmul_nt(a, wt, out_dtype, tm, tn, name):
    m, k = a.shape
    n = wt.shape[0]
    return pl.pallas_call(
        _mm_nt_body,
        grid=(m // tm, n // tn),
        in_specs=[pl.BlockSpec((tm, k), lambda i, j: (i, 0)),
                  pl.BlockSpec((tn, k), lambda i, j: (j, 0))],
        out_specs=pl.BlockSpec((tm, tn), lambda i, j: (i, j)),
        out_shape=jax.ShapeDtypeStruct((m, n), out_dtype),
        compiler_params=_cparams(2),
        name=name,
    )(a, wt)


def _reorder_w_uq(w):
    r = w.shape[0]
    g = QK_ROPE_DIM // 2
    w3 = w.reshape(r, B_HEADS, QK_NOPE_DIM + QK_ROPE_DIM)
    zero = jnp.zeros((r, B_HEADS, g), w.dtype)
    w3 = jnp.concatenate([w3[:, :, :QK_NOPE_DIM], w3[:, :, QK_NOPE_DIM:QK_NOPE_DIM + g], zero,
                          w3[:, :, QK_NOPE_DIM + g:], zero], axis=2)
    return w3.reshape(r, B_HEADS * 2 * LANES)


def kernel(x, c, positions, w_ada, b_ada, w_in, rel_bias, q_norm_g, w_uq, kv_norm_g, w_ukv,
           w_o, ln1_g, ln1_b, w_gate, w_up, conv_w, conv_b, w_down, ln2_g, ln2_b):
    batch, seq, d = x.shape
    depth = w_ada.shape[0]
    t = batch * seq
    assert d == D_MODEL and batch <= SUBLANES and seq % 1024 == 0
    alpha = (2 * depth) ** 0.25

    rope_a, rope_b = _rope_tables(positions)
    bias_tiles = _bias_tiles(rel_bias)
    c8 = jnp.zeros((SUBLANES, d), F32).at[:batch].set(c)
    x2 = x.reshape(t, d)

    for l in range(depth):
        mod = _ada(c8, w_ada[l], b_ada[l])
        mod3 = mod[:batch].reshape(batch * 6, 1, d)

        u = _modulate(x2, mod3, seq, shift_idx=0, scale_idx=1)
        w_in_t = jnp.swapaxes(w_in[l], 0, 1)
        p = _in_proj_main(u, w_in_t)
        p_tail = _matmul_nt(u, _in_proj_tail_weight(w_in_t), F32, 1024, 512, "in_proj_tail")
        y_a = _dsa(p, p_tail, rope_a, bias_tiles, batch, seq)
        q = _rmsnorm_matmul(p_tail, COL_QL, q_norm_g[l], _reorder_w_uq(w_uq[l]),
                            1024, 2048, "q_up_proj")
        kv = _rmsnorm_matmul(p_tail, COL_KVL, kv_norm_g[l], w_ukv[l], 1024, 4096, "kv_up_proj")
        y_b = _mla(q, kv, p_tail, rope_b, batch, seq)
        mix = _matmul_concat(y_a, y_b, w_o[l], 1024, 512, "out_proj")
        x2, u = _resln(x2, mix, mod3, seq, 2, ln1_g[l], ln1_b[l], alpha, mod_idx=(3, 4))

        hidden = _ffn_gate_up(u, w_gate[l], w_up[l], conv_w[l], conv_b[l], seq)
        y = _ffn_down(hidden, w_down[l])
        x2 = _resln(x2, y, mod3, seq, 5, ln2_g[l], ln2_b[l], alpha)

    return x2.reshape(batch, seq, d)
```

```python
import functools
import math

import jax
import jax.numpy as jnp
import numpy as np
from jax import lax
from jax.experimental import pallas as pl
from jax.experimental.pallas import tpu as pltpu

F32 = jnp.float32
BF16 = jnp.bfloat16

D_MODEL = 4096
A_HEAD_DIM = 128
A_HEADS = 16
IDX_HEADS = 32
IDX_HEAD_DIM = 128
IDX_ROPE_DIM = 64
TOPK_MAX = 256
V_HEAD_DIM = 128
B_HEADS = 16
Q_LORA_RANK = 1024
KV_LORA_RANK = 512
QK_NOPE_DIM = 128
QK_ROPE_DIM = 64
D_FF = 11008
CONV_WIDTH = 3
REL_BUCKETS = 32
REL_MAX_DIST = 128
ROPE_THETA = 10000.0
Q_BLOCK = 128
LN_EPS = 1e-5
RMS_EPS = 1e-6
NEG_INF = -1e30

LANES = 128
SUBLANES = 8
VMEM_LIMIT_BYTES = 56 * 1024 * 1024

IN_TILE = 256
IN_STEP = 2 * IN_TILE
COL_IQ = 0
COL_AQ = 4096
COL_AK = 6144
COL_AV = 6272
COL_IK = 6400
MAIN_WIDTH = 6656
COL_QL = 0
COL_KVL = 1024
COL_MISC = 1536
MISC_IW = 32
TAIL_WIDTH = 2048

_NT = (((1,), (1,)), ((), ()))


def _cparams(n_axes):
    return pltpu.CompilerParams(
        dimension_semantics=("arbitrary",) * n_axes,
        vmem_limit_bytes=VMEM_LIMIT_BYTES)


def _ada_body(c_ref, w_ref, b_ref, o_ref):
    c = c_ref[...]
    act = (c / (1.0 + jnp.exp(-c))).astype(BF16)
    o_ref[...] = jnp.dot(act, w_ref[...].astype(BF16),
                         preferred_element_type=F32) + b_ref[...]


def _ada(c8, w, b):
    d, n = w.shape
    tn = 512
    return pl.pallas_call(
        _ada_body,
        grid=(n // tn,),
        in_specs=[pl.BlockSpec((SUBLANES, d), lambda j: (0, 0)),
                  pl.BlockSpec((d, tn), lambda j: (0, j)),
                  pl.BlockSpec((1, tn), lambda j: (0, j))],
        out_specs=pl.BlockSpec((SUBLANES, tn), lambda j: (0, j)),
        out_shape=jax.ShapeDtypeStruct((SUBLANES, n), F32),
        compiler_params=_cparams(1),
        name="ada_proj",
    )(c8, w, b.reshape(1, n))


def _modulate_body(x_ref, sh_ref, sc_ref, o_ref):
    o_ref[...] = (x_ref[...] * (1.0 + sc_ref[0]) + sh_ref[0]).astype(BF16)


def _modulate(x2, mod3, seq, shift_idx, scale_idx):
    t, d = x2.shape
    tm = 256
    per_seq = seq // tm
    return pl.pallas_call(
        _modulate_body,
        grid=(t // tm,),
        in_specs=[pl.BlockSpec((tm, d), lambda i: (i, 0)),
                  pl.BlockSpec((1, 1, d), lambda i: (6 * (i // per_seq) + shift_idx, 0, 0)),
                  pl.BlockSpec((1, 1, d), lambda i: (6 * (i // per_seq) + scale_idx, 0, 0))],
        out_specs=pl.BlockSpec((tm, d), lambda i: (i, 0)),
        out_shape=jax.ShapeDtypeStruct((t, d), BF16),
        compiler_params=_cparams(1),
        name="modulate",
    )(x2, mod3, mod3)


def _mm_body(a_ref, w_ref, o_ref):
    o_ref[...] = jnp.dot(a_ref[...], w_ref[...].astype(BF16),
                         preferred_element_type=F32).astype(o_ref.dtype)


def _matmul(a, w, out_dtype, tm, tn, name):
    m, k = a.shape
    n = w.shape[1]
    return pl.pallas_call(
        _mm_body,
        grid=(m // tm, n // tn),
        in_specs=[pl.BlockSpec((tm, k), lambda i, j: (i, 0)),
                  pl.BlockSpec((k, tn), lambda i, j: (0, j))],
        out_specs=pl.BlockSpec((tm, tn), lambda i, j: (i, j)),
        out_shape=jax.ShapeDtypeStruct((m, n), out_dtype),
        compiler_params=_cparams(2),
        name=name,
    )(a, w)


def _mm2_body(a0_ref, a1_ref, w0_ref, w1_ref, o_ref):
    acc = jnp.dot(a0_ref[...], w0_ref[...].astype(BF16), preferred_element_type=F32)
    acc = acc + jnp.dot(a1_ref[...], w1_ref[...].astype(BF16), preferred_element_type=F32)
    o_ref[...] = acc.astype(o_ref.dtype)


def _matmul_concat(a0, a1, w, out_dtype, tm, tn, name):
    m, k0 = a0.shape
    k1 = a1.shape[1]
    assert k0 == k1
    n = w.shape[1]
    return pl.pallas_call(
        _mm2_body,
        grid=(m // tm, n // tn),
        in_specs=[pl.BlockSpec((tm, k0), lambda i, j: (i, 0)),
                  pl.BlockSpec((tm, k1), lambda i, j: (i, 0)),
                  pl.BlockSpec((k0, tn), lambda i, j: (0, j)),
                  pl.BlockSpec((k1, tn), lambda i, j: (1, j))],
        out_specs=pl.BlockSpec((tm, tn), lambda i, j: (i, j)),
        out_shape=jax.ShapeDtypeStruct((m, n), out_dtype),
        compiler_params=_cparams(2),
        name=name,
    )(a0, a1, w, w)


def _rms_mm_body(x_ref, g_ref, w_ref, o_ref):
    x = x_ref[...]
    ms = jnp.mean(x * x, axis=-1, keepdims=True)
    a = (x * lax.rsqrt(ms + RMS_EPS) * g_ref[...]).astype(BF16)
    o_ref[...] = jnp.dot(a, w_ref[...].astype(BF16),
                         preferred_element_type=F32).astype(o_ref.dtype)


def _rmsnorm_matmul(p, col, gain, w, tm, tn, name):
    t = p.shape[0]
    width, n = w.shape
    cb = col // width
    assert col % width == 0
    return pl.pallas_call(
        _rms_mm_body,
        grid=(t // tm, n // tn),
        in_specs=[pl.BlockSpec((tm, width), lambda i, j: (i, cb)),
                  pl.BlockSpec((1, width), lambda i, j: (0, 0)),
                  pl.BlockSpec((width, tn), lambda i, j: (0, j))],
        out_specs=pl.BlockSpec((tm, tn), lambda i, j: (i, j)),
        out_shape=jax.ShapeDtypeStruct((t, n), BF16),
        compiler_params=_cparams(2),
        name=name,
    )(p, gain.reshape(1, width), w)


def _resln_body(x_ref, y_ref, gate_ref, lg_ref, lb_ref, *rest, alpha, with_mod):
    z = alpha * x_ref[...] + (1.0 + gate_ref[0]) * y_ref[...].astype(F32)
    mu = jnp.mean(z, axis=-1, keepdims=True)
    zc = z - mu
    var = jnp.mean(zc * zc, axis=-1, keepdims=True)
    out = zc * lax.rsqrt(var + LN_EPS) * lg_ref[...] + lb_ref[...]
    if with_mod:
        sh_ref, sc_ref, o_ref, u_ref = rest
        u_ref[...] = (out * (1.0 + sc_ref[0]) + sh_ref[0]).astype(BF16)
    else:
        (o_ref,) = rest
    o_ref[...] = out


def _resln(x2, y, mod3, seq, gate_idx, ln_g, ln_b, alpha, mod_idx=None):
    t, d = x2.shape
    tm = 256
    per_seq = seq // tm
    with_mod = mod_idx is not None
    row = pl.BlockSpec((tm, d), lambda i: (i, 0))
    vec = pl.BlockSpec((1, d), lambda i: (0, 0))

    def mod_spec(idx):
        return pl.BlockSpec((1, 1, d), lambda i: (6 * (i // per_seq) + idx, 0, 0))

    in_specs = [row, row, mod_spec(gate_idx), vec, vec]
    args = [x2, y, mod3, ln_g.reshape(1, d), ln_b.reshape(1, d)]
    out_specs = row
    out_shape = jax.ShapeDtypeStruct((t, d), F32)
    if with_mod:
        in_specs += [mod_spec(mod_idx[0]), mod_spec(mod_idx[1])]
        args += [mod3, mod3]
        out_specs = [row, row]
        out_shape = [out_shape, jax.ShapeDtypeStruct((t, d), BF16)]
    return pl.pallas_call(
        functools.partial(_resln_body, alpha=alpha, with_mod=with_mod),
        grid=(t // tm,),
        in_specs=in_specs,
        out_specs=out_specs,
        out_shape=out_shape,
        compiler_params=_cparams(1),
        name="residual_ln",
    )(*args)


def _ffn_gu_body(a_ref, halo_ref, wg_ref, wu_ref, cw_ref, cb_ref, o_ref, *, tiles_per_seq):
    i = pl.program_id(0)
    a = a_ref[...]
    wg = wg_ref[...].astype(BF16)
    wu = wu_ref[...].astype(BF16)
    g = jnp.dot(a, wg, preferred_element_type=F32)
    up = jnp.dot(a, wu, preferred_element_type=F32)
    gh = jnp.dot(halo_ref[...], wg, preferred_element_type=F32)
    gh = jnp.where(i % tiles_per_seq == 0, 0.0, gh)
    prev1 = gh[SUBLANES - 1:SUBLANES, :]
    prev2 = gh[SUBLANES - 2:SUBLANES - 1, :]
    r1 = pltpu.roll(g, 1, 0)
    r2 = pltpu.roll(g, 2, 0)
    row = lax.broadcasted_iota(jnp.int32, (SUBLANES, g.shape[1]), 0)
    head1 = jnp.where(row == 0, prev1, r1[:SUBLANES])
    head2 = jnp.where(row == 0, prev2, jnp.where(row == 1, prev1, r2[:SUBLANES]))
    g1 = jnp.concatenate([head1, r1[SUBLANES:]], axis=0)
    g2 = jnp.concatenate([head2, r2[SUBLANES:]], axis=0)
    cw = cw_ref[...]
    conv = cb_ref[...] + cw[0:1, :] * g2 + cw[1:2, :] * g1 + cw[2:3, :] * g
    o_ref[...] = (conv / (1.0 + jnp.exp(-conv)) * up).astype(BF16)


def _ffn_gate_up(u, w_gate, w_up, conv_w, conv_b, seq):
    t, d = u.shape
    f = w_gate.shape[1]
    tm, tn = 1024, 256
    halo_blocks = tm // SUBLANES
    return pl.pallas_call(
        functools.partial(_ffn_gu_body, tiles_per_seq=seq // tm),
        grid=(t // tm, f // tn),
        in_specs=[pl.BlockSpec((tm, d), lambda i, j: (i, 0)),
                  pl.BlockSpec((SUBLANES, d), lambda i, j: (jnp.maximum(i * halo_blocks - 1, 0), 0)),
                  pl.BlockSpec((d, tn), lambda i, j: (0, j)),
                  pl.BlockSpec((d, tn), lambda i, j: (0, j)),
                  pl.BlockSpec((CONV_WIDTH, tn), lambda i, j: (0, j)),
                  pl.BlockSpec((1, tn), lambda i, j: (0, j))],
        out_specs=pl.BlockSpec((tm, tn), lambda i, j: (i, j)),
        out_shape=jax.ShapeDtypeStruct((t, f), BF16),
        compiler_params=_cparams(2),
        name="ffn_gate_up",
    )(u, u, w_gate, w_up, conv_w, conv_b.reshape(1, f))


def _ffn_down_body(am_ref, at_ref, wm_ref, wt_ref, o_ref, acc_ref, *, n_main, n_tail):
    k = pl.program_id(2)

    @pl.when(k == 0)
    def _():
        acc_ref[...] = jnp.zeros_like(acc_ref)

    @pl.when(k < n_main)
    def _():
        acc_ref[...] += jnp.dot(am_ref[...], wm_ref[...].astype(BF16),
                                preferred_element_type=F32)

    @pl.when(k >= n_main)
    def _():
        acc_ref[...] += jnp.dot(at_ref[...], wt_ref[...].astype(BF16),
                                preferred_element_type=F32)

    @pl.when(k == n_main + n_tail - 1)
    def _():
        o_ref[...] = acc_ref[...].astype(o_ref.dtype)


def _ffn_down(h, w_down):
    t, f = h.shape
    d = w_down.shape[1]
    tm, tn, tk, tk_tail = 2048, 1024, 1024, 256
    n_main = f // tk
    n_tail = (f - n_main * tk) // tk_tail
    assert n_main * tk + n_tail * tk_tail == f and n_tail > 0
    tail0 = n_main * tk // tk_tail

    def main_k(k):
        return jnp.minimum(k, n_main - 1)

    def tail_k(k):
        return tail0 + jnp.maximum(k - n_main, 0)

    return pl.pallas_call(
        functools.partial(_ffn_down_body, n_main=n_main, n_tail=n_tail),
        grid=(t // tm, d // tn, n_main + n_tail),
        in_specs=[pl.BlockSpec((tm, tk), lambda i, j, k: (i, main_k(k))),
                  pl.BlockSpec((tm, tk_tail), lambda i, j, k: (i, tail_k(k))),
                  pl.BlockSpec((tk, tn), lambda i, j, k: (main_k(k), j)),
                  pl.BlockSpec((tk_tail, tn), lambda i, j, k: (tail_k(k), j))],
        out_specs=pl.BlockSpec((tm, tn), lambda i, j, k: (i, j)),
        out_shape=jax.ShapeDtypeStruct((t, d), BF16),
        scratch_shapes=[pltpu.VMEM((tm, tn), F32)],
        compiler_params=_cparams(3),
        name="ffn_down",
    )(h, h, w_down, w_down)


ROPE_HALF = IDX_ROPE_DIM // 2
ROLLS_SPLIT = (LANES - ROPE_HALF, ROPE_HALF)
ROLLS_SPREAD = (LANES // 2,)


def _rope_tile(x, tabs, shifts):
    out = x * tabs[0]
    for n, shift in enumerate(shifts):
        out = out + pltpu.roll(x, shift, 1) * tabs[1 + n]
    return out


KCHUNK = 2 * Q_BLOCK
N_BIAS_TILES = 3
LOG2E = math.log2(math.e)


def _bias_tiles_body(rb_ref, o_ref):
    o = pl.program_id(0)
    tq = lax.broadcasted_iota(jnp.int32, (Q_BLOCK, KCHUNK), 0)
    sk = lax.broadcasted_iota(jnp.int32, (Q_BLOCK, KCHUNK), 1)
    n = jnp.maximum(o * Q_BLOCK + tq - sk, 0)
    max_exact = REL_BUCKETS // 2
    nf = jnp.maximum(n, 1).astype(F32)
    large = max_exact + (jnp.log(nf / max_exact) / math.log(REL_MAX_DIST / max_exact)
                         * (REL_BUCKETS - max_exact)).astype(jnp.int32)
    large = jnp.minimum(large, REL_BUCKETS - 1)
    bucket = jnp.where(n < max_exact, n, large)
    hits = [bucket == b for b in range(REL_BUCKETS - 1)]
    for h in range(A_HEADS):
        far = rb_ref[REL_BUCKETS - 1, h]
        tile = jnp.zeros((Q_BLOCK, KCHUNK), F32)
        for b in range(REL_BUCKETS - 1):
            tile = jnp.where(hits[b], (rb_ref[b, h] - far) * LOG2E, tile)
        o_ref[0, h] = tile


def _bias_tiles(rel_bias):
    return pl.pallas_call(
        _bias_tiles_body,
        grid=(N_BIAS_TILES,),
        in_specs=[pl.BlockSpec(memory_space=pltpu.SMEM)],
        out_specs=pl.BlockSpec((1, A_HEADS, Q_BLOCK, KCHUNK), lambda o: (o, 0, 0, 0)),
        out_shape=jax.ShapeDtypeStruct((N_BIAS_TILES, A_HEADS, Q_BLOCK, KCHUNK), F32),
        compiler_params=_cparams(1),
        name="rel_bias_tiles",
    )(rel_bias)


SEL_ROWS = 512
SEL_SLAB = 64


def _dsa_body(iq_ref, aq_ref, ak_ref, av_ref, ik_ref, misc_ref, rope_ref, bias_ref,
              o_ref,
              ikr_ref, akb_ref, vaug_ref, iq2_ref, wb_ref, st_ref, madd_ref, q2_ref,
              m_ref, l_ref, acc_ref, thr_ref,
              *, topk, seq):
    i = pl.program_id(1)
    n_chunks = i // 2 + 1
    t0 = i * Q_BLOCK
    rope = (rope_ref[0], rope_ref[1], rope_ref[2])

    @pl.when(i == 0)
    def _():
        ikr_ref[...] = jnp.zeros_like(ikr_ref)
        akb_ref[...] = ak_ref[...].astype(BF16)
        vaug_ref[:, :A_HEAD_DIM] = av_ref[...].astype(BF16)
        vaug_ref[:, A_HEAD_DIM:] = jnp.ones((seq, LANES), BF16)

    ikr_ref[pl.ds(pl.multiple_of(t0, Q_BLOCK), Q_BLOCK), :] = _rope_tile(
        ik_ref[...].astype(F32), rope, ROLLS_SPLIT).astype(BF16)

    for h in range(IDX_HEADS):
        sl = slice(h * Q_BLOCK, (h + 1) * Q_BLOCK)
        iq2_ref[sl, :] = _rope_tile(iq_ref[:, sl].astype(F32), rope, ROLLS_SPLIT).astype(BF16)

    w_scale = (IDX_HEADS ** -0.5) * (IDX_HEAD_DIM ** -0.5)
    for h in range(IDX_HEADS):
        col = misc_ref[:, MISC_IW + h:MISC_IW + h + 1] * w_scale
        wb_ref[h] = jnp.broadcast_to(col, (Q_BLOCK, LANES))

    heads_per_dot = 4

    def score_chunk(jj, carry):
        k0 = pl.multiple_of(jj * KCHUNK, KCHUNK)
        kc = ikr_ref[pl.ds(k0, KCHUNK), :]
        sc = jnp.zeros((Q_BLOCK, KCHUNK), F32)
        for g in range(IDX_HEADS // heads_per_dot):
            rows = slice(g * heads_per_dot * Q_BLOCK, (g + 1) * heads_per_dot * Q_BLOCK)
            d = lax.dot_general(iq2_ref[rows, :], kc, _NT, preferred_element_type=F32)
            for hh in range(heads_per_dot):
                w = wb_ref[g * heads_per_dot + hh]
                dh = jnp.maximum(d[hh * Q_BLOCK:(hh + 1) * Q_BLOCK, :], 0.0)
                sc = sc + dh * jnp.concatenate([w, w], axis=1)
        tq = t0 + lax.broadcasted_iota(jnp.int32, sc.shape, 0)
        sk = k0 + lax.broadcasted_iota(jnp.int32, sc.shape, 1)
        sc = jnp.where(sk <= tq, sc, NEG_INF)
        st_ref[pl.ds(k0, Q_BLOCK), :] = sc[:, :Q_BLOCK].T
        st_ref[pl.ds(k0 + Q_BLOCK, Q_BLOCK), :] = sc[:, Q_BLOCK:].T
        return carry

    lax.fori_loop(0, n_chunks, score_chunk, 0)

    @pl.when(n_chunks % (SEL_ROWS // KCHUNK) == 1)
    def _():
        pad0 = pl.multiple_of(n_chunks * KCHUNK, KCHUNK)
        st_ref[pl.ds(pad0, KCHUNK), :] = jnp.full((KCHUNK, Q_BLOCK), NEG_INF, F32)

    def key_to_f32(key):
        bits = key ^ ((key >> 31) & jnp.int32(0x7FFFFFFF))
        return lax.bitcast_convert_type(bits, F32)

    kf = float(topk)

    def write_mask(k0, rows, keep_fn):
        blk = st_ref[pl.ds(k0, rows), :]
        sk = k0 + lax.broadcasted_iota(jnp.int32, blk.shape, 0)
        tq = t0 + lax.broadcasted_iota(jnp.int32, blk.shape, 1)
        keep = keep_fn(blk, sk) & (sk <= tq)
        madd_ref[pl.ds(k0, rows), :] = jnp.where(keep, 0.0, NEG_INF)

    def select_static(rows):
        def count_ge(thr_row):
            thr_b = jnp.broadcast_to(thr_row, (SEL_SLAB, LANES))
            acc = jnp.zeros((SEL_SLAB, LANES), F32)
            for r in range(rows // SEL_SLAB):
                blk = st_ref[r * SEL_SLAB:(r + 1) * SEL_SLAB, :]
                acc = jnp.where(blk >= thr_b, acc + 1.0, acc)
            acc = jnp.sum(acc.reshape(SEL_SLAB // SUBLANES, SUBLANES, LANES), axis=0)
            return jnp.sum(acc, axis=0, keepdims=True)

        def select_pass(p, key):
            cand = key + jnp.left_shift(jnp.int32(1), 31 - p)
            return jnp.where(count_ge(key_to_f32(cand)) >= kf, cand, key)

        key0 = jnp.full((1, LANES), jnp.iinfo(jnp.int32).min, jnp.int32)
        thr = key_to_f32(lax.fori_loop(0, 32, select_pass, key0))
        thr_ref[0:1, :] = thr
        thr_ref[1:2, :] = count_ge(thr)
        write_mask(0, rows, lambda blk, sk: blk >= thr)

    for n in range(1, seq // SEL_ROWS + 1):
        pl.when(i // (SEL_ROWS // Q_BLOCK) + 1 == n)(
            functools.partial(select_static, n * SEL_ROWS))

    thr = thr_ref[0:1, :]
    tie = jnp.where((thr_ref[1:2, :] > kf) & (thr > NEG_INF), 1.0, 0.0)

    @pl.when(jnp.max(tie) > 0.0)
    def _():
        def count(pred):
            def body(jj, acc):
                k0 = pl.multiple_of(jj * KCHUNK, KCHUNK)
                hit = jnp.where(pred(st_ref[pl.ds(k0, KCHUNK), :], k0), 1.0, 0.0)
                return acc + jnp.sum(hit.reshape(KCHUNK // SUBLANES, SUBLANES, LANES), axis=0)
            acc = lax.fori_loop(0, n_chunks, body, jnp.zeros((SUBLANES, LANES), F32))
            return jnp.sum(acc, axis=0, keepdims=True)

        def key_index(shape, k0):
            return k0 + lax.broadcasted_iota(jnp.int32, shape, 0)

        need = kf - count(lambda blk, k0: blk > thr)

        def index_pass(p, lo):
            cand = lo + jnp.left_shift(jnp.int32(1), 10 - p)
            n_eq = count(lambda blk, k0: (blk == thr) & (key_index(blk.shape, k0) < cand))
            return jnp.where(n_eq < need, cand, lo)

        lo = lax.fori_loop(0, 11, index_pass, jnp.zeros((1, LANES), jnp.int32))

        def rewrite(jj, carry):
            write_mask(pl.multiple_of(jj * KCHUNK, KCHUNK), KCHUNK,
                       lambda blk, sk: (blk > thr) | ((blk == thr) & (sk <= lo)))
            return carry

        lax.fori_loop(0, n_chunks, rewrite, 0)

    q_scale = (A_HEAD_DIM ** -0.5) * LOG2E
    for h in range(A_HEADS):
        sl = slice(h * Q_BLOCK, (h + 1) * Q_BLOCK)
        q2_ref[sl, :] = (aq_ref[:, sl].astype(F32) * q_scale).astype(BF16)
    m_ref[...] = jnp.full_like(m_ref, NEG_INF)
    l_ref[...] = jnp.zeros_like(l_ref)
    acc_ref[...] = jnp.zeros_like(acc_ref)

    heads_per_att = 4

    def att_chunk(jj, carry, near):
        k0 = pl.multiple_of(jj * KCHUNK, KCHUNK)
        kc = akb_ref[pl.ds(k0, KCHUNK), :]
        vc = vaug_ref[pl.ds(k0, KCHUNK), :]
        sel = jnp.concatenate([madd_ref[pl.ds(k0, Q_BLOCK), :].T,
                               madd_ref[pl.ds(k0 + Q_BLOCK, Q_BLOCK), :].T], axis=1)
        for g in range(A_HEADS // heads_per_att):
            rows = slice(g * heads_per_att * Q_BLOCK, (g + 1) * heads_per_att * Q_BLOCK)
            lg = lax.dot_general(q2_ref[rows, :], kc, _NT, preferred_element_type=F32)
            ps = []
            alphas = []
            for hh in range(heads_per_att):
                h = g * heads_per_att + hh
                hs = slice(h * Q_BLOCK, (h + 1) * Q_BLOCK)
                s = lg[hh * Q_BLOCK:(hh + 1) * Q_BLOCK, :] + sel
                if near:
                    s = s + bias_ref[i - 2 * jj, h]
                m_old = m_ref[hs, :]
                m_new = jnp.maximum(m_old, jnp.max(s, axis=1, keepdims=True))
                alphas.append(jnp.exp2(m_old - m_new))
                m_ref[hs, :] = m_new
                p = jnp.exp2(s - jnp.concatenate([m_new, m_new], axis=1))
                ps.append(p.astype(BF16))
            pv = jnp.dot(jnp.concatenate(ps, axis=0), vc, preferred_element_type=F32)
            for hh in range(heads_per_att):
                h = g * heads_per_att + hh
                hs = slice(h * Q_BLOCK, (h + 1) * Q_BLOCK)
                pv_h = pv[hh * Q_BLOCK:(hh + 1) * Q_BLOCK, :]
                acc_ref[hs, :] = alphas[hh] * acc_ref[hs, :] + pv_h[:, :A_HEAD_DIM]
                l_ref[hs, :] = alphas[hh] * l_ref[hs, :] + pv_h[:, A_HEAD_DIM:]
        return carry

    first_near = jnp.maximum(i - N_BIAS_TILES + 2, 0) // 2
    lax.fori_loop(0, first_near, functools.partial(att_chunk, near=False), 0)
    lax.fori_loop(first_near, n_chunks, functools.partial(att_chunk, near=True), 0)

    for h in range(A_HEADS):
        hs = slice(h * Q_BLOCK, (h + 1) * Q_BLOCK)
        o_ref[:, hs] = (acc_ref[hs, :] / l_ref[hs, :]).astype(BF16)


def _dsa(p, p_tail, rope_a, bias_tiles, batch, seq):
    t = p.shape[0]
    nb = seq // Q_BLOCK
    topk = min(TOPK_MAX, seq // 4)
    assert topk <= KCHUNK and seq % SEL_ROWS == 0 and seq <= 2048
    qrow = lambda b, i: b * nb + i
    stat = pltpu.VMEM((A_HEADS * Q_BLOCK, LANES), F32)
    return pl.pallas_call(
        functools.partial(_dsa_body, topk=topk, seq=seq),
        grid=(batch, nb),
        in_specs=[
            pl.BlockSpec((Q_BLOCK, IDX_HEADS * IDX_HEAD_DIM), lambda b, i: (qrow(b, i), COL_IQ // 4096)),
            pl.BlockSpec((Q_BLOCK, A_HEADS * A_HEAD_DIM), lambda b, i: (qrow(b, i), COL_AQ // 2048)),
            pl.BlockSpec((seq, LANES), lambda b, i: (b, COL_AK // LANES)),
            pl.BlockSpec((seq, LANES), lambda b, i: (b, COL_AV // LANES)),
            pl.BlockSpec((Q_BLOCK, LANES), lambda b, i: (qrow(b, i), COL_IK // LANES)),
            pl.BlockSpec((Q_BLOCK, LANES), lambda b, i: (qrow(b, i), COL_MISC // LANES)),
            pl.BlockSpec((3, Q_BLOCK, LANES), lambda b, i: (0, qrow(b, i), 0)),
            pl.BlockSpec(bias_tiles.shape, lambda b, i: (0, 0, 0, 0)),
        ],
        out_specs=pl.BlockSpec((Q_BLOCK, A_HEADS * A_HEAD_DIM), lambda b, i: (qrow(b, i), 0)),
        out_shape=jax.ShapeDtypeStruct((t, A_HEADS * A_HEAD_DIM), BF16),
        scratch_shapes=[
            pltpu.VMEM((seq, IDX_HEAD_DIM), BF16),
            pltpu.VMEM((seq, A_HEAD_DIM), BF16),
            pltpu.VMEM((seq, 2 * LANES), BF16),
            pltpu.VMEM((IDX_HEADS * Q_BLOCK, IDX_HEAD_DIM), BF16),
            pltpu.VMEM((IDX_HEADS, Q_BLOCK, LANES), F32),
            pltpu.VMEM((seq, Q_BLOCK), F32),
            pltpu.VMEM((seq, Q_BLOCK), F32),
            pltpu.VMEM((A_HEADS * Q_BLOCK, A_HEAD_DIM), BF16),
            stat, stat, stat,
            pltpu.VMEM((SUBLANES, LANES), F32),
        ],
        compiler_params=_cparams(2),
        name="dsa_attention",
    )(p, p, p, p, p, p_tail, rope_a, bias_tiles)


MLA_BLOCK = 512
MLA_HEADS_PER_STEP = 4


def _mla_body(q_ref, kv_ref, misc_ref, ropeq_ref, ropek_ref, o_ref, krr_ref, kc_ref, vaug_ref):
    hp = pl.program_id(1)
    qi = pl.program_id(2)
    head_w = QK_NOPE_DIM + LANES
    seq = kv_ref.shape[0]

    @pl.when((hp == 0) & (qi == 0))
    def _():
        krr_ref[...] = _rope_tile(misc_ref[...], (ropek_ref[0], ropek_ref[1]),
                                  ROLLS_SPREAD).astype(BF16)

    @pl.when(qi == 0)
    def _():
        for e in range(MLA_HEADS_PER_STEP):
            c0 = e * head_w
            kc_ref[e, :, :QK_NOPE_DIM] = kv_ref[:, c0:c0 + QK_NOPE_DIM]
            kc_ref[e, :, QK_NOPE_DIM:] = krr_ref[...]
            vaug_ref[e, :, :V_HEAD_DIM] = kv_ref[:, c0 + QK_NOPE_DIM:c0 + head_w]
            vaug_ref[e, :, V_HEAD_DIM:] = jnp.ones((seq, LANES), BF16)

    scale = (QK_NOPE_DIM + QK_ROPE_DIM) ** -0.5 * LOG2E
    qcs = []
    for e in range(MLA_HEADS_PER_STEP):
        c0 = e * head_w
        q_rope = _rope_tile(q_ref[:, c0 + QK_NOPE_DIM:c0 + head_w].astype(F32),
                            (ropeq_ref[0], ropeq_ref[1]), ROLLS_SPREAD)
        qc = jnp.concatenate([q_ref[:, c0:c0 + QK_NOPE_DIM].astype(F32), q_rope], axis=1) * scale
        qcs.append(qc.astype(BF16))
    lane_tiles = MLA_BLOCK // LANES

    def step(j, carry, masked):
        k0 = pl.multiple_of(j * MLA_BLOCK, MLA_BLOCK)
        out = []
        for e in range(MLA_HEADS_PER_STEP):
            m_old, l_old, acc = carry[e]
            s = lax.dot_general(qcs[e], kc_ref[e, pl.ds(k0, MLA_BLOCK), :], _NT,
                                preferred_element_type=F32)
            if masked:
                tq = lax.broadcasted_iota(jnp.int32, s.shape, 0)
                sk = lax.broadcasted_iota(jnp.int32, s.shape, 1)
                s = jnp.where(sk <= tq, s, NEG_INF)
            m_new = jnp.maximum(m_old, jnp.max(s, axis=1, keepdims=True))
            alpha = jnp.exp2(m_old - m_new)
            p = jnp.exp2(s - jnp.concatenate([m_new] * lane_tiles, axis=1))
            pv = jnp.dot(p.astype(BF16), vaug_ref[e, pl.ds(k0, MLA_BLOCK), :],
                         preferred_element_type=F32)
            out.append((m_new, alpha * l_old + pv[:, V_HEAD_DIM:],
                        alpha * acc + pv[:, :V_HEAD_DIM]))
        return tuple(out)

    init = (jnp.full((MLA_BLOCK, LANES), NEG_INF, F32),
            jnp.zeros((MLA_BLOCK, LANES), F32),
            jnp.zeros((MLA_BLOCK, V_HEAD_DIM), F32))
    carry = lax.fori_loop(0, qi, functools.partial(step, masked=False),
                          (init,) * MLA_HEADS_PER_STEP)
    carry = step(qi, carry, masked=True)
    for e in range(MLA_HEADS_PER_STEP):
        _, l_fin, acc = carry[e]
        o_ref[:, e * V_HEAD_DIM:(e + 1) * V_HEAD_DIM] = (acc / l_fin).astype(BF16)


def _mla(q, kv, p, rope_b, batch, seq):
    t = q.shape[0]
    nq = seq // MLA_BLOCK
    head_w = QK_NOPE_DIM + LANES
    step_w = MLA_HEADS_PER_STEP * head_w
    qrow = lambda b, hp, qi: b * nq + qi
    return pl.pallas_call(
        _mla_body,
        grid=(batch, B_HEADS // MLA_HEADS_PER_STEP, nq),
        in_specs=[
            pl.BlockSpec((MLA_BLOCK, step_w), lambda b, hp, qi: (qrow(b, hp, qi), hp)),
            pl.BlockSpec((seq, step_w), lambda b, hp, qi: (b, hp)),
            pl.BlockSpec((seq, LANES), lambda b, hp, qi: (b, COL_MISC // LANES)),
            pl.BlockSpec((2, MLA_BLOCK, LANES), lambda b, hp, qi: (0, qrow(b, hp, qi), 0)),
            pl.BlockSpec((2, seq, LANES), lambda b, hp, qi: (0, b, 0)),
        ],
        out_specs=pl.BlockSpec((MLA_BLOCK, MLA_HEADS_PER_STEP * V_HEAD_DIM),
                               lambda b, hp, qi: (qrow(b, hp, qi), hp)),
        out_shape=jax.ShapeDtypeStruct((t, B_HEADS * V_HEAD_DIM), BF16),
        scratch_shapes=[
            pltpu.VMEM((seq, LANES), BF16),
            pltpu.VMEM((MLA_HEADS_PER_STEP, seq, head_w), BF16),
            pltpu.VMEM((MLA_HEADS_PER_STEP, seq, head_w), BF16)],
        compiler_params=_cparams(3),
        name="mla_attention",
    )(q, kv, p, rope_b, rope_b)


def _rope_tables(positions):
    freqs = ROPE_THETA ** (-jnp.arange(ROPE_HALF, dtype=F32) / ROPE_HALF)
    ang = positions.astype(F32).reshape(-1, 1) * freqs
    cos, sin = jnp.cos(ang), jnp.sin(ang)
    one, zero = jnp.ones_like(cos), jnp.zeros_like(cos)
    cat = lambda parts: jnp.concatenate(parts, axis=1)
    rope_a = jnp.stack([cat([one, one, cos, cos]), cat([zero, zero, -sin, zero]),
                        cat([zero, zero, zero, sin])])
    rope_b = jnp.stack([cat([cos, zero, cos, zero]), cat([-sin, zero, sin, zero])])
    return rope_a, rope_b


IN_SPLITS = (A_HEADS * A_HEAD_DIM, A_HEAD_DIM, A_HEAD_DIM, IDX_HEADS * IDX_HEAD_DIM,
             IDX_HEAD_DIM, IDX_HEADS, Q_LORA_RANK, KV_LORA_RANK, QK_ROPE_DIM)
IN_OFFS = [int(v) for v in np.concatenate([[0], np.cumsum(IN_SPLITS)])]


def _in_proj_tail_weight(wt):
    iw, ql, kvl, kr = [wt[IN_OFFS[n]:IN_OFFS[n + 1]] for n in range(5, 9)]
    g = QK_ROPE_DIM // 2
    pad = jnp.zeros((TAIL_WIDTH - Q_LORA_RANK - KV_LORA_RANK - QK_ROPE_DIM - IDX_HEADS,
                     wt.shape[1]), wt.dtype)
    out = jnp.concatenate([ql, kvl, kr[:g], iw, kr[g:], pad], axis=0)
    assert out.shape[0] == TAIL_WIDTH and MISC_IW == g
    assert (COL_QL, COL_KVL, COL_MISC) == (0, Q_LORA_RANK, Q_LORA_RANK + KV_LORA_RANK)
    return out


def _in_proj_step_plan():
    aq0, ak0, iq0, ik0 = (IN_OFFS[n] // IN_TILE for n in (0, 1, 3, 4))
    assert all(IN_OFFS[n] % IN_TILE == 0 for n in (0, 1, 3, 4))
    pairs = lambda t0, n: [(t0 + 2 * k, t0 + 2 * k + 1) for k in range(n // IN_STEP)]
    plan = pairs(iq0, IN_SPLITS[3]) + pairs(aq0, IN_SPLITS[0]) + [(ak0, ik0)]
    assert len(plan) * IN_STEP == MAIN_WIDTH
    as_i32 = lambda v: jnp.asarray(np.asarray(v, np.int32))
    return as_i32([p[0] for p in plan]), as_i32([p[1] for p in plan])


def _in_proj_body(ia_ref, ib_ref, a_ref, wa_ref, wb_ref, o_ref):
    a = a_ref[...]
    o_ref[:, :IN_TILE] = lax.dot_general(a, wa_ref[...].astype(BF16), _NT,
                                         preferred_element_type=F32).astype(o_ref.dtype)
    o_ref[:, IN_TILE:] = lax.dot_general(a, wb_ref[...].astype(BF16), _NT,
                                         preferred_element_type=F32).astype(o_ref.dtype)


def _in_proj_main(u, wt):
    t, d = u.shape
    tm = 1024
    idx_a, idx_b = _in_proj_step_plan()
    grid_spec = pltpu.PrefetchScalarGridSpec(
        num_scalar_prefetch=2,
        grid=(t // tm, MAIN_WIDTH // IN_STEP),
        in_specs=[pl.BlockSpec((tm, d), lambda i, j, ia, ib: (i, 0)),
                  pl.BlockSpec((IN_TILE, d), lambda i, j, ia, ib: (ia[j], 0)),
                  pl.BlockSpec((IN_TILE, d), lambda i, j, ia, ib: (ib[j], 0))],
        out_specs=pl.BlockSpec((tm, IN_STEP), lambda i, j, ia, ib: (i, j)),
    )
    return pl.pallas_call(
        _in_proj_body,
        grid_spec=grid_spec,
        out_shape=jax.ShapeDtypeStruct((t, MAIN_WIDTH), BF16),
        compiler_params=_cparams(2),
        name="in_proj",
    )(idx_a, idx_b, u, wt, wt)


def _mm_nt_body(a_ref, w_ref, o_ref):
    o_ref[...] = lax.dot_general(a_ref[...], w_ref[...].astype(BF16), _NT,
                                 preferred_element_type=F32).astype(o_ref.dtype)


def _matmul_nt(a, wt, out_dtype, tm, tn, name):
    m, k = a.shape
    n = wt.shape[0]
    return pl.pallas_call(
        _mm_nt_body,
        grid=(m // tm, n // tn),
        in_specs=[pl.BlockSpec((tm, k), lambda i, j: (i, 0)),
                  pl.BlockSpec((tn, k), lambda i, j: (j, 0))],
        out_specs=pl.BlockSpec((tm, tn), lambda i, j: (i, j)),
        out_shape=jax.ShapeDtypeStruct((m, n), out_dtype),
        compiler_params=_cparams(2),
        name=name,
    )(a, wt)


def _reorder_w_uq(w):
    r = w.shape[0]
    g = QK_ROPE_DIM // 2
    w3 = w.reshape(r, B_HEADS, QK_NOPE_DIM + QK_ROPE_DIM)
    zero = jnp.zeros((r, B_HEADS, g), w.dtype)
    w3 = jnp.concatenate([w3[:, :, :QK_NOPE_DIM], w3[:, :, QK_NOPE_DIM:QK_NOPE_DIM + g], zero,
                          w3[:, :, QK_NOPE_DIM + g:], zero], axis=2)
    return w3.reshape(r, B_HEADS * 2 * LANES)


def kernel(x, c, positions, w_ada, b_ada, w_in, rel_bias, q_norm_g, w_uq, kv_norm_g, w_ukv,
           w_o, ln1_g, ln1_b, w_gate, w_up, conv_w, conv_b, w_down, ln2_g, ln2_b):
    batch, seq, d = x.shape
    depth = w_ada.shape[0]
    t = batch * seq
    assert d == D_MODEL and batch <= SUBLANES and seq % 1024 == 0
    alpha = (2 * depth) ** 0.25

    rope_a, rope_b = _rope_tables(positions)
    bias_tiles = _bias_tiles(rel_bias)
    c8 = jnp.zeros((SUBLANES, d), F32).at[:batch].set(c)
    x2 = x.reshape(t, d)

    for l in range(depth):
        mod = _ada(c8, w_ada[l], b_ada[l])
        mod3 = mod[:batch].reshape(batch * 6, 1, d)

        u = _modulate(x2, mod3, seq, shift_idx=0, scale_idx=1)
        w_in_t = jnp.swapaxes(w_in[l], 0, 1)
        p = _in_proj_main(u, w_in_t)
        p_tail = _matmul_nt(u, _in_proj_tail_weight(w_in_t), F32, 1024, 512, "in_proj_tail")
        y_a = _dsa(p, p_tail, rope_a, bias_tiles, batch, seq)
        q = _rmsnorm_matmul(p_tail, COL_QL, q_norm_g[l], _reorder_w_uq(w_uq[l]),
                            1024, 2048, "q_up_proj")
        kv = _rmsnorm_matmul(p_tail, COL_KVL, kv_norm_g[l], w_ukv[l], 1024, 4096, "kv_up_proj")
        y_b = _mla(q, kv, p_tail, rope_b, batch, seq)
        mix = _matmul_concat(y_a, y_b, w_o[l], BF16, 1024, 512, "out_proj")
        x2, u = _resln(x2, mix, mod3, seq, 2, ln1_g[l], ln1_b[l], alpha, mod_idx=(3, 4))

        hidden = _ffn_gate_up(u, w_gate[l], w_up[l], conv_w[l], conv_b[l], seq)
        y = _ffn_down(hidden, w_down[l])
        x2 = _resln(x2, y, mod3, seq, 5, ln2_g[l], ln2_b[l], alpha)

    return x2.reshape(batch, seq, d)
```

```python
import functools
import math

import jax
import jax.numpy as jnp
import numpy as np
from jax import lax
from jax.experimental import pallas as pl
from jax.experimental.pallas import tpu as pltpu

F32 = jnp.float32
BF16 = jnp.bfloat16

D_MODEL = 4096
A_HEAD_DIM = 128
A_HEADS = 16
IDX_HEADS = 32
IDX_HEAD_DIM = 128
IDX_ROPE_DIM = 64
TOPK_MAX = 256
V_HEAD_DIM = 128
B_HEADS = 16
Q_LORA_RANK = 1024
KV_LORA_RANK = 512
QK_NOPE_DIM = 128
QK_ROPE_DIM = 64
CONV_WIDTH = 3
REL_BUCKETS = 32
REL_MAX_DIST = 128
ROPE_THETA = 10000.0
Q_BLOCK = 128
LN_EPS = 1e-5
RMS_EPS = 1e-6
NEG_INF = -1e30

LANES = 128
SUBLANES = 8
VMEM_LIMIT_BYTES = 56 * 1024 * 1024

IN_TILE = 256
IN_STEP = 2 * IN_TILE
COL_IQ = 0
COL_AQ = 4096
COL_AK = 6144
COL_AV = 6272
COL_IK = 6400
MAIN_WIDTH = 6656
COL_QL = 0
COL_KVL = 1024
COL_MISC = 1536
MISC_IW = 32
TAIL_WIDTH = 2048

_NT = (((1,), (1,)), ((), ()))


def _cparams(n_axes):
    return pltpu.CompilerParams(
        dimension_semantics=("arbitrary",) * n_axes,
        vmem_limit_bytes=VMEM_LIMIT_BYTES)


def _ada_body(c_ref, w_ref, b_ref, o_ref):
    c = c_ref[...]
    act = (c / (1.0 + jnp.exp(-c))).astype(BF16)
    o_ref[...] = jnp.dot(act, w_ref[...].astype(BF16),
                         preferred_element_type=F32) + b_ref[...]


def _ada(c8, w, b):
    d, n = w.shape
    tn = 512
    return pl.pallas_call(
        _ada_body,
        grid=(n // tn,),
        in_specs=[pl.BlockSpec((SUBLANES, d), lambda j: (0, 0)),
                  pl.BlockSpec((d, tn), lambda j: (0, j)),
                  pl.BlockSpec((1, tn), lambda j: (0, j))],
        out_specs=pl.BlockSpec((SUBLANES, tn), lambda j: (0, j)),
        out_shape=jax.ShapeDtypeStruct((SUBLANES, n), F32),
        compiler_params=_cparams(1),
        name="ada_proj",
    )(c8, w, b.reshape(1, n))


def _modulate_body(x_ref, sh_ref, sc_ref, o_ref):
    o_ref[...] = (x_ref[...] * (1.0 + sc_ref[0]) + sh_ref[0]).astype(BF16)


def _modulate(x2, mod3, seq, shift_idx, scale_idx):
    t, d = x2.shape
    tm = 256
    per_seq = seq // tm
    return pl.pallas_call(
        _modulate_body,
        grid=(t // tm,),
        in_specs=[pl.BlockSpec((tm, d), lambda i: (i, 0)),
                  pl.BlockSpec((1, 1, d), lambda i: (6 * (i // per_seq) + shift_idx, 0, 0)),
                  pl.BlockSpec((1, 1, d), lambda i: (6 * (i // per_seq) + scale_idx, 0, 0))],
        out_specs=pl.BlockSpec((tm, d), lambda i: (i, 0)),
        out_shape=jax.ShapeDtypeStruct((t, d), BF16),
        compiler_params=_cparams(1),
        name="modulate",
    )(x2, mod3, mod3)


def _mm2_body(a0_ref, a1_ref, w0_ref, w1_ref, o_ref):
    acc = jnp.dot(a0_ref[...], w0_ref[...].astype(BF16), preferred_element_type=F32)
    acc = acc + jnp.dot(a1_ref[...], w1_ref[...].astype(BF16), preferred_element_type=F32)
    o_ref[...] = acc.astype(o_ref.dtype)


def _matmul_concat(a0, a1, w, out_dtype, tm, tn, name):
    m, k0 = a0.shape
    k1 = a1.shape[1]
    assert k0 == k1
    n = w.shape[1]
    return pl.pallas_call(
        _mm2_body,
        grid=(m // tm, n // tn),
        in_specs=[pl.BlockSpec((tm, k0), lambda i, j: (i, 0)),
                  pl.BlockSpec((tm, k1), lambda i, j: (i, 0)),
                  pl.BlockSpec((k0, tn), lambda i, j: (0, j)),
                  pl.BlockSpec((k1, tn), lambda i, j: (1, j))],
        out_specs=pl.BlockSpec((tm, tn), lambda i, j: (i, j)),
        out_shape=jax.ShapeDtypeStruct((m, n), out_dtype),
        compiler_params=_cparams(2),
        name=name,
    )(a0, a1, w, w)


def _rms_mm_body(x_ref, g_ref, w_ref, o_ref):
    x = x_ref[...]
    ms = jnp.mean(x * x, axis=-1, keepdims=True)
    a = (x * lax.rsqrt(ms + RMS_EPS) * g_ref[...]).astype(BF16)
    o_ref[...] = jnp.dot(a, w_ref[...].astype(BF16),
                         preferred_element_type=F32).astype(o_ref.dtype)


def _rmsnorm_matmul(p, col, gain, w, tm, tn, name):
    t = p.shape[0]
    width, n = w.shape
    cb = col // width
    assert col % width == 0
    return pl.pallas_call(
        _rms_mm_body,
        grid=(t // tm, n // tn),
        in_specs=[pl.BlockSpec((tm, width), lambda i, j: (i, cb)),
                  pl.BlockSpec((1, width), lambda i, j: (0, 0)),
                  pl.BlockSpec((width, tn), lambda i, j: (0, j))],
        out_specs=pl.BlockSpec((tm, tn), lambda i, j: (i, j)),
        out_shape=jax.ShapeDtypeStruct((t, n), BF16),
        compiler_params=_cparams(2),
        name=name,
    )(p, gain.reshape(1, width), w)


def _resln_body(x_ref, y_ref, gate_ref, lg_ref, lb_ref, *rest, alpha, with_mod):
    z = alpha * x_ref[...] + (1.0 + gate_ref[0]) * y_ref[...].astype(F32)
    mu = jnp.mean(z, axis=-1, keepdims=True)
    zc = z - mu
    var = jnp.mean(zc * zc, axis=-1, keepdims=True)
    out = zc * lax.rsqrt(var + LN_EPS) * lg_ref[...] + lb_ref[...]
    if with_mod:
        sh_ref, sc_ref, o_ref, u_ref = rest
        u_ref[...] = (out * (1.0 + sc_ref[0]) + sh_ref[0]).astype(BF16)
    else:
        (o_ref,) = rest
    o_ref[...] = out


def _resln(x2, y, mod3, seq, gate_idx, ln_g, ln_b, alpha, mod_idx=None):
    t, d = x2.shape
    tm = 256
    per_seq = seq // tm
    with_mod = mod_idx is not None
    row = pl.BlockSpec((tm, d), lambda i: (i, 0))
    vec = pl.BlockSpec((1, d), lambda i: (0, 0))

    def mod_spec(idx):
        return pl.BlockSpec((1, 1, d), lambda i: (6 * (i // per_seq) + idx, 0, 0))

    in_specs = [row, row, mod_spec(gate_idx), vec, vec]
    args = [x2, y, mod3, ln_g.reshape(1, d), ln_b.reshape(1, d)]
    out_specs = row
    out_shape = jax.ShapeDtypeStruct((t, d), F32)
    if with_mod:
        in_specs += [mod_spec(mod_idx[0]), mod_spec(mod_idx[1])]
        args += [mod3, mod3]
        out_specs = [row, row]
        out_shape = [out_shape, jax.ShapeDtypeStruct((t, d), BF16)]
    return pl.pallas_call(
        functools.partial(_resln_body, alpha=alpha, with_mod=with_mod),
        grid=(t // tm,),
        in_specs=in_specs,
        out_specs=out_specs,
        out_shape=out_shape,
        compiler_params=_cparams(1),
        name="residual_ln",
    )(*args)


def _ffn_gu_body(a_ref, halo_ref, wg_ref, wu_ref, cw_ref, cb_ref, o_ref, *, tiles_per_seq):
    i = pl.program_id(0)
    a = a_ref[...]
    wg = wg_ref[...].astype(BF16)
    wu = wu_ref[...].astype(BF16)
    g = jnp.dot(a, wg, preferred_element_type=F32)
    up = jnp.dot(a, wu, preferred_element_type=F32)
    gh = jnp.dot(halo_ref[...], wg, preferred_element_type=F32)
    gh = jnp.where(i % tiles_per_seq == 0, 0.0, gh)
    prev1 = gh[SUBLANES - 1:SUBLANES, :]
    prev2 = gh[SUBLANES - 2:SUBLANES - 1, :]
    r1 = pltpu.roll(g, 1, 0)
    r2 = pltpu.roll(g, 2, 0)
    row = lax.broadcasted_iota(jnp.int32, (SUBLANES, g.shape[1]), 0)
    head1 = jnp.where(row == 0, prev1, r1[:SUBLANES])
    head2 = jnp.where(row == 0, prev2, jnp.where(row == 1, prev1, r2[:SUBLANES]))
    g1 = jnp.concatenate([head1, r1[SUBLANES:]], axis=0)
    g2 = jnp.concatenate([head2, r2[SUBLANES:]], axis=0)
    cw = cw_ref[...]
    conv = cb_ref[...] + cw[0:1, :] * g2 + cw[1:2, :] * g1 + cw[2:3, :] * g
    o_ref[...] = (conv / (1.0 + jnp.exp(-conv)) * up).astype(BF16)


def _ffn_gate_up(u, w_gate, w_up, conv_w, conv_b, seq):
    t, d = u.shape
    f = w_gate.shape[1]
    tm, tn = 1024, 256
    halo_blocks = tm // SUBLANES
    return pl.pallas_call(
        functools.partial(_ffn_gu_body, tiles_per_seq=seq // tm),
        grid=(t // tm, f // tn),
        in_specs=[pl.BlockSpec((tm, d), lambda i, j: (i, 0)),
                  pl.BlockSpec((SUBLANES, d), lambda i, j: (jnp.maximum(i * halo_blocks - 1, 0), 0)),
                  pl.BlockSpec((d, tn), lambda i, j: (0, j)),
                  pl.BlockSpec((d, tn), lambda i, j: (0, j)),
                  pl.BlockSpec((CONV_WIDTH, tn), lambda i, j: (0, j)),
                  pl.BlockSpec((1, tn), lambda i, j: (0, j))],
        out_specs=pl.BlockSpec((tm, tn), lambda i, j: (i, j)),
        out_shape=jax.ShapeDtypeStruct((t, f), BF16),
        compiler_params=_cparams(2),
        name="ffn_gate_up",
    )(u, u, w_gate, w_up, conv_w, conv_b.reshape(1, f))


def _ffn_down_body(am_ref, at_ref, wm_ref, wt_ref, o_ref, acc_ref, *, n_main, n_tail):
    k = pl.program_id(2)

    @pl.when(k == 0)
    def _():
        acc_ref[...] = jnp.zeros_like(acc_ref)

    @pl.when(k < n_main)
    def _():
        acc_ref[...] += jnp.dot(am_ref[...], wm_ref[...].astype(BF16),
                                preferred_element_type=F32)

    @pl.when(k >= n_main)
    def _():
        acc_ref[...] += jnp.dot(at_ref[...], wt_ref[...].astype(BF16),
                                preferred_element_type=F32)

    @pl.when(k == n_main + n_tail - 1)
    def _():
        o_ref[...] = acc_ref[...].astype(o_ref.dtype)


def _ffn_down(h, w_down):
    t, f = h.shape
    d = w_down.shape[1]
    tm, tn, tk, tk_tail = 2048, 1024, 1024, 256
    n_main = f // tk
    n_tail = (f - n_main * tk) // tk_tail
    assert n_main * tk + n_tail * tk_tail == f and n_tail > 0
    tail0 = n_main * tk // tk_tail

    def main_k(k):
        return jnp.minimum(k, n_main - 1)

    def tail_k(k):
        return tail0 + jnp.maximum(k - n_main, 0)

    return pl.pallas_call(
        functools.partial(_ffn_down_body, n_main=n_main, n_tail=n_tail),
        grid=(t // tm, d // tn, n_main + n_tail),
        in_specs=[pl.BlockSpec((tm, tk), lambda i, j, k: (i, main_k(k))),
                  pl.BlockSpec((tm, tk_tail), lambda i, j, k: (i, tail_k(k))),
                  pl.BlockSpec((tk, tn), lambda i, j, k: (main_k(k), j)),
                  pl.BlockSpec((tk_tail, tn), lambda i, j, k: (tail_k(k), j))],
        out_specs=pl.BlockSpec((tm, tn), lambda i, j, k: (i, j)),
        out_shape=jax.ShapeDtypeStruct((t, d), BF16),
        scratch_shapes=[pltpu.VMEM((tm, tn), F32)],
        compiler_params=_cparams(3),
        name="ffn_down",
    )(h, h, w_down, w_down)


ROPE_HALF = IDX_ROPE_DIM // 2
ROLLS_SPLIT = (LANES - ROPE_HALF, ROPE_HALF)
ROLLS_SPREAD = (LANES // 2,)


def _rope_tile(x, tabs, shifts):
    out = x * tabs[0]
    for n, shift in enumerate(shifts):
        out = out + pltpu.roll(x, shift, 1) * tabs[1 + n]
    return out


KCHUNK = 2 * Q_BLOCK
N_BIAS_TILES = 3
LOG2E = math.log2(math.e)


def _bias_tiles_body(rb_ref, o_ref):
    o = pl.program_id(0)
    tq = lax.broadcasted_iota(jnp.int32, (Q_BLOCK, KCHUNK), 0)
    sk = lax.broadcasted_iota(jnp.int32, (Q_BLOCK, KCHUNK), 1)
    n = jnp.maximum(o * Q_BLOCK + tq - sk, 0)
    max_exact = REL_BUCKETS // 2
    nf = jnp.maximum(n, 1).astype(F32)
    large = max_exact + (jnp.log(nf / max_exact) / math.log(REL_MAX_DIST / max_exact)
                         * (REL_BUCKETS - max_exact)).astype(jnp.int32)
    large = jnp.minimum(large, REL_BUCKETS - 1)
    bucket = jnp.where(n < max_exact, n, large)
    hits = [bucket == b for b in range(REL_BUCKETS - 1)]
    for h in range(A_HEADS):
        far = rb_ref[REL_BUCKETS - 1, h]
        tile = jnp.zeros((Q_BLOCK, KCHUNK), F32)
        for b in range(REL_BUCKETS - 1):
            tile = jnp.where(hits[b], (rb_ref[b, h] - far) * LOG2E, tile)
        o_ref[0, h] = tile


def _bias_tiles(rel_bias):
    return pl.pallas_call(
        _bias_tiles_body,
        grid=(N_BIAS_TILES,),
        in_specs=[pl.BlockSpec(memory_space=pltpu.SMEM)],
        out_specs=pl.BlockSpec((1, A_HEADS, Q_BLOCK, KCHUNK), lambda o: (o, 0, 0, 0)),
        out_shape=jax.ShapeDtypeStruct((N_BIAS_TILES, A_HEADS, Q_BLOCK, KCHUNK), F32),
        compiler_params=_cparams(1),
        name="rel_bias_tiles",
    )(rel_bias)


SEL_ROWS = 512
SEL_SLAB = 64


def _dsa_body(iq_ref, aq_ref, ak_ref, av_ref, ik_ref, misc_ref, rope_ref, bias_ref,
              o_ref,
              ikr_ref, vaug_ref, iq2_ref, wb_ref, st_ref, madd_ref, q2_ref,
              m_ref, l_ref, acc_ref, thr_ref,
              *, topk, seq):
    i = pl.program_id(1)
    n_chunks = i // 2 + 1
    t0 = i * Q_BLOCK
    rope = (rope_ref[0], rope_ref[1], rope_ref[2])

    @pl.when(i == 0)
    def _():
        ikr_ref[...] = jnp.zeros_like(ikr_ref)
        vaug_ref[:, :A_HEAD_DIM] = av_ref[...]
        vaug_ref[:, A_HEAD_DIM:] = jnp.ones((seq, LANES), BF16)

    ikr_ref[pl.ds(pl.multiple_of(t0, Q_BLOCK), Q_BLOCK), :] = _rope_tile(
        ik_ref[...].astype(F32), rope, ROLLS_SPLIT).astype(BF16)

    for h in range(IDX_HEADS):
        sl = slice(h * Q_BLOCK, (h + 1) * Q_BLOCK)
        iq2_ref[sl, :] = _rope_tile(iq_ref[:, sl].astype(F32), rope, ROLLS_SPLIT).astype(BF16)

    w_scale = (IDX_HEADS ** -0.5) * (IDX_HEAD_DIM ** -0.5)
    for h in range(IDX_HEADS):
        col = misc_ref[:, MISC_IW + h:MISC_IW + h + 1] * w_scale
        wb_ref[h] = jnp.broadcast_to(col, (Q_BLOCK, LANES))

    heads_per_dot = 4

    def score_chunk(jj, carry):
        k0 = pl.multiple_of(jj * KCHUNK, KCHUNK)
        kc = ikr_ref[pl.ds(k0, KCHUNK), :]
        sc = jnp.zeros((Q_BLOCK, KCHUNK), F32)
        for g in range(IDX_HEADS // heads_per_dot):
            rows = slice(g * heads_per_dot * Q_BLOCK, (g + 1) * heads_per_dot * Q_BLOCK)
            d = lax.dot_general(iq2_ref[rows, :], kc, _NT, preferred_element_type=F32)
            for hh in range(heads_per_dot):
                w = wb_ref[g * heads_per_dot + hh]
                dh = jnp.maximum(d[hh * Q_BLOCK:(hh + 1) * Q_BLOCK, :], 0.0)
                sc = sc + dh * jnp.concatenate([w, w], axis=1)
        tq = t0 + lax.broadcasted_iota(jnp.int32, sc.shape, 0)
        sk = k0 + lax.broadcasted_iota(jnp.int32, sc.shape, 1)
        sc = jnp.where(sk <= tq, sc, NEG_INF)
        st_ref[pl.ds(k0, Q_BLOCK), :] = sc[:, :Q_BLOCK].T
        st_ref[pl.ds(k0 + Q_BLOCK, Q_BLOCK), :] = sc[:, Q_BLOCK:].T
        return carry

    lax.fori_loop(0, n_chunks, score_chunk, 0)

    @pl.when(n_chunks % (SEL_ROWS // KCHUNK) == 1)
    def _():
        pad0 = pl.multiple_of(n_chunks * KCHUNK, KCHUNK)
        st_ref[pl.ds(pad0, KCHUNK), :] = jnp.full((KCHUNK, Q_BLOCK), NEG_INF, F32)

    def key_to_f32(key):
        bits = key ^ ((key >> 31) & jnp.int32(0x7FFFFFFF))
        return lax.bitcast_convert_type(bits, F32)

    kf = float(topk)

    def write_mask(k0, rows, keep_fn):
        blk = st_ref[pl.ds(k0, rows), :]
        sk = k0 + lax.broadcasted_iota(jnp.int32, blk.shape, 0)
        tq = t0 + lax.broadcasted_iota(jnp.int32, blk.shape, 1)
        keep = keep_fn(blk, sk) & (sk <= tq)
        madd_ref[pl.ds(k0, rows), :] = jnp.where(keep, 0.0, NEG_INF)

    def select_static(rows):
        def count_ge(thr_row):
            thr_b = jnp.broadcast_to(thr_row, (SEL_SLAB, LANES))
            acc = jnp.zeros((SEL_SLAB, LANES), F32)
            for r in range(rows // SEL_SLAB):
                blk = st_ref[r * SEL_SLAB:(r + 1) * SEL_SLAB, :]
                acc = jnp.where(blk >= thr_b, acc + 1.0, acc)
            acc = jnp.sum(acc.reshape(SEL_SLAB // SUBLANES, SUBLANES, LANES), axis=0)
            return jnp.sum(acc, axis=0, keepdims=True)

        def select_pass(p, key):
            cand = key + jnp.left_shift(jnp.int32(1), 31 - p)
            return jnp.where(count_ge(key_to_f32(cand)) >= kf, cand, key)

        key0 = jnp.full((1, LANES), jnp.iinfo(jnp.int32).min, jnp.int32)
        thr = key_to_f32(lax.fori_loop(0, 32, select_pass, key0))
        thr_ref[0:1, :] = thr
        thr_ref[1:2, :] = count_ge(thr)
        write_mask(0, rows, lambda blk, sk: blk >= thr)

    for n in range(1, seq // SEL_ROWS + 1):
        pl.when(i // (SEL_ROWS // Q_BLOCK) + 1 == n)(
            functools.partial(select_static, n * SEL_ROWS))

    thr = thr_ref[0:1, :]
    tie = jnp.where((thr_ref[1:2, :] > kf) & (thr > NEG_INF), 1.0, 0.0)

    @pl.when(jnp.max(tie) > 0.0)
    def _():
        def count(pred):
            def body(jj, acc):
                k0 = pl.multiple_of(jj * KCHUNK, KCHUNK)
                hit = jnp.where(pred(st_ref[pl.ds(k0, KCHUNK), :], k0), 1.0, 0.0)
                return acc + jnp.sum(hit.reshape(KCHUNK // SUBLANES, SUBLANES, LANES), axis=0)
            acc = lax.fori_loop(0, n_chunks, body, jnp.zeros((SUBLANES, LANES), F32))
            return jnp.sum(acc, axis=0, keepdims=True)

        def key_index(shape, k0):
            return k0 + lax.broadcasted_iota(jnp.int32, shape, 0)

        need = kf - count(lambda blk, k0: blk > thr)

        def index_pass(p, lo):
            cand = lo + jnp.left_shift(jnp.int32(1), 10 - p)
            n_eq = count(lambda blk, k0: (blk == thr) & (key_index(blk.shape, k0) < cand))
            return jnp.where(n_eq < need, cand, lo)

        lo = lax.fori_loop(0, 11, index_pass, jnp.zeros((1, LANES), jnp.int32))

        def rewrite(jj, carry):
            write_mask(pl.multiple_of(jj * KCHUNK, KCHUNK), KCHUNK,
                       lambda blk, sk: (blk > thr) | ((blk == thr) & (sk <= lo)))
            return carry

        lax.fori_loop(0, n_chunks, rewrite, 0)

    q_scale = (A_HEAD_DIM ** -0.5) * LOG2E
    for h in range(A_HEADS):
        sl = slice(h * Q_BLOCK, (h + 1) * Q_BLOCK)
        q2_ref[sl, :] = (aq_ref[:, sl].astype(F32) * q_scale).astype(BF16)
    m_ref[...] = jnp.full_like(m_ref, NEG_INF)
    l_ref[...] = jnp.zeros_like(l_ref)
    acc_ref[...] = jnp.zeros_like(acc_ref)

    heads_per_att = 4

    def att_chunk(jj, carry, near):
        k0 = pl.multiple_of(jj * KCHUNK, KCHUNK)
        kc = ak_ref[pl.ds(k0, KCHUNK), :]
        vc = vaug_ref[pl.ds(k0, KCHUNK), :]
        sel = jnp.concatenate([madd_ref[pl.ds(k0, Q_BLOCK), :].T,
                               madd_ref[pl.ds(k0 + Q_BLOCK, Q_BLOCK), :].T], axis=1)
        for g in range(A_HEADS // heads_per_att):
            rows = slice(g * heads_per_att * Q_BLOCK, (g + 1) * heads_per_att * Q_BLOCK)
            lg = lax.dot_general(q2_ref[rows, :], kc, _NT, preferred_element_type=F32)
            ps = []
            alphas = []
            for hh in range(heads_per_att):
                h = g * heads_per_att + hh
                hs = slice(h * Q_BLOCK, (h + 1) * Q_BLOCK)
                s = lg[hh * Q_BLOCK:(hh + 1) * Q_BLOCK, :] + sel
                if near:
                    s = s + bias_ref[i - 2 * jj, h]
                m_old = m_ref[hs, :]
                m_new = jnp.maximum(m_old, jnp.max(s, axis=1, keepdims=True))
                alphas.append(jnp.exp2(m_old - m_new))
                m_ref[hs, :] = m_new
                p = jnp.exp2(s - jnp.concatenate([m_new, m_new], axis=1))
                ps.append(p.astype(BF16))
            pv = jnp.dot(jnp.concatenate(ps, axis=0), vc, preferred_element_type=F32)
            for hh in range(heads_per_att):
                h = g * heads_per_att + hh
                hs = slice(h * Q_BLOCK, (h + 1) * Q_BLOCK)
                pv_h = pv[hh * Q_BLOCK:(hh + 1) * Q_BLOCK, :]
                acc_ref[hs, :] = alphas[hh] * acc_ref[hs, :] + pv_h[:, :A_HEAD_DIM]
                l_ref[hs, :] = alphas[hh] * l_ref[hs, :] + pv_h[:, A_HEAD_DIM:]
        return carry

    first_near = jnp.maximum(i - N_BIAS_TILES + 2, 0) // 2
    lax.fori_loop(0, first_near, functools.partial(att_chunk, near=False), 0)
    lax.fori_loop(first_near, n_chunks, functools.partial(att_chunk, near=True), 0)

    for h in range(A_HEADS):
        hs = slice(h * Q_BLOCK, (h + 1) * Q_BLOCK)
        o_ref[:, hs] = (acc_ref[hs, :] / l_ref[hs, :]).astype(BF16)


def _dsa(p, p_tail, rope_a, bias_tiles, batch, seq):
    t = p.shape[0]
    nb = seq // Q_BLOCK
    topk = min(TOPK_MAX, seq // 4)
    assert topk <= KCHUNK and seq % SEL_ROWS == 0 and seq <= 2048
    qrow = lambda b, i: b * nb + i
    stat = pltpu.VMEM((A_HEADS * Q_BLOCK, LANES), F32)
    return pl.pallas_call(
        functools.partial(_dsa_body, topk=topk, seq=seq),
        grid=(batch, nb),
        in_specs=[
            pl.BlockSpec((Q_BLOCK, IDX_HEADS * IDX_HEAD_DIM), lambda b, i: (qrow(b, i), COL_IQ // 4096)),
            pl.BlockSpec((Q_BLOCK, A_HEADS * A_HEAD_DIM), lambda b, i: (qrow(b, i), COL_AQ // 2048)),
            pl.BlockSpec((seq, LANES), lambda b, i: (b, COL_AK // LANES)),
            pl.BlockSpec((seq, LANES), lambda b, i: (b, COL_AV // LANES)),
            pl.BlockSpec((Q_BLOCK, LANES), lambda b, i: (qrow(b, i), COL_IK // LANES)),
            pl.BlockSpec((Q_BLOCK, LANES), lambda b, i: (qrow(b, i), COL_MISC // LANES)),
            pl.BlockSpec((3, Q_BLOCK, LANES), lambda b, i: (0, qrow(b, i), 0)),
            pl.BlockSpec(bias_tiles.shape, lambda b, i: (0, 0, 0, 0)),
        ],
        out_specs=pl.BlockSpec((Q_BLOCK, A_HEADS * A_HEAD_DIM), lambda b, i: (qrow(b, i), 0)),
        out_shape=jax.ShapeDtypeStruct((t, A_HEADS * A_HEAD_DIM), BF16),
        scratch_shapes=[
            pltpu.VMEM((seq, IDX_HEAD_DIM), BF16),
            pltpu.VMEM((seq, 2 * LANES), BF16),
            pltpu.VMEM((IDX_HEADS * Q_BLOCK, IDX_HEAD_DIM), BF16),
            pltpu.VMEM((IDX_HEADS, Q_BLOCK, LANES), F32),
            pltpu.VMEM((seq, Q_BLOCK), F32),
            pltpu.VMEM((seq, Q_BLOCK), F32),
            pltpu.VMEM((A_HEADS * Q_BLOCK, A_HEAD_DIM), BF16),
            stat, stat, stat,
            pltpu.VMEM((SUBLANES, LANES), F32),
        ],
        compiler_params=_cparams(2),
        name="dsa_attention",
    )(p, p, p, p, p, p_tail, rope_a, bias_tiles)


MLA_BLOCK = 512
MLA_HEADS_PER_STEP = 4


def _mla_body(q_ref, kv_ref, misc_ref, ropeq_ref, ropek_ref, o_ref, krr_ref, kc_ref, vaug_ref):
    hp = pl.program_id(1)
    qi = pl.program_id(2)
    head_w = QK_NOPE_DIM + LANES
    seq = kv_ref.shape[0]

    @pl.when((hp == 0) & (qi == 0))
    def _():
        krr_ref[...] = _rope_tile(misc_ref[...], (ropek_ref[0], ropek_ref[1]),
                                  ROLLS_SPREAD).astype(BF16)

    @pl.when(qi == 0)
    def _():
        for e in range(MLA_HEADS_PER_STEP):
            c0 = e * head_w
            kc_ref[e, :, :QK_NOPE_DIM] = kv_ref[:, c0:c0 + QK_NOPE_DIM]
            kc_ref[e, :, QK_NOPE_DIM:] = krr_ref[...]
            vaug_ref[e, :, :V_HEAD_DIM] = kv_ref[:, c0 + QK_NOPE_DIM:c0 + head_w]
            vaug_ref[e, :, V_HEAD_DIM:] = jnp.ones((seq, LANES), BF16)

    scale = (QK_NOPE_DIM + QK_ROPE_DIM) ** -0.5 * LOG2E
    qcs = []
    for e in range(MLA_HEADS_PER_STEP):
        c0 = e * head_w
        q_rope = _rope_tile(q_ref[:, c0 + QK_NOPE_DIM:c0 + head_w].astype(F32),
                            (ropeq_ref[0], ropeq_ref[1]), ROLLS_SPREAD)
        qc = jnp.concatenate([q_ref[:, c0:c0 + QK_NOPE_DIM].astype(F32), q_rope], axis=1) * scale
        qcs.append(qc.astype(BF16))
    lane_tiles = MLA_BLOCK // LANES

    def step(j, carry, masked):
        k0 = pl.multiple_of(j * MLA_BLOCK, MLA_BLOCK)
        out = []
        for e in range(MLA_HEADS_PER_STEP):
            m_old, l_old, acc = carry[e]
            s = lax.dot_general(qcs[e], kc_ref[e, pl.ds(k0, MLA_BLOCK), :], _NT,
                                preferred_element_type=F32)
            if masked:
                tq = lax.broadcasted_iota(jnp.int32, s.shape, 0)
                sk = lax.broadcasted_iota(jnp.int32, s.shape, 1)
                s = jnp.where(sk <= tq, s, NEG_INF)
            m_new = jnp.maximum(m_old, jnp.max(s, axis=1, keepdims=True))
            alpha = jnp.exp2(m_old - m_new)
            p = jnp.exp2(s - jnp.concatenate([m_new] * lane_tiles, axis=1))
            pv = jnp.dot(p.astype(BF16), vaug_ref[e, pl.ds(k0, MLA_BLOCK), :],
                         preferred_element_type=F32)
            out.append((m_new, alpha * l_old + pv[:, V_HEAD_DIM:],
                        alpha * acc + pv[:, :V_HEAD_DIM]))
        return tuple(out)

    init = (jnp.full((MLA_BLOCK, LANES), NEG_INF, F32),
            jnp.zeros((MLA_BLOCK, LANES), F32),
            jnp.zeros((MLA_BLOCK, V_HEAD_DIM), F32))
    carry = lax.fori_loop(0, qi, functools.partial(step, masked=False),
                          (init,) * MLA_HEADS_PER_STEP)
    carry = step(qi, carry, masked=True)
    for e in range(MLA_HEADS_PER_STEP):
        _, l_fin, acc = carry[e]
        o_ref[:, e * V_HEAD_DIM:(e + 1) * V_HEAD_DIM] = (acc / l_fin).astype(BF16)


def _mla(q, kv, p, rope_b, batch, seq):
    t = q.shape[0]
    nq = seq // MLA_BLOCK
    head_w = QK_NOPE_DIM + LANES
    step_w = MLA_HEADS_PER_STEP * head_w
    qrow = lambda b, hp, qi: b * nq + qi
    return pl.pallas_call(
        _mla_body,
        grid=(batch, B_HEADS // MLA_HEADS_PER_STEP, nq),
        in_specs=[
            pl.BlockSpec((MLA_BLOCK, step_w), lambda b, hp, qi: (qrow(b, hp, qi), hp)),
            pl.BlockSpec((seq, step_w), lambda b, hp, qi: (b, hp)),
            pl.BlockSpec((seq, LANES), lambda b, hp, qi: (b, COL_MISC // LANES)),
            pl.BlockSpec((2, MLA_BLOCK, LANES), lambda b, hp, qi: (0, qrow(b, hp, qi), 0)),
            pl.BlockSpec((2, seq, LANES), lambda b, hp, qi: (0, b, 0)),
        ],
        out_specs=pl.BlockSpec((MLA_BLOCK, MLA_HEADS_PER_STEP * V_HEAD_DIM),
                               lambda b, hp, qi: (qrow(b, hp, qi), hp)),
        out_shape=jax.ShapeDtypeStruct((t, B_HEADS * V_HEAD_DIM), BF16),
        scratch_shapes=[
            pltpu.VMEM((seq, LANES), BF16),
            pltpu.VMEM((MLA_HEADS_PER_STEP, seq, head_w), BF16),
            pltpu.VMEM((MLA_HEADS_PER_STEP, seq, head_w), BF16)],
        compiler_params=_cparams(3),
        name="mla_attention",
    )(q, kv, p, rope_b, rope_b)


def _rope_tables(positions):
    freqs = ROPE_THETA ** (-jnp.arange(ROPE_HALF, dtype=F32) / ROPE_HALF)
    ang = positions.astype(F32).reshape(-1, 1) * freqs
    cos, sin = jnp.cos(ang), jnp.sin(ang)
    one, zero = jnp.ones_like(cos), jnp.zeros_like(cos)
    cat = lambda parts: jnp.concatenate(parts, axis=1)
    rope_a = jnp.stack([cat([one, one, cos, cos]), cat([zero, zero, -sin, zero]),
                        cat([zero, zero, zero, sin])])
    rope_b = jnp.stack([cat([cos, zero, cos, zero]), cat([-sin, zero, sin, zero])])
    return rope_a, rope_b


IN_SPLITS = (A_HEADS * A_HEAD_DIM, A_HEAD_DIM, A_HEAD_DIM, IDX_HEADS * IDX_HEAD_DIM,
             IDX_HEAD_DIM, IDX_HEADS, Q_LORA_RANK, KV_LORA_RANK, QK_ROPE_DIM)
IN_OFFS = [int(v) for v in np.concatenate([[0], np.cumsum(IN_SPLITS)])]


def _in_proj_tail_weight(wt):
    iw, ql, kvl, kr = [wt[IN_OFFS[n]:IN_OFFS[n + 1]] for n in range(5, 9)]
    g = QK_ROPE_DIM // 2
    pad = jnp.zeros((TAIL_WIDTH - Q_LORA_RANK - KV_LORA_RANK - QK_ROPE_DIM - IDX_HEADS,
                     wt.shape[1]), wt.dtype)
    out = jnp.concatenate([ql, kvl, kr[:g], iw, kr[g:], pad], axis=0)
    assert out.shape[0] == TAIL_WIDTH and MISC_IW == g
    assert (COL_QL, COL_KVL, COL_MISC) == (0, Q_LORA_RANK, Q_LORA_RANK + KV_LORA_RANK)
    return out


def _in_proj_step_plan():
    aq0, ak0, iq0, ik0 = (IN_OFFS[n] // IN_TILE for n in (0, 1, 3, 4))
    assert all(IN_OFFS[n] % IN_TILE == 0 for n in (0, 1, 3, 4))
    pairs = lambda t0, n: [(t0 + 2 * k, t0 + 2 * k + 1) for k in range(n // IN_STEP)]
    plan = pairs(iq0, IN_SPLITS[3]) + pairs(aq0, IN_SPLITS[0]) + [(ak0, ik0)]
    assert len(plan) * IN_STEP == MAIN_WIDTH
    as_i32 = lambda v: jnp.asarray(np.asarray(v, np.int32))
    return as_i32([p[0] for p in plan]), as_i32([p[1] for p in plan])


def _in_proj_body(ia_ref, ib_ref, a_ref, wa_ref, wb_ref, o_ref):
    a = a_ref[...]
    o_ref[:, :IN_TILE] = lax.dot_general(a, wa_ref[...].astype(BF16), _NT,
                                         preferred_element_type=F32).astype(o_ref.dtype)
    o_ref[:, IN_TILE:] = lax.dot_general(a, wb_ref[...].astype(BF16), _NT,
                                         preferred_element_type=F32).astype(o_ref.dtype)


def _in_proj_main(u, wt):
    t, d = u.shape
    tm = 1024
    idx_a, idx_b = _in_proj_step_plan()
    grid_spec = pltpu.PrefetchScalarGridSpec(
        num_scalar_prefetch=2,
        grid=(t // tm, MAIN_WIDTH // IN_STEP),
        in_specs=[pl.BlockSpec((tm, d), lambda i, j, ia, ib: (i, 0)),
                  pl.BlockSpec((IN_TILE, d), lambda i, j, ia, ib: (ia[j], 0)),
                  pl.BlockSpec((IN_TILE, d), lambda i, j, ia, ib: (ib[j], 0))],
        out_specs=pl.BlockSpec((tm, IN_STEP), lambda i, j, ia, ib: (i, j)),
    )
    return pl.pallas_call(
        _in_proj_body,
        grid_spec=grid_spec,
        out_shape=jax.ShapeDtypeStruct((t, MAIN_WIDTH), BF16),
        compiler_params=_cparams(2),
        name="in_proj",
    )(idx_a, idx_b, u, wt, wt)


def _mm_nt_body(a_ref, w_ref, o_ref):
    o_ref[...] = lax.dot_general(a_ref[...], w_ref[...].astype(BF16), _NT,
                                 preferred_element_type=F32).astype(o_ref.dtype)


def _matmul_nt(a, wt, out_dtype, tm, tn, name):
    m, k = a.shape
    n = wt.shape[0]
    return pl.pallas_call(
        _mm_nt_body,
        grid=(m // tm, n // tn),
        in_specs=[pl.BlockSpec((tm, k), lambda i, j: (i, 0)),
                  pl.BlockSpec((tn, k), lambda i, j: (j, 0))],
        out_specs=pl.BlockSpec((tm, tn), lambda i, j: (i, j)),
        out_shape=jax.ShapeDtypeStruct((m, n), out_dtype),
        compiler_params=_cparams(2),
        name=name,
    )(a, wt)


def _reorder_w_uq(w):
    r = w.shape[0]
    g = QK_ROPE_DIM // 2
    w3 = w.reshape(r, B_HEADS, QK_NOPE_DIM + QK_ROPE_DIM)
    zero = jnp.zeros((r, B_HEADS, g), w.dtype)
    w3 = jnp.concatenate([w3[:, :, :QK_NOPE_DIM], w3[:, :, QK_NOPE_DIM:QK_NOPE_DIM + g], zero,
                          w3[:, :, QK_NOPE_DIM + g:], zero], axis=2)
    return w3.reshape(r, B_HEADS * 2 * LANES)


def kernel(x, c, positions, w_ada, b_ada, w_in, rel_bias, q_norm_g, w_uq, kv_norm_g, w_ukv,
           w_o, ln1_g, ln1_b, w_gate, w_up, conv_w, conv_b, w_down, ln2_g, ln2_b):
    batch, seq, d = x.shape
    depth = w_ada.shape[0]
    t = batch * seq
    assert d == D_MODEL and batch <= SUBLANES and seq % 1024 == 0
    alpha = (2 * depth) ** 0.25

    rope_a, rope_b = _rope_tables(positions)
    bias_tiles = _bias_tiles(rel_bias)
    c8 = jnp.zeros((SUBLANES, d), F32).at[:batch].set(c)
    x2 = x.reshape(t, d)

    for l in range(depth):
        mod = _ada(c8, w_ada[l], b_ada[l])
        mod3 = mod[:batch].reshape(batch * 6, 1, d)

        u = _modulate(x2, mod3, seq, shift_idx=0, scale_idx=1)
        w_in_t = jnp.swapaxes(w_in[l], 0, 1)
        p = _in_proj_main(u, w_in_t)
        p_tail = _matmul_nt(u, _in_proj_tail_weight(w_in_t), F32, 1024, 512, "in_proj_tail")
        y_a = _dsa(p, p_tail, rope_a, bias_tiles, batch, seq)
        q = _rmsnorm_matmul(p_tail, COL_QL, q_norm_g[l], _reorder_w_uq(w_uq[l]),
                            1024, 2048, "q_up_proj")
        kv = _rmsnorm_matmul(p_tail, COL_KVL, kv_norm_g[l], w_ukv[l], 1024, 4096, "kv_up_proj")
        y_b = _mla(q, kv, p_tail, rope_b, batch, seq)
        mix = _matmul_concat(y_a, y_b, w_o[l], BF16, 1024, 512, "out_proj")
        x2, u = _resln(x2, mix, mod3, seq, 2, ln1_g[l], ln1_b[l], alpha, mod_idx=(3, 4))

        hidden = _ffn_gate_up(u, w_gate[l], w_up[l], conv_w[l], conv_b[l], seq)
        y = _ffn_down(hidden, w_down[l])
        x2 = _resln(x2, y, mod3, seq, 5, ln2_g[l], ln2_b[l], alpha)

    return x2.reshape(batch, seq, d)
```

```python
import functools
import math

import jax
import jax.numpy as jnp
import numpy as np
from jax import lax
from jax.experimental import pallas as pl
from jax.experimental.pallas import tpu as pltpu

F32 = jnp.float32
BF16 = jnp.bfloat16

D_MODEL = 4096
A_HEAD_DIM = 128
A_HEADS = 16
IDX_HEADS = 32
IDX_HEAD_DIM = 128
IDX_ROPE_DIM = 64
TOPK_MAX = 256
V_HEAD_DIM = 128
B_HEADS = 16
Q_LORA_RANK = 1024
KV_LORA_RANK = 512
QK_NOPE_DIM = 128
QK_ROPE_DIM = 64
CONV_WIDTH = 3
REL_BUCKETS = 32
REL_MAX_DIST = 128
ROPE_THETA = 10000.0
Q_BLOCK = 128
LN_EPS = 1e-5
RMS_EPS = 1e-6
NEG_INF = -1e30

LANES = 128
SUBLANES = 8
VMEM_LIMIT_BYTES = 56 * 1024 * 1024

IN_TILE = 256
IN_STEP = 2 * IN_TILE
COL_IQ = 0
COL_AQ = 4096
COL_AK = 6144
COL_AV = 6272
COL_IK = 6400
MAIN_WIDTH = 6656
COL_QL = 0
COL_KVL = 1024
COL_MISC = 1536
MISC_IW = 32
TAIL_WIDTH = 2048

_NT = (((1,), (1,)), ((), ()))


def _cparams(n_axes):
    return pltpu.CompilerParams(
        dimension_semantics=("arbitrary",) * n_axes,
        vmem_limit_bytes=VMEM_LIMIT_BYTES)


def _ada_body(c_ref, w_ref, b_ref, o_ref):
    c = c_ref[...]
    act = (c / (1.0 + jnp.exp(-c))).astype(BF16)
    o_ref[...] = jnp.dot(act, w_ref[...].astype(BF16),
                         preferred_element_type=F32) + b_ref[...]


def _ada(c8, w, b):
    d, n = w.shape
    tn = 512
    return pl.pallas_call(
        _ada_body,
        grid=(n // tn,),
        in_specs=[pl.BlockSpec((SUBLANES, d), lambda j: (0, 0)),
                  pl.BlockSpec((d, tn), lambda j: (0, j)),
                  pl.BlockSpec((1, tn), lambda j: (0, j))],
        out_specs=pl.BlockSpec((SUBLANES, tn), lambda j: (0, j)),
        out_shape=jax.ShapeDtypeStruct((SUBLANES, n), F32),
        compiler_params=_cparams(1),
        name="ada_proj",
    )(c8, w, b.reshape(1, n))


def _modulate_body(x_ref, sh_ref, sc_ref, o_ref):
    o_ref[...] = (x_ref[...] * (1.0 + sc_ref[0]) + sh_ref[0]).astype(BF16)


def _modulate(x2, mod3, seq, shift_idx, scale_idx):
    t, d = x2.shape
    tm = 256
    per_seq = seq // tm
    return pl.pallas_call(
        _modulate_body,
        grid=(t // tm,),
        in_specs=[pl.BlockSpec((tm, d), lambda i: (i, 0)),
                  pl.BlockSpec((1, 1, d), lambda i: (6 * (i // per_seq) + shift_idx, 0, 0)),
                  pl.BlockSpec((1, 1, d), lambda i: (6 * (i // per_seq) + scale_idx, 0, 0))],
        out_specs=pl.BlockSpec((tm, d), lambda i: (i, 0)),
        out_shape=jax.ShapeDtypeStruct((t, d), BF16),
        compiler_params=_cparams(1),
        name="modulate",
    )(x2, mod3, mod3)


def _mm2_body(a0_ref, a1_ref, w0_ref, w1_ref, o_ref):
    acc = jnp.dot(a0_ref[...], w0_ref[...].astype(BF16), preferred_element_type=F32)
    acc = acc + jnp.dot(a1_ref[...], w1_ref[...].astype(BF16), preferred_element_type=F32)
    o_ref[...] = acc.astype(o_ref.dtype)


def _matmul_concat(a0, a1, w, out_dtype, tm, tn, name):
    m, k0 = a0.shape
    k1 = a1.shape[1]
    assert k0 == k1
    n = w.shape[1]
    return pl.pallas_call(
        _mm2_body,
        grid=(m // tm, n // tn),
        in_specs=[pl.BlockSpec((tm, k0), lambda i, j: (i, 0)),
                  pl.BlockSpec((tm, k1), lambda i, j: (i, 0)),
                  pl.BlockSpec((k0, tn), lambda i, j: (0, j)),
                  pl.BlockSpec((k1, tn), lambda i, j: (1, j))],
        out_specs=pl.BlockSpec((tm, tn), lambda i, j: (i, j)),
        out_shape=jax.ShapeDtypeStruct((m, n), out_dtype),
        compiler_params=_cparams(2),
        name=name,
    )(a0, a1, w, w)


def _rms_mm_body(x_ref, g_ref, w_ref, o_ref):
    x = x_ref[...]
    ms = jnp.mean(x * x, axis=-1, keepdims=True)
    a = (x * lax.rsqrt(ms + RMS_EPS) * g_ref[...]).astype(BF16)
    o_ref[...] = jnp.dot(a, w_ref[...].astype(BF16),
                         preferred_element_type=F32).astype(o_ref.dtype)


def _rmsnorm_matmul(p, col, gain, w, tm, tn, name):
    t = p.shape[0]
    width, n = w.shape
    cb = col // width
    assert col % width == 0
    return pl.pallas_call(
        _rms_mm_body,
        grid=(t // tm, n // tn),
        in_specs=[pl.BlockSpec((tm, width), lambda i, j: (i, cb)),
                  pl.BlockSpec((1, width), lambda i, j: (0, 0)),
                  pl.BlockSpec((width, tn), lambda i, j: (0, j))],
        out_specs=pl.BlockSpec((tm, tn), lambda i, j: (i, j)),
        out_shape=jax.ShapeDtypeStruct((t, n), BF16),
        compiler_params=_cparams(2),
        name=name,
    )(p, gain.reshape(1, width), w)


def _resln_body(x_ref, y_ref, gate_ref, lg_ref, lb_ref, *rest, alpha, with_mod):
    z = alpha * x_ref[...] + (1.0 + gate_ref[0]) * y_ref[...].astype(F32)
    mu = jnp.mean(z, axis=-1, keepdims=True)
    zc = z - mu
    var = jnp.mean(zc * zc, axis=-1, keepdims=True)
    out = zc * lax.rsqrt(var + LN_EPS) * lg_ref[...] + lb_ref[...]
    if with_mod:
        sh_ref, sc_ref, o_ref, u_ref = rest
        u_ref[...] = (out * (1.0 + sc_ref[0]) + sh_ref[0]).astype(BF16)
    else:
        (o_ref,) = rest
    o_ref[...] = out


def _resln(x2, y, mod3, seq, gate_idx, ln_g, ln_b, alpha, mod_idx=None):
    t, d = x2.shape
    tm = 256
    per_seq = seq // tm
    with_mod = mod_idx is not None
    row = pl.BlockSpec((tm, d), lambda i: (i, 0))
    vec = pl.BlockSpec((1, d), lambda i: (0, 0))

    def mod_spec(idx):
        return pl.BlockSpec((1, 1, d), lambda i: (6 * (i // per_seq) + idx, 0, 0))

    in_specs = [row, row, mod_spec(gate_idx), vec, vec]
    args = [x2, y, mod3, ln_g.reshape(1, d), ln_b.reshape(1, d)]
    out_specs = row
    out_shape = jax.ShapeDtypeStruct((t, d), F32)
    if with_mod:
        in_specs += [mod_spec(mod_idx[0]), mod_spec(mod_idx[1])]
        args += [mod3, mod3]
        out_specs = [row, row]
        out_shape = [out_shape, jax.ShapeDtypeStruct((t, d), BF16)]
    return pl.pallas_call(
        functools.partial(_resln_body, alpha=alpha, with_mod=with_mod),
        grid=(t // tm,),
        in_specs=in_specs,
        out_specs=out_specs,
        out_shape=out_shape,
        compiler_params=_cparams(1),
        name="residual_ln",
    )(*args)


def _ffn_gu_body(a_ref, wg_ref, wu_ref, cw_ref, cb_ref, o_ref, gtail_ref, *, tiles_per_seq):
    i = pl.program_id(0)
    j = pl.program_id(1)

    @pl.when((i == 0) & (j == 0))
    def _():
        gtail_ref[...] = jnp.zeros_like(gtail_ref)

    a = a_ref[...]
    wg = wg_ref[...].astype(BF16)
    wu = wu_ref[...].astype(BF16)
    g = jnp.dot(a, wg, preferred_element_type=F32)
    up = jnp.dot(a, wu, preferred_element_type=F32)
    gh = jnp.where(i % tiles_per_seq == 0, 0.0, gtail_ref[j])
    gtail_ref[j] = g[g.shape[0] - SUBLANES:, :]
    prev1 = gh[SUBLANES - 1:SUBLANES, :]
    prev2 = gh[SUBLANES - 2:SUBLANES - 1, :]
    r1 = pltpu.roll(g, 1, 0)
    r2 = pltpu.roll(g, 2, 0)
    row = lax.broadcasted_iota(jnp.int32, (SUBLANES, g.shape[1]), 0)
    head1 = jnp.where(row == 0, prev1, r1[:SUBLANES])
    head2 = jnp.where(row == 0, prev2, jnp.where(row == 1, prev1, r2[:SUBLANES]))
    g1 = jnp.concatenate([head1, r1[SUBLANES:]], axis=0)
    g2 = jnp.concatenate([head2, r2[SUBLANES:]], axis=0)
    cw = cw_ref[...]
    conv = cb_ref[...] + cw[0:1, :] * g2 + cw[1:2, :] * g1 + cw[2:3, :] * g
    o_ref[...] = (conv / (1.0 + jnp.exp(-conv)) * up).astype(BF16)


def _ffn_gate_up(u, w_gate, w_up, conv_w, conv_b, seq):
    t, d = u.shape
    f = w_gate.shape[1]
    tm, tn = 1024, 256
    return pl.pallas_call(
        functools.partial(_ffn_gu_body, tiles_per_seq=seq // tm),
        grid=(t // tm, f // tn),
        in_specs=[pl.BlockSpec((tm, d), lambda i, j: (i, 0)),
                  pl.BlockSpec((d, tn), lambda i, j: (0, j)),
                  pl.BlockSpec((d, tn), lambda i, j: (0, j)),
                  pl.BlockSpec((CONV_WIDTH, tn), lambda i, j: (0, j)),
                  pl.BlockSpec((1, tn), lambda i, j: (0, j))],
        out_specs=pl.BlockSpec((tm, tn), lambda i, j: (i, j)),
        out_shape=jax.ShapeDtypeStruct((t, f), BF16),
        scratch_shapes=[pltpu.VMEM((f // tn, SUBLANES, tn), F32)],
        compiler_params=_cparams(2),
        name="ffn_gate_up",
    )(u, w_gate, w_up, conv_w, conv_b.reshape(1, f))


def _ffn_down_body(am_ref, at_ref, wm_ref, wt_ref, o_ref, acc_ref, *, n_main, n_tail):
    k = pl.program_id(2)

    @pl.when(k == 0)
    def _():
        acc_ref[...] = jnp.zeros_like(acc_ref)

    @pl.when(k < n_main)
    def _():
        acc_ref[...] += jnp.dot(am_ref[...], wm_ref[...].astype(BF16),
                                preferred_element_type=F32)

    @pl.when(k >= n_main)
    def _():
        acc_ref[...] += jnp.dot(at_ref[...], wt_ref[...].astype(BF16),
                                preferred_element_type=F32)

    @pl.when(k == n_main + n_tail - 1)
    def _():
        o_ref[...] = acc_ref[...].astype(o_ref.dtype)


def _ffn_down(h, w_down):
    t, f = h.shape
    d = w_down.shape[1]
    tm, tn, tk, tk_tail = 2048, 1024, 1024, 256
    n_main = f // tk
    n_tail = (f - n_main * tk) // tk_tail
    assert n_main * tk + n_tail * tk_tail == f and n_tail > 0
    tail0 = n_main * tk // tk_tail

    def main_k(k):
        return jnp.minimum(k, n_main - 1)

    def tail_k(k):
        return tail0 + jnp.maximum(k - n_main, 0)

    return pl.pallas_call(
        functools.partial(_ffn_down_body, n_main=n_main, n_tail=n_tail),
        grid=(t // tm, d // tn, n_main + n_tail),
        in_specs=[pl.BlockSpec((tm, tk), lambda i, j, k: (i, main_k(k))),
                  pl.BlockSpec((tm, tk_tail), lambda i, j, k: (i, tail_k(k))),
                  pl.BlockSpec((tk, tn), lambda i, j, k: (main_k(k), j)),
                  pl.BlockSpec((tk_tail, tn), lambda i, j, k: (tail_k(k), j))],
        out_specs=pl.BlockSpec((tm, tn), lambda i, j, k: (i, j)),
        out_shape=jax.ShapeDtypeStruct((t, d), BF16),
        scratch_shapes=[pltpu.VMEM((tm, tn), F32)],
        compiler_params=_cparams(3),
        name="ffn_down",
    )(h, h, w_down, w_down)


ROPE_HALF = IDX_ROPE_DIM // 2
ROLLS_SPLIT = (LANES - ROPE_HALF, ROPE_HALF)
ROLLS_SPREAD = (LANES // 2,)


def _rope_tile(x, tabs, shifts):
    out = x * tabs[0]
    for n, shift in enumerate(shifts):
        out = out + pltpu.roll(x, shift, 1) * tabs[1 + n]
    return out


KCHUNK = 2 * Q_BLOCK
N_BIAS_TILES = 3
LOG2E = math.log2(math.e)


def _bias_tiles_body(rb_ref, o_ref):
    o = pl.program_id(0)
    tq = lax.broadcasted_iota(jnp.int32, (Q_BLOCK, KCHUNK), 0)
    sk = lax.broadcasted_iota(jnp.int32, (Q_BLOCK, KCHUNK), 1)
    n = jnp.maximum(o * Q_BLOCK + tq - sk, 0)
    max_exact = REL_BUCKETS // 2
    nf = jnp.maximum(n, 1).astype(F32)
    large = max_exact + (jnp.log(nf / max_exact) / math.log(REL_MAX_DIST / max_exact)
                         * (REL_BUCKETS - max_exact)).astype(jnp.int32)
    large = jnp.minimum(large, REL_BUCKETS - 1)
    bucket = jnp.where(n < max_exact, n, large)
    hits = [bucket == b for b in range(REL_BUCKETS - 1)]
    for h in range(A_HEADS):
        far = rb_ref[REL_BUCKETS - 1, h]
        tile = jnp.zeros((Q_BLOCK, KCHUNK), F32)
        for b in range(REL_BUCKETS - 1):
            tile = jnp.where(hits[b], (rb_ref[b, h] - far) * LOG2E, tile)
        o_ref[0, h] = tile


def _bias_tiles(rel_bias):
    return pl.pallas_call(
        _bias_tiles_body,
        grid=(N_BIAS_TILES,),
        in_specs=[pl.BlockSpec(memory_space=pltpu.SMEM)],
        out_specs=pl.BlockSpec((1, A_HEADS, Q_BLOCK, KCHUNK), lambda o: (o, 0, 0, 0)),
        out_shape=jax.ShapeDtypeStruct((N_BIAS_TILES, A_HEADS, Q_BLOCK, KCHUNK), F32),
        compiler_params=_cparams(1),
        name="rel_bias_tiles",
    )(rel_bias)


SEL_ROWS = 512
SEL_SLAB = 64


def _dsa_body(iq_ref, aq_ref, ak_ref, av_ref, ik_ref, misc_ref, rope_ref, bias_ref,
              o_ref,
              ikr_ref, vaug_ref, iq2_ref, wb_ref, st_ref, madd_ref, q2_ref,
              m_ref, l_ref, acc_ref, thr_ref,
              *, topk, seq):
    i = pl.program_id(1)
    n_chunks = i // 2 + 1
    t0 = i * Q_BLOCK
    rope = (rope_ref[0], rope_ref[1], rope_ref[2])

    @pl.when(i == 0)
    def _():
        ikr_ref[...] = jnp.zeros_like(ikr_ref)
        vaug_ref[:, :A_HEAD_DIM] = av_ref[...]
        vaug_ref[:, A_HEAD_DIM:] = jnp.ones((seq, LANES), BF16)

    ikr_ref[pl.ds(pl.multiple_of(t0, Q_BLOCK), Q_BLOCK), :] = _rope_tile(
        ik_ref[...].astype(F32), rope, ROLLS_SPLIT).astype(BF16)

    for h in range(IDX_HEADS):
        sl = slice(h * Q_BLOCK, (h + 1) * Q_BLOCK)
        iq2_ref[sl, :] = _rope_tile(iq_ref[:, sl].astype(F32), rope, ROLLS_SPLIT).astype(BF16)

    w_scale = (IDX_HEADS ** -0.5) * (IDX_HEAD_DIM ** -0.5)
    for h in range(IDX_HEADS):
        col = misc_ref[:, MISC_IW + h:MISC_IW + h + 1] * w_scale
        wb_ref[h] = jnp.broadcast_to(col, (Q_BLOCK, LANES))

    heads_per_dot = 4

    def score_chunk(jj, carry):
        k0 = pl.multiple_of(jj * KCHUNK, KCHUNK)
        kc = ikr_ref[pl.ds(k0, KCHUNK), :]
        sc = jnp.zeros((Q_BLOCK, KCHUNK), F32)
        for g in range(IDX_HEADS // heads_per_dot):
            rows = slice(g * heads_per_dot * Q_BLOCK, (g + 1) * heads_per_dot * Q_BLOCK)
            d = lax.dot_general(iq2_ref[rows, :], kc, _NT, preferred_element_type=F32)
            for hh in range(heads_per_dot):
                w = wb_ref[g * heads_per_dot + hh]
                dh = jnp.maximum(d[hh * Q_BLOCK:(hh + 1) * Q_BLOCK, :], 0.0)
                sc = sc + dh * jnp.concatenate([w, w], axis=1)
        tq = t0 + lax.broadcasted_iota(jnp.int32, sc.shape, 0)
        sk = k0 + lax.broadcasted_iota(jnp.int32, sc.shape, 1)
        sc = jnp.where(sk <= tq, sc, NEG_INF)
        st_ref[pl.ds(k0, Q_BLOCK), :] = sc[:, :Q_BLOCK].T
        st_ref[pl.ds(k0 + Q_BLOCK, Q_BLOCK), :] = sc[:, Q_BLOCK:].T
        return carry

    lax.fori_loop(0, n_chunks, score_chunk, 0)

    @pl.when(n_chunks % (SEL_ROWS // KCHUNK) == 1)
    def _():
        pad0 = pl.multiple_of(n_chunks * KCHUNK, KCHUNK)
        st_ref[pl.ds(pad0, KCHUNK), :] = jnp.full((KCHUNK, Q_BLOCK), NEG_INF, F32)

    def key_to_f32(key):
        bits = key ^ ((key >> 31) & jnp.int32(0x7FFFFFFF))
        return lax.bitcast_convert_type(bits, F32)

    kf = float(topk)

    def write_mask(k0, rows, keep_fn):
        blk = st_ref[pl.ds(k0, rows), :]
        sk = k0 + lax.broadcasted_iota(jnp.int32, blk.shape, 0)
        tq = t0 + lax.broadcasted_iota(jnp.int32, blk.shape, 1)
        keep = keep_fn(blk, sk) & (sk <= tq)
        madd_ref[pl.ds(k0, rows), :] = jnp.where(keep, 0.0, NEG_INF)

    def select_static(rows):
        def count_ge(thr_row):
            thr_b = jnp.broadcast_to(thr_row, (SEL_SLAB, LANES))
            acc = jnp.zeros((SEL_SLAB, LANES), F32)
            for r in range(rows // SEL_SLAB):
                blk = st_ref[r * SEL_SLAB:(r + 1) * SEL_SLAB, :]
                acc = jnp.where(blk >= thr_b, acc + 1.0, acc)
            acc = jnp.sum(acc.reshape(SEL_SLAB // SUBLANES, SUBLANES, LANES), axis=0)
            return jnp.sum(acc, axis=0, keepdims=True)

        def select_pass(p, key):
            cand = key + jnp.left_shift(jnp.int32(1), 31 - p)
            return jnp.where(count_ge(key_to_f32(cand)) >= kf, cand, key)

        key0 = jnp.full((1, LANES), jnp.iinfo(jnp.int32).min, jnp.int32)
        thr = key_to_f32(lax.fori_loop(0, 32, select_pass, key0))
        thr_ref[0:1, :] = thr
        thr_ref[1:2, :] = count_ge(thr)
        write_mask(0, rows, lambda blk, sk: blk >= thr)

    for n in range(1, seq // SEL_ROWS + 1):
        pl.when(i // (SEL_ROWS // Q_BLOCK) + 1 == n)(
            functools.partial(select_static, n * SEL_ROWS))

    thr = thr_ref[0:1, :]
    tie = jnp.where((thr_ref[1:2, :] > kf) & (thr > NEG_INF), 1.0, 0.0)

    @pl.when(jnp.max(tie) > 0.0)
    def _():
        def count(pred):
            def body(jj, acc):
                k0 = pl.multiple_of(jj * KCHUNK, KCHUNK)
                hit = jnp.where(pred(st_ref[pl.ds(k0, KCHUNK), :], k0), 1.0, 0.0)
                return acc + jnp.sum(hit.reshape(KCHUNK // SUBLANES, SUBLANES, LANES), axis=0)
            acc = lax.fori_loop(0, n_chunks, body, jnp.zeros((SUBLANES, LANES), F32))
            return jnp.sum(acc, axis=0, keepdims=True)

        def key_index(shape, k0):
            return k0 + lax.broadcasted_iota(jnp.int32, shape, 0)

        need = kf - count(lambda blk, k0: blk > thr)

        def index_pass(p, lo):
            cand = lo + jnp.left_shift(jnp.int32(1), 10 - p)
            n_eq = count(lambda blk, k0: (blk == thr) & (key_index(blk.shape, k0) < cand))
            return jnp.where(n_eq < need, cand, lo)

        lo = lax.fori_loop(0, 11, index_pass, jnp.zeros((1, LANES), jnp.int32))

        def rewrite(jj, carry):
            write_mask(pl.multiple_of(jj * KCHUNK, KCHUNK), KCHUNK,
                       lambda blk, sk: (blk > thr) | ((blk == thr) & (sk <= lo)))
            return carry

        lax.fori_loop(0, n_chunks, rewrite, 0)

    q_scale = (A_HEAD_DIM ** -0.5) * LOG2E
    for h in range(A_HEADS):
        sl = slice(h * Q_BLOCK, (h + 1) * Q_BLOCK)
        q2_ref[sl, :] = (aq_ref[:, sl].astype(F32) * q_scale).astype(BF16)
    m_ref[...] = jnp.full_like(m_ref, NEG_INF)
    l_ref[...] = jnp.zeros_like(l_ref)
    acc_ref[...] = jnp.zeros_like(acc_ref)

    heads_per_att = 4

    def att_chunk(jj, carry, near):
        k0 = pl.multiple_of(jj * KCHUNK, KCHUNK)
        kc = ak_ref[pl.ds(k0, KCHUNK), :]
        vc = vaug_ref[pl.ds(k0, KCHUNK), :]
        sel = jnp.concatenate([madd_ref[pl.ds(k0, Q_BLOCK), :].T,
                               madd_ref[pl.ds(k0 + Q_BLOCK, Q_BLOCK), :].T], axis=1)
        for g in range(A_HEADS // heads_per_att):
            rows = slice(g * heads_per_att * Q_BLOCK, (g + 1) * heads_per_att * Q_BLOCK)
            lg = lax.dot_general(q2_ref[rows, :], kc, _NT, preferred_element_type=F32)
            ps = []
            alphas = []
            for hh in range(heads_per_att):
                h = g * heads_per_att + hh
                hs = slice(h * Q_BLOCK, (h + 1) * Q_BLOCK)
                s = lg[hh * Q_BLOCK:(hh + 1) * Q_BLOCK, :] + sel
                if near:
                    s = s + bias_ref[i - 2 * jj, h]
                m_old = m_ref[hs, :]
                m_new = jnp.maximum(m_old, jnp.max(s, axis=1, keepdims=True))
                alphas.append(jnp.exp2(m_old - m_new))
                m_ref[hs, :] = m_new
                p = jnp.exp2(s - jnp.concatenate([m_new, m_new], axis=1))
                ps.append(p.astype(BF16))
            pv = jnp.dot(jnp.concatenate(ps, axis=0), vc, preferred_element_type=F32)
            for hh in range(heads_per_att):
                h = g * heads_per_att + hh
                hs = slice(h * Q_BLOCK, (h + 1) * Q_BLOCK)
                pv_h = pv[hh * Q_BLOCK:(hh + 1) * Q_BLOCK, :]
                acc_ref[hs, :] = alphas[hh] * acc_ref[hs, :] + pv_h[:, :A_HEAD_DIM]
                l_ref[hs, :] = alphas[hh] * l_ref[hs, :] + pv_h[:, A_HEAD_DIM:]
        return carry

    first_near = jnp.maximum(i - N_BIAS_TILES + 2, 0) // 2
    lax.fori_loop(0, first_near, functools.partial(att_chunk, near=False), 0)
    lax.fori_loop(first_near, n_chunks, functools.partial(att_chunk, near=True), 0)

    for h in range(A_HEADS):
        hs = slice(h * Q_BLOCK, (h + 1) * Q_BLOCK)
        o_ref[:, hs] = (acc_ref[hs, :] / l_ref[hs, :]).astype(BF16)


def _dsa(p, p_tail, rope_a, bias_tiles, batch, seq):
    t = p.shape[0]
    nb = seq // Q_BLOCK
    topk = min(TOPK_MAX, seq // 4)
    assert topk <= KCHUNK and seq % SEL_ROWS == 0 and seq <= 2048
    qrow = lambda b, i: b * nb + i
    stat = pltpu.VMEM((A_HEADS * Q_BLOCK, LANES), F32)
    return pl.pallas_call(
        functools.partial(_dsa_body, topk=topk, seq=seq),
        grid=(batch, nb),
        in_specs=[
            pl.BlockSpec((Q_BLOCK, IDX_HEADS * IDX_HEAD_DIM), lambda b, i: (qrow(b, i), COL_IQ // 4096)),
            pl.BlockSpec((Q_BLOCK, A_HEADS * A_HEAD_DIM), lambda b, i: (qrow(b, i), COL_AQ // 2048)),
            pl.BlockSpec((seq, LANES), lambda b, i: (b, COL_AK // LANES)),
            pl.BlockSpec((seq, LANES), lambda b, i: (b, COL_AV // LANES)),
            pl.BlockSpec((Q_BLOCK, LANES), lambda b, i: (qrow(b, i), COL_IK // LANES)),
            pl.BlockSpec((Q_BLOCK, LANES), lambda b, i: (qrow(b, i), COL_MISC // LANES)),
            pl.BlockSpec((3, Q_BLOCK, LANES), lambda b, i: (0, qrow(b, i), 0)),
            pl.BlockSpec(bias_tiles.shape, lambda b, i: (0, 0, 0, 0)),
        ],
        out_specs=pl.BlockSpec((Q_BLOCK, A_HEADS * A_HEAD_DIM), lambda b, i: (qrow(b, i), 0)),
        out_shape=jax.ShapeDtypeStruct((t, A_HEADS * A_HEAD_DIM), BF16),
        scratch_shapes=[
            pltpu.VMEM((seq, IDX_HEAD_DIM), BF16),
            pltpu.VMEM((seq, 2 * LANES), BF16),
            pltpu.VMEM((IDX_HEADS * Q_BLOCK, IDX_HEAD_DIM), BF16),
            pltpu.VMEM((IDX_HEADS, Q_BLOCK, LANES), F32),
            pltpu.VMEM((seq, Q_BLOCK), F32),
            pltpu.VMEM((seq, Q_BLOCK), F32),
            pltpu.VMEM((A_HEADS * Q_BLOCK, A_HEAD_DIM), BF16),
            stat, stat, stat,
            pltpu.VMEM((SUBLANES, LANES), F32),
        ],
        compiler_params=_cparams(2),
        name="dsa_attention",
    )(p, p, p, p, p, p_tail, rope_a, bias_tiles)


MLA_BLOCK = 512
MLA_HEADS_PER_STEP = 4


def _mla_body(q_ref, kv_ref, misc_ref, ropeq_ref, ropek_ref, o_ref, krr_ref, kc_ref, vaug_ref):
    hp = pl.program_id(1)
    qi = pl.program_id(2)
    head_w = QK_NOPE_DIM + LANES
    seq = kv_ref.shape[0]

    @pl.when((hp == 0) & (qi == 0))
    def _():
        krr_ref[...] = _rope_tile(misc_ref[...], (ropek_ref[0], ropek_ref[1]),
                                  ROLLS_SPREAD).astype(BF16)

    @pl.when(qi == 0)
    def _():
        for e in range(MLA_HEADS_PER_STEP):
            c0 = e * head_w
            kc_ref[e, :, :QK_NOPE_DIM] = kv_ref[:, c0:c0 + QK_NOPE_DIM]
            kc_ref[e, :, QK_NOPE_DIM:] = krr_ref[...]
            vaug_ref[e, :, :V_HEAD_DIM] = kv_ref[:, c0 + QK_NOPE_DIM:c0 + head_w]
            vaug_ref[e, :, V_HEAD_DIM:] = jnp.ones((seq, LANES), BF16)

    scale = (QK_NOPE_DIM + QK_ROPE_DIM) ** -0.5 * LOG2E
    qcs = []
    for e in range(MLA_HEADS_PER_STEP):
        c0 = e * head_w
        q_rope = _rope_tile(q_ref[:, c0 + QK_NOPE_DIM:c0 + head_w].astype(F32),
                            (ropeq_ref[0], ropeq_ref[1]), ROLLS_SPREAD)
        qc = jnp.concatenate([q_ref[:, c0:c0 + QK_NOPE_DIM].astype(F32), q_rope], axis=1) * scale
        qcs.append(qc.astype(BF16))
    lane_tiles = MLA_BLOCK // LANES

    def step(j, carry, masked):
        k0 = pl.multiple_of(j * MLA_BLOCK, MLA_BLOCK)
        out = []
        for e in range(MLA_HEADS_PER_STEP):
            m_old, l_old, acc = carry[e]
            s = lax.dot_general(qcs[e], kc_ref[e, pl.ds(k0, MLA_BLOCK), :], _NT,
                                preferred_element_type=F32)
            if masked:
                tq = lax.broadcasted_iota(jnp.int32, s.shape, 0)
                sk = lax.broadcasted_iota(jnp.int32, s.shape, 1)
                s = jnp.where(sk <= tq, s, NEG_INF)
            m_new = jnp.maximum(m_old, jnp.max(s, axis=1, keepdims=True))
            alpha = jnp.exp2(m_old - m_new)
            p = jnp.exp2(s - jnp.concatenate([m_new] * lane_tiles, axis=1))
            pv = jnp.dot(p.astype(BF16), vaug_ref[e, pl.ds(k0, MLA_BLOCK), :],
                         preferred_element_type=F32)
            out.append((m_new, alpha * l_old + pv[:, V_HEAD_DIM:],
                        alpha * acc + pv[:, :V_HEAD_DIM]))
        return tuple(out)

    init = (jnp.full((MLA_BLOCK, LANES), NEG_INF, F32),
            jnp.zeros((MLA_BLOCK, LANES), F32),
            jnp.zeros((MLA_BLOCK, V_HEAD_DIM), F32))
    carry = lax.fori_loop(0, qi, functools.partial(step, masked=False),
                          (init,) * MLA_HEADS_PER_STEP)
    carry = step(qi, carry, masked=True)
    for e in range(MLA_HEADS_PER_STEP):
        _, l_fin, acc = carry[e]
        o_ref[:, e * V_HEAD_DIM:(e + 1) * V_HEAD_DIM] = (acc / l_fin).astype(BF16)


def _mla(q, kv, p, rope_b, batch, seq):
    t = q.shape[0]
    nq = seq // MLA_BLOCK
    head_w = QK_NOPE_DIM + LANES
    step_w = MLA_HEADS_PER_STEP * head_w
    qrow = lambda b, hp, qi: b * nq + qi
    return pl.pallas_call(
        _mla_body,
        grid=(batch, B_HEADS // MLA_HEADS_PER_STEP, nq),
        in_specs=[
            pl.BlockSpec((MLA_BLOCK, step_w), lambda b, hp, qi: (qrow(b, hp, qi), hp)),
            pl.BlockSpec((seq, step_w), lambda b, hp, qi: (b, hp)),
            pl.BlockSpec((seq, LANES), lambda b, hp, qi: (b, COL_MISC // LANES)),
            pl.BlockSpec((2, MLA_BLOCK, LANES), lambda b, hp, qi: (0, qrow(b, hp, qi), 0)),
            pl.BlockSpec((2, seq, LANES), lambda b, hp, qi: (0, b, 0)),
        ],
        out_specs=pl.BlockSpec((MLA_BLOCK, MLA_HEADS_PER_STEP * V_HEAD_DIM),
                               lambda b, hp, qi: (qrow(b, hp, qi), hp)),
        out_shape=jax.ShapeDtypeStruct((t, B_HEADS * V_HEAD_DIM), BF16),
        scratch_shapes=[
            pltpu.VMEM((seq, LANES), BF16),
            pltpu.VMEM((MLA_HEADS_PER_STEP, seq, head_w), BF16),
            pltpu.VMEM((MLA_HEADS_PER_STEP, seq, head_w), BF16)],
        compiler_params=_cparams(3),
        name="mla_attention",
    )(q, kv, p, rope_b, rope_b)


def _rope_tables(positions):
    freqs = ROPE_THETA ** (-jnp.arange(ROPE_HALF, dtype=F32) / ROPE_HALF)
    ang = positions.astype(F32).reshape(-1, 1) * freqs
    cos, sin = jnp.cos(ang), jnp.sin(ang)
    one, zero = jnp.ones_like(cos), jnp.zeros_like(cos)
    cat = lambda parts: jnp.concatenate(parts, axis=1)
    rope_a = jnp.stack([cat([one, one, cos, cos]), cat([zero, zero, -sin, zero]),
                        cat([zero, zero, zero, sin])])
    rope_b = jnp.stack([cat([cos, zero, cos, zero]), cat([-sin, zero, sin, zero])])
    return rope_a, rope_b


IN_SPLITS = (A_HEADS * A_HEAD_DIM, A_HEAD_DIM, A_HEAD_DIM, IDX_HEADS * IDX_HEAD_DIM,
             IDX_HEAD_DIM, IDX_HEADS, Q_LORA_RANK, KV_LORA_RANK, QK_ROPE_DIM)
IN_OFFS = [int(v) for v in np.concatenate([[0], np.cumsum(IN_SPLITS)])]


def _in_proj_tail_weight(wt):
    iw, ql, kvl, kr = [wt[IN_OFFS[n]:IN_OFFS[n + 1]] for n in range(5, 9)]
    g = QK_ROPE_DIM // 2
    pad = jnp.zeros((TAIL_WIDTH - Q_LORA_RANK - KV_LORA_RANK - QK_ROPE_DIM - IDX_HEADS,
                     wt.shape[1]), wt.dtype)
    out = jnp.concatenate([ql, kvl, kr[:g], iw, kr[g:], pad], axis=0)
    assert out.shape[0] == TAIL_WIDTH and MISC_IW == g
    assert (COL_QL, COL_KVL, COL_MISC) == (0, Q_LORA_RANK, Q_LORA_RANK + KV_LORA_RANK)
    return out


def _in_proj_step_plan():
    aq0, ak0, iq0, ik0 = (IN_OFFS[n] // IN_TILE for n in (0, 1, 3, 4))
    assert all(IN_OFFS[n] % IN_TILE == 0 for n in (0, 1, 3, 4))
    pairs = lambda t0, n: [(t0 + 2 * k, t0 + 2 * k + 1) for k in range(n // IN_STEP)]
    plan = pairs(iq0, IN_SPLITS[3]) + pairs(aq0, IN_SPLITS[0]) + [(ak0, ik0)]
    assert len(plan) * IN_STEP == MAIN_WIDTH
    as_i32 = lambda v: jnp.asarray(np.asarray(v, np.int32))
    return as_i32([p[0] for p in plan]), as_i32([p[1] for p in plan])


def _in_proj_body(ia_ref, ib_ref, a_ref, wa_ref, wb_ref, o_ref):
    a = a_ref[...]
    o_ref[:, :IN_TILE] = lax.dot_general(a, wa_ref[...].astype(BF16), _NT,
                                         preferred_element_type=F32).astype(o_ref.dtype)
    o_ref[:, IN_TILE:] = lax.dot_general(a, wb_ref[...].astype(BF16), _NT,
                                         preferred_element_type=F32).astype(o_ref.dtype)


def _in_proj_main(u, wt):
    t, d = u.shape
    tm = 1024
    idx_a, idx_b = _in_proj_step_plan()
    grid_spec = pltpu.PrefetchScalarGridSpec(
        num_scalar_prefetch=2,
        grid=(t // tm, MAIN_WIDTH // IN_STEP),
        in_specs=[pl.BlockSpec((tm, d), lambda i, j, ia, ib: (i, 0)),
                  pl.BlockSpec((IN_TILE, d), lambda i, j, ia, ib: (ia[j], 0)),
                  pl.BlockSpec((IN_TILE, d), lambda i, j, ia, ib: (ib[j], 0))],
        out_specs=pl.BlockSpec((tm, IN_STEP), lambda i, j, ia, ib: (i, j)),
    )
    return pl.pallas_call(
        _in_proj_body,
        grid_spec=grid_spec,
        out_shape=jax.ShapeDtypeStruct((t, MAIN_WIDTH), BF16),
        compiler_params=_cparams(2),
        name="in_proj",
    )(idx_a, idx_b, u, wt, wt)


def _mm_nt_body(a_ref, w_ref, o_ref):
    o_ref[...] = lax.dot_general(a_ref[...], w_ref[...].astype(BF16), _NT,
                                 preferred_element_type=F32).astype(o_ref.dtype)


def _matmul_nt(a, wt, out_dtype, tm, tn, name):
    m, k = a.shape
    n = wt.shape[0]
    return pl.pallas_call(
        _mm_nt_body,
        grid=(m // tm, n // tn),
        in_specs=[pl.BlockSpec((tm, k), lambda i, j: (i, 0)),
                  pl.BlockSpec((tn, k), lambda i, j: (j, 0))],
        out_specs=pl.BlockSpec((tm, tn), lambda i, j: (i, j)),
        out_shape=jax.ShapeDtypeStruct((m, n), out_dtype),
        compiler_params=_cparams(2),
        name=name,
    )(a, wt)


def _reorder_w_uq(w):
    r = w.shape[0]
    g = QK_ROPE_DIM // 2
    w3 = w.reshape(r, B_HEADS, QK_NOPE_DIM + QK_ROPE_DIM)
    zero = jnp.zeros((r, B_HEADS, g), w.dtype)
    w3 = jnp.concatenate([w3[:, :, :QK_NOPE_DIM], w3[:, :, QK_NOPE_DIM:QK_NOPE_DIM + g], zero,
                          w3[:, :, QK_NOPE_DIM + g:], zero], axis=2)
    return w3.reshape(r, B_HEADS * 2 * LANES)


def kernel(x, c, positions, w_ada, b_ada, w_in, rel_bias, q_norm_g, w_uq, kv_norm_g, w_ukv,
           w_o, ln1_g, ln1_b, w_gate, w_up, conv_w, conv_b, w_down, ln2_g, ln2_b):
    batch, seq, d = x.shape
    depth = w_ada.shape[0]
    t = batch * seq
    assert d == D_MODEL and batch <= SUBLANES and seq % 1024 == 0
    alpha = (2 * depth) ** 0.25

    rope_a, rope_b = _rope_tables(positions)
    bias_tiles = _bias_tiles(rel_bias)
    c8 = jnp.zeros((SUBLANES, d), F32).at[:batch].set(c)
    x2 = x.reshape(t, d)

    for l in range(depth):
        mod = _ada(c8, w_ada[l], b_ada[l])
        mod3 = mod[:batch].reshape(batch * 6, 1, d)

        u = _modulate(x2, mod3, seq, shift_idx=0, scale_idx=1)
        w_in_t = jnp.swapaxes(w_in[l], 0, 1)
        p = _in_proj_main(u, w_in_t)
        p_tail = _matmul_nt(u, _in_proj_tail_weight(w_in_t), F32, 1024, 512, "in_proj_tail")
        y_a = _dsa(p, p_tail, rope_a, bias_tiles, batch, seq)
        q = _rmsnorm_matmul(p_tail, COL_QL, q_norm_g[l], _reorder_w_uq(w_uq[l]),
                            1024, 2048, "q_up_proj")
        kv = _rmsnorm_matmul(p_tail, COL_KVL, kv_norm_g[l], w_ukv[l], 1024, 4096, "kv_up_proj")
        y_b = _mla(q, kv, p_tail, rope_b, batch, seq)
        mix = _matmul_concat(y_a, y_b, w_o[l], BF16, 1024, 512, "out_proj")
        x2, u = _resln(x2, mix, mod3, seq, 2, ln1_g[l], ln1_b[l], alpha, mod_idx=(3, 4))

        hidden = _ffn_gate_up(u, w_gate[l], w_up[l], conv_w[l], conv_b[l], seq)
        y = _ffn_down(hidden, w_down[l])
        x2 = _resln(x2, y, mod3, seq, 5, ln2_g[l], ln2_b[l], alpha)

    return x2.reshape(batch, seq, d)
```

```python
import functools
import math

import jax
import jax.numpy as jnp
import numpy as np
from jax import lax
from jax.experimental import pallas as pl
from jax.experimental.pallas import tpu as pltpu

F32 = jnp.float32
BF16 = jnp.bfloat16

D_MODEL = 4096
A_HEAD_DIM = 128
A_HEADS = 16
IDX_HEADS = 32
IDX_HEAD_DIM = 128
IDX_ROPE_DIM = 64
TOPK_MAX = 256
V_HEAD_DIM = 128
B_HEADS = 16
Q_LORA_RANK = 1024
KV_LORA_RANK = 512
QK_NOPE_DIM = 128
QK_ROPE_DIM = 64
CONV_WIDTH = 3
REL_BUCKETS = 32
REL_MAX_DIST = 128
ROPE_THETA = 10000.0
Q_BLOCK = 128
LN_EPS = 1e-5
RMS_EPS = 1e-6
NEG_INF = -1e30

LANES = 128
SUBLANES = 8
VMEM_LIMIT_BYTES = 56 * 1024 * 1024

IN_TILE = 256
IN_STEP = 2 * IN_TILE
COL_IQ = 0
COL_AQ = 4096
COL_AK = 6144
COL_AV = 6272
COL_IK = 6400
MAIN_WIDTH = 6656
COL_QL = 0
COL_KVL = 1024
COL_MISC = 1536
MISC_IW = 32
TAIL_WIDTH = 2048

_NT = (((1,), (1,)), ((), ()))


def _cparams(n_axes):
    return pltpu.CompilerParams(
        dimension_semantics=("arbitrary",) * n_axes,
        vmem_limit_bytes=VMEM_LIMIT_BYTES)


def _ada_body(c_ref, w_ref, b_ref, o_ref):
    c = c_ref[...]
    act = (c / (1.0 + jnp.exp(-c))).astype(BF16)
    o_ref[...] = jnp.dot(act, w_ref[...].astype(BF16),
                         preferred_element_type=F32) + b_ref[...]


def _ada(c8, w, b):
    d, n = w.shape
    tn = 512
    return pl.pallas_call(
        _ada_body,
        grid=(n // tn,),
        in_specs=[pl.BlockSpec((SUBLANES, d), lambda j: (0, 0)),
                  pl.BlockSpec((d, tn), lambda j: (0, j)),
                  pl.BlockSpec((1, tn), lambda j: (0, j))],
        out_specs=pl.BlockSpec((SUBLANES, tn), lambda j: (0, j)),
        out_shape=jax.ShapeDtypeStruct((SUBLANES, n), F32),
        compiler_params=_cparams(1),
        name="ada_proj",
    )(c8, w, b.reshape(1, n))


def _modulate_body(x_ref, sh_ref, sc_ref, o_ref):
    o_ref[...] = (x_ref[...] * (1.0 + sc_ref[0]) + sh_ref[0]).astype(BF16)


def _modulate(x2, mod3, seq, shift_idx, scale_idx):
    t, d = x2.shape
    tm = 256
    per_seq = seq // tm
    return pl.pallas_call(
        _modulate_body,
        grid=(t // tm,),
        in_specs=[pl.BlockSpec((tm, d), lambda i: (i, 0)),
                  pl.BlockSpec((1, 1, d), lambda i: (6 * (i // per_seq) + shift_idx, 0, 0)),
                  pl.BlockSpec((1, 1, d), lambda i: (6 * (i // per_seq) + scale_idx, 0, 0))],
        out_specs=pl.BlockSpec((tm, d), lambda i: (i, 0)),
        out_shape=jax.ShapeDtypeStruct((t, d), BF16),
        compiler_params=_cparams(1),
        name="modulate",
    )(x2, mod3, mod3)


def _mm2_body(a0_ref, a1_ref, w0_ref, w1_ref, o_ref):
    acc = jnp.dot(a0_ref[...], w0_ref[...].astype(BF16), preferred_element_type=F32)
    acc = acc + jnp.dot(a1_ref[...], w1_ref[...].astype(BF16), preferred_element_type=F32)
    o_ref[...] = acc.astype(o_ref.dtype)


def _matmul_concat(a0, a1, w, out_dtype, tm, tn, name):
    m, k0 = a0.shape
    k1 = a1.shape[1]
    assert k0 == k1
    n = w.shape[1]
    return pl.pallas_call(
        _mm2_body,
        grid=(m // tm, n // tn),
        in_specs=[pl.BlockSpec((tm, k0), lambda i, j: (i, 0)),
                  pl.BlockSpec((tm, k1), lambda i, j: (i, 0)),
                  pl.BlockSpec((k0, tn), lambda i, j: (0, j)),
                  pl.BlockSpec((k1, tn), lambda i, j: (1, j))],
        out_specs=pl.BlockSpec((tm, tn), lambda i, j: (i, j)),
        out_shape=jax.ShapeDtypeStruct((m, n), out_dtype),
        compiler_params=_cparams(2),
        name=name,
    )(a0, a1, w, w)


def _rms_mm_body(x_ref, g_ref, w_ref, o_ref):
    x = x_ref[...]
    ms = jnp.mean(x * x, axis=-1, keepdims=True)
    a = (x * lax.rsqrt(ms + RMS_EPS) * g_ref[...]).astype(BF16)
    o_ref[...] = jnp.dot(a, w_ref[...].astype(BF16),
                         preferred_element_type=F32).astype(o_ref.dtype)


def _rmsnorm_matmul(p, col, gain, w, tm, tn, name):
    t = p.shape[0]
    width, n = w.shape
    cb = col // width
    assert col % width == 0
    return pl.pallas_call(
        _rms_mm_body,
        grid=(t // tm, n // tn),
        in_specs=[pl.BlockSpec((tm, width), lambda i, j: (i, cb)),
                  pl.BlockSpec((1, width), lambda i, j: (0, 0)),
                  pl.BlockSpec((width, tn), lambda i, j: (0, j))],
        out_specs=pl.BlockSpec((tm, tn), lambda i, j: (i, j)),
        out_shape=jax.ShapeDtypeStruct((t, n), BF16),
        compiler_params=_cparams(2),
        name=name,
    )(p, gain.reshape(1, width), w)


def _resln_body(x_ref, y_ref, gate_ref, lg_ref, lb_ref, *rest, alpha, with_mod):
    z = alpha * x_ref[...] + (1.0 + gate_ref[0]) * y_ref[...].astype(F32)
    mu = jnp.mean(z, axis=-1, keepdims=True)
    zc = z - mu
    var = jnp.mean(zc * zc, axis=-1, keepdims=True)
    out = zc * lax.rsqrt(var + LN_EPS) * lg_ref[...] + lb_ref[...]
    if with_mod:
        sh_ref, sc_ref, o_ref, u_ref = rest
        u_ref[...] = (out * (1.0 + sc_ref[0]) + sh_ref[0]).astype(BF16)
    else:
        (o_ref,) = rest
    o_ref[...] = out


def _resln(x2, y, mod3, seq, gate_idx, ln_g, ln_b, alpha, mod_idx=None):
    t, d = x2.shape
    tm = 256
    per_seq = seq // tm
    with_mod = mod_idx is not None
    row = pl.BlockSpec((tm, d), lambda i: (i, 0))
    vec = pl.BlockSpec((1, d), lambda i: (0, 0))

    def mod_spec(idx):
        return pl.BlockSpec((1, 1, d), lambda i: (6 * (i // per_seq) + idx, 0, 0))

    in_specs = [row, row, mod_spec(gate_idx), vec, vec]
    args = [x2, y, mod3, ln_g.reshape(1, d), ln_b.reshape(1, d)]
    out_specs = row
    out_shape = jax.ShapeDtypeStruct((t, d), F32)
    if with_mod:
        in_specs += [mod_spec(mod_idx[0]), mod_spec(mod_idx[1])]
        args += [mod3, mod3]
        out_specs = [row, row]
        out_shape = [out_shape, jax.ShapeDtypeStruct((t, d), BF16)]
    return pl.pallas_call(
        functools.partial(_resln_body, alpha=alpha, with_mod=with_mod),
        grid=(t // tm,),
        in_specs=in_specs,
        out_specs=out_specs,
        out_shape=out_shape,
        compiler_params=_cparams(1),
        name="residual_ln",
    )(*args)


def _ffn_gu_body(a_ref, wg_ref, wu_ref, cw_ref, cb_ref, o_ref, gtail_ref, *, tiles_per_seq):
    i = pl.program_id(0)
    j = pl.program_id(1)

    @pl.when((i == 0) & (j == 0))
    def _():
        gtail_ref[...] = jnp.zeros_like(gtail_ref)

    a = a_ref[...]
    wg = wg_ref[...].astype(BF16)
    wu = wu_ref[...].astype(BF16)
    g = jnp.dot(a, wg, preferred_element_type=F32)
    up = jnp.dot(a, wu, preferred_element_type=F32)
    gh = jnp.where(i % tiles_per_seq == 0, 0.0, gtail_ref[j])
    gtail_ref[j] = g[g.shape[0] - SUBLANES:, :]
    prev1 = gh[SUBLANES - 1:SUBLANES, :]
    prev2 = gh[SUBLANES - 2:SUBLANES - 1, :]
    r1 = pltpu.roll(g, 1, 0)
    r2 = pltpu.roll(g, 2, 0)
    row = lax.broadcasted_iota(jnp.int32, (SUBLANES, g.shape[1]), 0)
    head1 = jnp.where(row == 0, prev1, r1[:SUBLANES])
    head2 = jnp.where(row == 0, prev2, jnp.where(row == 1, prev1, r2[:SUBLANES]))
    g1 = jnp.concatenate([head1, r1[SUBLANES:]], axis=0)
    g2 = jnp.concatenate([head2, r2[SUBLANES:]], axis=0)
    cw = cw_ref[...]
    conv = cb_ref[...] + cw[0:1, :] * g2 + cw[1:2, :] * g1 + cw[2:3, :] * g
    o_ref[...] = (conv / (1.0 + jnp.exp(-conv)) * up).astype(BF16)


def _ffn_gate_up(u, w_gate, w_up, conv_w, conv_b, seq):
    t, d = u.shape
    f = w_gate.shape[1]
    tm, tn = 1024, 256
    return pl.pallas_call(
        functools.partial(_ffn_gu_body, tiles_per_seq=seq // tm),
        grid=(t // tm, f // tn),
        in_specs=[pl.BlockSpec((tm, d), lambda i, j: (i, 0)),
                  pl.BlockSpec((d, tn), lambda i, j: (0, j)),
                  pl.BlockSpec((d, tn), lambda i, j: (0, j)),
                  pl.BlockSpec((CONV_WIDTH, tn), lambda i, j: (0, j)),
                  pl.BlockSpec((1, tn), lambda i, j: (0, j))],
        out_specs=pl.BlockSpec((tm, tn), lambda i, j: (i, j)),
        out_shape=jax.ShapeDtypeStruct((t, f), BF16),
        scratch_shapes=[pltpu.VMEM((f // tn, SUBLANES, tn), F32)],
        compiler_params=_cparams(2),
        name="ffn_gate_up",
    )(u, w_gate, w_up, conv_w, conv_b.reshape(1, f))


def _ffn_down_body(am_ref, at_ref, wm_ref, wt_ref, o_ref, acc_ref, *, n_main, n_tail):
    k = pl.program_id(2)

    @pl.when(k == 0)
    def _():
        acc_ref[...] = jnp.zeros_like(acc_ref)

    @pl.when(k < n_main)
    def _():
        acc_ref[...] += jnp.dot(am_ref[...], wm_ref[...].astype(BF16),
                                preferred_element_type=F32)

    @pl.when(k >= n_main)
    def _():
        acc_ref[...] += jnp.dot(at_ref[...], wt_ref[...].astype(BF16),
                                preferred_element_type=F32)

    @pl.when(k == n_main + n_tail - 1)
    def _():
        o_ref[...] = acc_ref[...].astype(o_ref.dtype)


def _ffn_down(h, w_down):
    t, f = h.shape
    d = w_down.shape[1]
    tm, tn, tk, tk_tail = 2048, 1024, 1024, 256
    n_main = f // tk
    n_tail = (f - n_main * tk) // tk_tail
    assert n_main * tk + n_tail * tk_tail == f and n_tail > 0
    tail0 = n_main * tk // tk_tail

    def main_k(k):
        return jnp.minimum(k, n_main - 1)

    def tail_k(k):
        return tail0 + jnp.maximum(k - n_main, 0)

    return pl.pallas_call(
        functools.partial(_ffn_down_body, n_main=n_main, n_tail=n_tail),
        grid=(t // tm, d // tn, n_main + n_tail),
        in_specs=[pl.BlockSpec((tm, tk), lambda i, j, k: (i, main_k(k))),
                  pl.BlockSpec((tm, tk_tail), lambda i, j, k: (i, tail_k(k))),
                  pl.BlockSpec((tk, tn), lambda i, j, k: (main_k(k), j)),
                  pl.BlockSpec((tk_tail, tn), lambda i, j, k: (tail_k(k), j))],
        out_specs=pl.BlockSpec((tm, tn), lambda i, j, k: (i, j)),
        out_shape=jax.ShapeDtypeStruct((t, d), BF16),
        scratch_shapes=[pltpu.VMEM((tm, tn), F32)],
        compiler_params=_cparams(3),
        name="ffn_down",
    )(h, h, w_down, w_down)


ROPE_HALF = IDX_ROPE_DIM // 2
ROLLS_SPLIT = (LANES - ROPE_HALF, ROPE_HALF)
ROLLS_SPREAD = (LANES // 2,)


def _rope_tile(x, tabs, shifts):
    out = x * tabs[0]
    for n, shift in enumerate(shifts):
        out = out + pltpu.roll(x, shift, 1) * tabs[1 + n]
    return out


KCHUNK = 2 * Q_BLOCK
N_BIAS_TILES = 3
LOG2E = math.log2(math.e)


def _bias_tiles_body(rb_ref, o_ref):
    o = pl.program_id(0)
    tq = lax.broadcasted_iota(jnp.int32, (Q_BLOCK, KCHUNK), 0)
    sk = lax.broadcasted_iota(jnp.int32, (Q_BLOCK, KCHUNK), 1)
    n = jnp.maximum(o * Q_BLOCK + tq - sk, 0)
    max_exact = REL_BUCKETS // 2
    nf = jnp.maximum(n, 1).astype(F32)
    large = max_exact + (jnp.log(nf / max_exact) / math.log(REL_MAX_DIST / max_exact)
                         * (REL_BUCKETS - max_exact)).astype(jnp.int32)
    large = jnp.minimum(large, REL_BUCKETS - 1)
    bucket = jnp.where(n < max_exact, n, large)
    hits = [bucket == b for b in range(REL_BUCKETS - 1)]
    for h in range(A_HEADS):
        far = rb_ref[REL_BUCKETS - 1, h]
        tile = jnp.zeros((Q_BLOCK, KCHUNK), F32)
        for b in range(REL_BUCKETS - 1):
            tile = jnp.where(hits[b], (rb_ref[b, h] - far) * LOG2E, tile)
        o_ref[0, h] = tile


def _bias_tiles(rel_bias):
    return pl.pallas_call(
        _bias_tiles_body,
        grid=(N_BIAS_TILES,),
        in_specs=[pl.BlockSpec(memory_space=pltpu.SMEM)],
        out_specs=pl.BlockSpec((1, A_HEADS, Q_BLOCK, KCHUNK), lambda o: (o, 0, 0, 0)),
        out_shape=jax.ShapeDtypeStruct((N_BIAS_TILES, A_HEADS, Q_BLOCK, KCHUNK), F32),
        compiler_params=_cparams(1),
        name="rel_bias_tiles",
    )(rel_bias)


SEL_ROWS = 512
SEL_SLAB = 64


def _dsa_body(iq_ref, aq_ref, ak_ref, av_ref, ik_ref, misc_ref, rope_ref, bias_ref,
              wg_ref, wu_ref,
              o_ref, wg_bf_ref, wu_bf_ref,
              ikr_ref, vaug_ref, iq2_ref, wb_ref, st_ref, madd_ref, q2_ref,
              m_ref, l_ref, acc_ref, thr_ref,
              *, topk, seq):
    i = pl.program_id(1)
    n_chunks = i // 2 + 1
    t0 = i * Q_BLOCK
    rope = (rope_ref[0], rope_ref[1], rope_ref[2])

    wg_bf_ref[...] = wg_ref[...].astype(BF16)
    wu_bf_ref[...] = wu_ref[...].astype(BF16)

    @pl.when(i == 0)
    def _():
        ikr_ref[...] = jnp.zeros_like(ikr_ref)
        vaug_ref[:, :A_HEAD_DIM] = av_ref[...]
        vaug_ref[:, A_HEAD_DIM:] = jnp.ones((seq, LANES), BF16)

    ikr_ref[pl.ds(pl.multiple_of(t0, Q_BLOCK), Q_BLOCK), :] = _rope_tile(
        ik_ref[...].astype(F32), rope, ROLLS_SPLIT).astype(BF16)

    for h in range(IDX_HEADS):
        sl = slice(h * Q_BLOCK, (h + 1) * Q_BLOCK)
        iq2_ref[sl, :] = _rope_tile(iq_ref[:, sl].astype(F32), rope, ROLLS_SPLIT).astype(BF16)

    w_scale = (IDX_HEADS ** -0.5) * (IDX_HEAD_DIM ** -0.5)
    for h in range(IDX_HEADS):
        col = misc_ref[:, MISC_IW + h:MISC_IW + h + 1] * w_scale
        wb_ref[h] = jnp.broadcast_to(col, (Q_BLOCK, LANES))

    heads_per_dot = 4

    def score_chunk(jj, carry):
        k0 = pl.multiple_of(jj * KCHUNK, KCHUNK)
        kc = ikr_ref[pl.ds(k0, KCHUNK), :]
        sc = jnp.zeros((Q_BLOCK, KCHUNK), F32)
        for g in range(IDX_HEADS // heads_per_dot):
            rows = slice(g * heads_per_dot * Q_BLOCK, (g + 1) * heads_per_dot * Q_BLOCK)
            d = lax.dot_general(iq2_ref[rows, :], kc, _NT, preferred_element_type=F32)
            for hh in range(heads_per_dot):
                w = wb_ref[g * heads_per_dot + hh]
                dh = jnp.maximum(d[hh * Q_BLOCK:(hh + 1) * Q_BLOCK, :], 0.0)
                sc = sc + dh * jnp.concatenate([w, w], axis=1)
        tq = t0 + lax.broadcasted_iota(jnp.int32, sc.shape, 0)
        sk = k0 + lax.broadcasted_iota(jnp.int32, sc.shape, 1)
        sc = jnp.where(sk <= tq, sc, NEG_INF)
        st_ref[pl.ds(k0, Q_BLOCK), :] = sc[:, :Q_BLOCK].T
        st_ref[pl.ds(k0 + Q_BLOCK, Q_BLOCK), :] = sc[:, Q_BLOCK:].T
        return carry

    lax.fori_loop(0, n_chunks, score_chunk, 0)

    @pl.when(n_chunks % (SEL_ROWS // KCHUNK) == 1)
    def _():
        pad0 = pl.multiple_of(n_chunks * KCHUNK, KCHUNK)
        st_ref[pl.ds(pad0, KCHUNK), :] = jnp.full((KCHUNK, Q_BLOCK), NEG_INF, F32)

    def key_to_f32(key):
        bits = key ^ ((key >> 31) & jnp.int32(0x7FFFFFFF))
        return lax.bitcast_convert_type(bits, F32)

    kf = float(topk)

    def write_mask(k0, rows, keep_fn):
        blk = st_ref[pl.ds(k0, rows), :]
        sk = k0 + lax.broadcasted_iota(jnp.int32, blk.shape, 0)
        tq = t0 + lax.broadcasted_iota(jnp.int32, blk.shape, 1)
        keep = keep_fn(blk, sk) & (sk <= tq)
        madd_ref[pl.ds(k0, rows), :] = jnp.where(keep, 0.0, NEG_INF)

    def select_static(rows):
        def count_ge(thr_row):
            thr_b = jnp.broadcast_to(thr_row, (SEL_SLAB, LANES))
            acc = jnp.zeros((SEL_SLAB, LANES), F32)
            for r in range(rows // SEL_SLAB):
                blk = st_ref[r * SEL_SLAB:(r + 1) * SEL_SLAB, :]
                acc = jnp.where(blk >= thr_b, acc + 1.0, acc)
            acc = jnp.sum(acc.reshape(SEL_SLAB // SUBLANES, SUBLANES, LANES), axis=0)
            return jnp.sum(acc, axis=0, keepdims=True)

        def select_pass(p, key):
            cand = key + jnp.left_shift(jnp.int32(1), 31 - p)
            return jnp.where(count_ge(key_to_f32(cand)) >= kf, cand, key)

        key0 = jnp.full((1, LANES), jnp.iinfo(jnp.int32).min, jnp.int32)
        thr = key_to_f32(lax.fori_loop(0, 32, select_pass, key0))
        thr_ref[0:1, :] = thr
        thr_ref[1:2, :] = count_ge(thr)
        write_mask(0, rows, lambda blk, sk: blk >= thr)

    for n in range(1, seq // SEL_ROWS + 1):
        pl.when(i // (SEL_ROWS // Q_BLOCK) + 1 == n)(
            functools.partial(select_static, n * SEL_ROWS))

    thr = thr_ref[0:1, :]
    tie = jnp.where((thr_ref[1:2, :] > kf) & (thr > NEG_INF), 1.0, 0.0)

    @pl.when(jnp.max(tie) > 0.0)
    def _():
        def count(pred):
            def body(jj, acc):
                k0 = pl.multiple_of(jj * KCHUNK, KCHUNK)
                hit = jnp.where(pred(st_ref[pl.ds(k0, KCHUNK), :], k0), 1.0, 0.0)
                return acc + jnp.sum(hit.reshape(KCHUNK // SUBLANES, SUBLANES, LANES), axis=0)
            acc = lax.fori_loop(0, n_chunks, body, jnp.zeros((SUBLANES, LANES), F32))
            return jnp.sum(acc, axis=0, keepdims=True)

        def key_index(shape, k0):
            return k0 + lax.broadcasted_iota(jnp.int32, shape, 0)

        need = kf - count(lambda blk, k0: blk > thr)

        def index_pass(p, lo):
            cand = lo + jnp.left_shift(jnp.int32(1), 10 - p)
            n_eq = count(lambda blk, k0: (blk == thr) & (key_index(blk.shape, k0) < cand))
            return jnp.where(n_eq < need, cand, lo)

        lo = lax.fori_loop(0, 11, index_pass, jnp.zeros((1, LANES), jnp.int32))

        def rewrite(jj, carry):
            write_mask(pl.multiple_of(jj * KCHUNK, KCHUNK), KCHUNK,
                       lambda blk, sk: (blk > thr) | ((blk == thr) & (sk <= lo)))
            return carry

        lax.fori_loop(0, n_chunks, rewrite, 0)

    q_scale = (A_HEAD_DIM ** -0.5) * LOG2E
    for h in range(A_HEADS):
        sl = slice(h * Q_BLOCK, (h + 1) * Q_BLOCK)
        q2_ref[sl, :] = (aq_ref[:, sl].astype(F32) * q_scale).astype(BF16)
    m_ref[...] = jnp.full_like(m_ref, NEG_INF)
    l_ref[...] = jnp.zeros_like(l_ref)
    acc_ref[...] = jnp.zeros_like(acc_ref)

    heads_per_att = 4

    def att_chunk(jj, carry, near):
        k0 = pl.multiple_of(jj * KCHUNK, KCHUNK)
        kc = ak_ref[pl.ds(k0, KCHUNK), :]
        vc = vaug_ref[pl.ds(k0, KCHUNK), :]
        sel = jnp.concatenate([madd_ref[pl.ds(k0, Q_BLOCK), :].T,
                               madd_ref[pl.ds(k0 + Q_BLOCK, Q_BLOCK), :].T], axis=1)
        for g in range(A_HEADS // heads_per_att):
            rows = slice(g * heads_per_att * Q_BLOCK, (g + 1) * heads_per_att * Q_BLOCK)
            lg = lax.dot_general(q2_ref[rows, :], kc, _NT, preferred_element_type=F32)
            ps = []
            alphas = []
            for hh in range(heads_per_att):
                h = g * heads_per_att + hh
                hs = slice(h * Q_BLOCK, (h + 1) * Q_BLOCK)
                s = lg[hh * Q_BLOCK:(hh + 1) * Q_BLOCK, :] + sel
                if near:
                    s = s + bias_ref[i - 2 * jj, h]
                m_old = m_ref[hs, :]
                m_new = jnp.maximum(m_old, jnp.max(s, axis=1, keepdims=True))
                alphas.append(jnp.exp2(m_old - m_new))
                m_ref[hs, :] = m_new
                p = jnp.exp2(s - jnp.concatenate([m_new, m_new], axis=1))
                ps.append(p.astype(BF16))
            pv = jnp.dot(jnp.concatenate(ps, axis=0), vc, preferred_element_type=F32)
            for hh in range(heads_per_att):
                h = g * heads_per_att + hh
                hs = slice(h * Q_BLOCK, (h + 1) * Q_BLOCK)
                pv_h = pv[hh * Q_BLOCK:(hh + 1) * Q_BLOCK, :]
                acc_ref[hs, :] = alphas[hh] * acc_ref[hs, :] + pv_h[:, :A_HEAD_DIM]
                l_ref[hs, :] = alphas[hh] * l_ref[hs, :] + pv_h[:, A_HEAD_DIM:]
        return carry

    first_near = jnp.maximum(i - N_BIAS_TILES + 2, 0) // 2
    lax.fori_loop(0, first_near, functools.partial(att_chunk, near=False), 0)
    lax.fori_loop(first_near, n_chunks, functools.partial(att_chunk, near=True), 0)

    for h in range(A_HEADS):
        hs = slice(h * Q_BLOCK, (h + 1) * Q_BLOCK)
        o_ref[:, hs] = (acc_ref[hs, :] / l_ref[hs, :]).astype(BF16)


def _dsa(p, p_tail, rope_a, bias_tiles, w_gate, w_up, batch, seq):
    t = p.shape[0]
    nb = seq // Q_BLOCK
    topk = min(TOPK_MAX, seq // 4)
    assert topk <= KCHUNK and seq % SEL_ROWS == 0 and seq <= 2048
    qrow = lambda b, i: b * nb + i
    stat = pltpu.VMEM((A_HEADS * Q_BLOCK, LANES), F32)
    w_rows, f = w_gate.shape
    slab = w_rows // (batch * nb)
    assert slab * batch * nb == w_rows and slab % (2 * SUBLANES) == 0 and w_up.shape == w_gate.shape
    w_spec = pl.BlockSpec((slab, f), lambda b, i: (qrow(b, i), 0))
    return pl.pallas_call(
        functools.partial(_dsa_body, topk=topk, seq=seq),
        grid=(batch, nb),
        in_specs=[
            pl.BlockSpec((Q_BLOCK, IDX_HEADS * IDX_HEAD_DIM), lambda b, i: (qrow(b, i), COL_IQ // 4096)),
            pl.BlockSpec((Q_BLOCK, A_HEADS * A_HEAD_DIM), lambda b, i: (qrow(b, i), COL_AQ // 2048)),
            pl.BlockSpec((seq, LANES), lambda b, i: (b, COL_AK // LANES)),
            pl.BlockSpec((seq, LANES), lambda b, i: (b, COL_AV // LANES)),
            pl.BlockSpec((Q_BLOCK, LANES), lambda b, i: (qrow(b, i), COL_IK // LANES)),
            pl.BlockSpec((Q_BLOCK, LANES), lambda b, i: (qrow(b, i), COL_MISC // LANES)),
            pl.BlockSpec((3, Q_BLOCK, LANES), lambda b, i: (0, qrow(b, i), 0)),
            pl.BlockSpec(bias_tiles.shape, lambda b, i: (0, 0, 0, 0)),
            w_spec, w_spec,
        ],
        out_specs=[pl.BlockSpec((Q_BLOCK, A_HEADS * A_HEAD_DIM), lambda b, i: (qrow(b, i), 0)),
                   w_spec, w_spec],
        out_shape=[jax.ShapeDtypeStruct((t, A_HEADS * A_HEAD_DIM), BF16),
                   jax.ShapeDtypeStruct(w_gate.shape, BF16),
                   jax.ShapeDtypeStruct(w_up.shape, BF16)],
        scratch_shapes=[
            pltpu.VMEM((seq, IDX_HEAD_DIM), BF16),
            pltpu.VMEM((seq, 2 * LANES), BF16),
            pltpu.VMEM((IDX_HEADS * Q_BLOCK, IDX_HEAD_DIM), BF16),
            pltpu.VMEM((IDX_HEADS, Q_BLOCK, LANES), F32),
            pltpu.VMEM((seq, Q_BLOCK), F32),
            pltpu.VMEM((seq, Q_BLOCK), F32),
            pltpu.VMEM((A_HEADS * Q_BLOCK, A_HEAD_DIM), BF16),
            stat, stat, stat,
            pltpu.VMEM((SUBLANES, LANES), F32),
        ],
        compiler_params=_cparams(2),
        name="dsa_attention",
    )(p, p, p, p, p, p_tail, rope_a, bias_tiles, w_gate, w_up)


MLA_BLOCK = 512
MLA_HEADS_PER_STEP = 4


def _mla_body(q_ref, kv_ref, misc_ref, ropeq_ref, ropek_ref, o_ref, krr_ref, kc_ref, vaug_ref):
    hp = pl.program_id(1)
    qi = pl.program_id(2)
    head_w = QK_NOPE_DIM + LANES
    seq = kv_ref.shape[0]

    @pl.when((hp == 0) & (qi == 0))
    def _():
        krr_ref[...] = _rope_tile(misc_ref[...], (ropek_ref[0], ropek_ref[1]),
                                  ROLLS_SPREAD).astype(BF16)

    @pl.when(qi == 0)
    def _():
        for e in range(MLA_HEADS_PER_STEP):
            c0 = e * head_w
            kc_ref[e, :, :QK_NOPE_DIM] = kv_ref[:, c0:c0 + QK_NOPE_DIM]
            kc_ref[e, :, QK_NOPE_DIM:] = krr_ref[...]
            vaug_ref[e, :, :V_HEAD_DIM] = kv_ref[:, c0 + QK_NOPE_DIM:c0 + head_w]
            vaug_ref[e, :, V_HEAD_DIM:] = jnp.ones((seq, LANES), BF16)

    scale = (QK_NOPE_DIM + QK_ROPE_DIM) ** -0.5 * LOG2E
    qcs = []
    for e in range(MLA_HEADS_PER_STEP):
        c0 = e * head_w
        q_rope = _rope_tile(q_ref[:, c0 + QK_NOPE_DIM:c0 + head_w].astype(F32),
                            (ropeq_ref[0], ropeq_ref[1]), ROLLS_SPREAD)
        qc = jnp.concatenate([q_ref[:, c0:c0 + QK_NOPE_DIM].astype(F32), q_rope], axis=1) * scale
        qcs.append(qc.astype(BF16))
    lane_tiles = MLA_BLOCK // LANES

    def step(j, carry, masked):
        k0 = pl.multiple_of(j * MLA_BLOCK, MLA_BLOCK)
        out = []
        for e in range(MLA_HEADS_PER_STEP):
            m_old, l_old, acc = carry[e]
            s = lax.dot_general(qcs[e], kc_ref[e, pl.ds(k0, MLA_BLOCK), :], _NT,
                                preferred_element_type=F32)
            if masked:
                tq = lax.broadcasted_iota(jnp.int32, s.shape, 0)
                sk = lax.broadcasted_iota(jnp.int32, s.shape, 1)
                s = jnp.where(sk <= tq, s, NEG_INF)
            m_new = jnp.maximum(m_old, jnp.max(s, axis=1, keepdims=True))
            alpha = jnp.exp2(m_old - m_new)
            p = jnp.exp2(s - jnp.concatenate([m_new] * lane_tiles, axis=1))
            pv = jnp.dot(p.astype(BF16), vaug_ref[e, pl.ds(k0, MLA_BLOCK), :],
                         preferred_element_type=F32)
            out.append((m_new, alpha * l_old + pv[:, V_HEAD_DIM:],
                        alpha * acc + pv[:, :V_HEAD_DIM]))
        return tuple(out)

    init = (jnp.full((MLA_BLOCK, LANES), NEG_INF, F32),
            jnp.zeros((MLA_BLOCK, LANES), F32),
            jnp.zeros((MLA_BLOCK, V_HEAD_DIM), F32))
    carry = lax.fori_loop(0, qi, functools.partial(step, masked=False),
                          (init,) * MLA_HEADS_PER_STEP)
    carry = step(qi, carry, masked=True)
    for e in range(MLA_HEADS_PER_STEP):
        _, l_fin, acc = carry[e]
        o_ref[:, e * V_HEAD_DIM:(e + 1) * V_HEAD_DIM] = (acc / l_fin).astype(BF16)


def _mla(q, kv, p, rope_b, batch, seq):
    t = q.shape[0]
    nq = seq // MLA_BLOCK
    head_w = QK_NOPE_DIM + LANES
    step_w = MLA_HEADS_PER_STEP * head_w
    qrow = lambda b, hp, qi: b * nq + qi
    return pl.pallas_call(
        _mla_body,
        grid=(batch, B_HEADS // MLA_HEADS_PER_STEP, nq),
        in_specs=[
            pl.BlockSpec((MLA_BLOCK, step_w), lambda b, hp, qi: (qrow(b, hp, qi), hp)),
            pl.BlockSpec((seq, step_w), lambda b, hp, qi: (b, hp)),
            pl.BlockSpec((seq, LANES), lambda b, hp, qi: (b, COL_MISC // LANES)),
            pl.BlockSpec((2, MLA_BLOCK, LANES), lambda b, hp, qi: (0, qrow(b, hp, qi), 0)),
            pl.BlockSpec((2, seq, LANES), lambda b, hp, qi: (0, b, 0)),
        ],
        out_specs=pl.BlockSpec((MLA_BLOCK, MLA_HEADS_PER_STEP * V_HEAD_DIM),
                               lambda b, hp, qi: (qrow(b, hp, qi), hp)),
        out_shape=jax.ShapeDtypeStruct((t, B_HEADS * V_HEAD_DIM), BF16),
        scratch_shapes=[
            pltpu.VMEM((seq, LANES), BF16),
            pltpu.VMEM((MLA_HEADS_PER_STEP, seq, head_w), BF16),
            pltpu.VMEM((MLA_HEADS_PER_STEP, seq, head_w), BF16)],
        compiler_params=_cparams(3),
        name="mla_attention",
    )(q, kv, p, rope_b, rope_b)


def _rope_tables(positions):
    freqs = ROPE_THETA ** (-jnp.arange(ROPE_HALF, dtype=F32) / ROPE_HALF)
    ang = positions.astype(F32).reshape(-1, 1) * freqs
    cos, sin = jnp.cos(ang), jnp.sin(ang)
    one, zero = jnp.ones_like(cos), jnp.zeros_like(cos)
    cat = lambda parts: jnp.concatenate(parts, axis=1)
    rope_a = jnp.stack([cat([one, one, cos, cos]), cat([zero, zero, -sin, zero]),
                        cat([zero, zero, zero, sin])])
    rope_b = jnp.stack([cat([cos, zero, cos, zero]), cat([-sin, zero, sin, zero])])
    return rope_a, rope_b


IN_SPLITS = (A_HEADS * A_HEAD_DIM, A_HEAD_DIM, A_HEAD_DIM, IDX_HEADS * IDX_HEAD_DIM,
             IDX_HEAD_DIM, IDX_HEADS, Q_LORA_RANK, KV_LORA_RANK, QK_ROPE_DIM)
IN_OFFS = [int(v) for v in np.concatenate([[0], np.cumsum(IN_SPLITS)])]


def _in_proj_tail_weight(wt):
    iw, ql, kvl, kr = [wt[IN_OFFS[n]:IN_OFFS[n + 1]] for n in range(5, 9)]
    g = QK_ROPE_DIM // 2
    pad = jnp.zeros((TAIL_WIDTH - Q_LORA_RANK - KV_LORA_RANK - QK_ROPE_DIM - IDX_HEADS,
                     wt.shape[1]), wt.dtype)
    out = jnp.concatenate([ql, kvl, kr[:g], iw, kr[g:], pad], axis=0)
    assert out.shape[0] == TAIL_WIDTH and MISC_IW == g
    assert (COL_QL, COL_KVL, COL_MISC) == (0, Q_LORA_RANK, Q_LORA_RANK + KV_LORA_RANK)
    return out


def _in_proj_step_plan():
    aq0, ak0, iq0, ik0 = (IN_OFFS[n] // IN_TILE for n in (0, 1, 3, 4))
    assert all(IN_OFFS[n] % IN_TILE == 0 for n in (0, 1, 3, 4))
    pairs = lambda t0, n: [(t0 + 2 * k, t0 + 2 * k + 1) for k in range(n // IN_STEP)]
    plan = pairs(iq0, IN_SPLITS[3]) + pairs(aq0, IN_SPLITS[0]) + [(ak0, ik0)]
    assert len(plan) * IN_STEP == MAIN_WIDTH
    as_i32 = lambda v: jnp.asarray(np.asarray(v, np.int32))
    return as_i32([p[0] for p in plan]), as_i32([p[1] for p in plan])


def _in_proj_body(ia_ref, ib_ref, a_ref, wa_ref, wb_ref, o_ref):
    a = a_ref[...]
    o_ref[:, :IN_TILE] = lax.dot_general(a, wa_ref[...].astype(BF16), _NT,
                                         preferred_element_type=F32).astype(o_ref.dtype)
    o_ref[:, IN_TILE:] = lax.dot_general(a, wb_ref[...].astype(BF16), _NT,
                                         preferred_element_type=F32).astype(o_ref.dtype)


def _in_proj_main(u, wt):
    t, d = u.shape
    tm = 1024
    idx_a, idx_b = _in_proj_step_plan()
    grid_spec = pltpu.PrefetchScalarGridSpec(
        num_scalar_prefetch=2,
        grid=(t // tm, MAIN_WIDTH // IN_STEP),
        in_specs=[pl.BlockSpec((tm, d), lambda i, j, ia, ib: (i, 0)),
                  pl.BlockSpec((IN_TILE, d), lambda i, j, ia, ib: (ia[j], 0)),
                  pl.BlockSpec((IN_TILE, d), lambda i, j, ia, ib: (ib[j], 0))],
        out_specs=pl.BlockSpec((tm, IN_STEP), lambda i, j, ia, ib: (i, j)),
    )
    return pl.pallas_call(
        _in_proj_body,
        grid_spec=grid_spec,
        out_shape=jax.ShapeDtypeStruct((t, MAIN_WIDTH), BF16),
        compiler_params=_cparams(2),
        name="in_proj",
    )(idx_a, idx_b, u, wt, wt)


def _mm_nt_body(a_ref, w_ref, o_ref):
    o_ref[...] = lax.dot_general(a_ref[...], w_ref[...].astype(BF16), _NT,
                                 preferred_element_type=F32).astype(o_ref.dtype)


def _matmul_nt(a, wt, out_dtype, tm, tn, name):
    m, k = a.shape
    n = wt.shape[0]
    return pl.pallas_call(
        _mm_nt_body,
        grid=(m // tm, n // tn),
        in_specs=[pl.BlockSpec((tm, k), lambda i, j: (i, 0)),
                  pl.BlockSpec((tn, k), lambda i, j: (j, 0))],
        out_specs=pl.BlockSpec((tm, tn), lambda i, j: (i, j)),
        out_shape=jax.ShapeDtypeStruct((m, n), out_dtype),
        compiler_params=_cparams(2),
        name=name,
    )(a, wt)


def _reorder_w_uq(w):
    r = w.shape[0]
    g = QK_ROPE_DIM // 2
    w3 = w.reshape(r, B_HEADS, QK_NOPE_DIM + QK_ROPE_DIM)
    zero = jnp.zeros((r, B_HEADS, g), w.dtype)
    w3 = jnp.concatenate([w3[:, :, :QK_NOPE_DIM], w3[:, :, QK_NOPE_DIM:QK_NOPE_DIM + g], zero,
                          w3[:, :, QK_NOPE_DIM + g:], zero], axis=2)
    return w3.reshape(r, B_HEADS * 2 * LANES)


def kernel(x, c, positions, w_ada, b_ada, w_in, rel_bias, q_norm_g, w_uq, kv_norm_g, w_ukv,
           w_o, ln1_g, ln1_b, w_gate, w_up, conv_w, conv_b, w_down, ln2_g, ln2_b):
    batch, seq, d = x.shape
    depth = w_ada.shape[0]
    t = batch * seq
    assert d == D_MODEL and batch <= SUBLANES and seq % 1024 == 0
    alpha = (2 * depth) ** 0.25

    rope_a, rope_b = _rope_tables(positions)
    bias_tiles = _bias_tiles(rel_bias)
    c8 = jnp.zeros((SUBLANES, d), F32).at[:batch].set(c)
    x2 = x.reshape(t, d)

    for l in range(depth):
        mod = _ada(c8, w_ada[l], b_ada[l])
        mod3 = mod[:batch].reshape(batch * 6, 1, d)

        u = _modulate(x2, mod3, seq, shift_idx=0, scale_idx=1)
        w_in_t = jnp.swapaxes(w_in[l], 0, 1)
        p = _in_proj_main(u, w_in_t)
        p_tail = _matmul_nt(u, _in_proj_tail_weight(w_in_t), F32, 1024, 512, "in_proj_tail")
        y_a, w_gate_bf, w_up_bf = _dsa(p, p_tail, rope_a, bias_tiles, w_gate[l], w_up[l],
                                       batch, seq)
        q = _rmsnorm_matmul(p_tail, COL_QL, q_norm_g[l], _reorder_w_uq(w_uq[l]),
                            1024, 2048, "q_up_proj")
        kv = _rmsnorm_matmul(p_tail, COL_KVL, kv_norm_g[l], w_ukv[l], 1024, 4096, "kv_up_proj")
        y_b = _mla(q, kv, p_tail, rope_b, batch, seq)
        mix = _matmul_concat(y_a, y_b, w_o[l], BF16, 1024, 512, "out_proj")
        x2, u = _resln(x2, mix, mod3, seq, 2, ln1_g[l], ln1_b[l], alpha, mod_idx=(3, 4))

        hidden = _ffn_gate_up(u, w_gate_bf, w_up_bf, conv_w[l], conv_b[l], seq)
        y = _ffn_down(hidden, w_down[l])
        x2 = _resln(x2, y, mod3, seq, 5, ln2_g[l], ln2_b[l], alpha)

    return x2.reshape(batch, seq, d)
```

```python
import functools
import math

import jax
import jax.numpy as jnp
import numpy as np
from jax import lax
from jax.experimental import pallas as pl
from jax.experimental.pallas import tpu as pltpu

F32 = jnp.float32
BF16 = jnp.bfloat16

D_MODEL = 4096
A_HEAD_DIM = 128
A_HEADS = 16
IDX_HEADS = 32
IDX_HEAD_DIM = 128
IDX_ROPE_DIM = 64
TOPK_MAX = 256
V_HEAD_DIM = 128
B_HEADS = 16
Q_LORA_RANK = 1024
KV_LORA_RANK = 512
QK_NOPE_DIM = 128
QK_ROPE_DIM = 64
CONV_WIDTH = 3
REL_BUCKETS = 32
REL_MAX_DIST = 128
ROPE_THETA = 10000.0
Q_BLOCK = 128
LN_EPS = 1e-5
RMS_EPS = 1e-6
NEG_INF = -1e30

LANES = 128
SUBLANES = 8
VMEM_LIMIT_BYTES = 56 * 1024 * 1024

IN_TILE = 256
IN_STEP = 2 * IN_TILE
COL_IQ = 0
COL_AQ = 4096
COL_AK = 6144
COL_AV = 6272
COL_IK = 6400
MAIN_WIDTH = 6656
COL_QL = 0
COL_KVL = 1024
COL_MISC = 1536
MISC_IW = 32
TAIL_WIDTH = 2048

_NT = (((1,), (1,)), ((), ()))


def _cparams(n_axes):
    return pltpu.CompilerParams(
        dimension_semantics=("arbitrary",) * n_axes,
        vmem_limit_bytes=VMEM_LIMIT_BYTES)


def _ada_body(c_ref, w_ref, b_ref, o_ref):
    c = c_ref[...]
    act = (c / (1.0 + jnp.exp(-c))).astype(BF16)
    o_ref[...] = jnp.dot(act, w_ref[...].astype(BF16),
                         preferred_element_type=F32) + b_ref[...]


def _ada(c8, w, b):
    d, n = w.shape
    tn = 512
    return pl.pallas_call(
        _ada_body,
        grid=(n // tn,),
        in_specs=[pl.BlockSpec((SUBLANES, d), lambda j: (0, 0)),
                  pl.BlockSpec((d, tn), lambda j: (0, j)),
                  pl.BlockSpec((1, tn), lambda j: (0, j))],
        out_specs=pl.BlockSpec((SUBLANES, tn), lambda j: (0, j)),
        out_shape=jax.ShapeDtypeStruct((SUBLANES, n), F32),
        compiler_params=_cparams(1),
        name="ada_proj",
    )(c8, w, b.reshape(1, n))


def _modulate_body(x_ref, sh_ref, sc_ref, o_ref):
    o_ref[...] = (x_ref[...] * (1.0 + sc_ref[0]) + sh_ref[0]).astype(BF16)


def _modulate(x2, mod3, seq, shift_idx, scale_idx):
    t, d = x2.shape
    tm = 256
    per_seq = seq // tm
    return pl.pallas_call(
        _modulate_body,
        grid=(t // tm,),
        in_specs=[pl.BlockSpec((tm, d), lambda i: (i, 0)),
                  pl.BlockSpec((1, 1, d), lambda i: (6 * (i // per_seq) + shift_idx, 0, 0)),
                  pl.BlockSpec((1, 1, d), lambda i: (6 * (i // per_seq) + scale_idx, 0, 0))],
        out_specs=pl.BlockSpec((tm, d), lambda i: (i, 0)),
        out_shape=jax.ShapeDtypeStruct((t, d), BF16),
        compiler_params=_cparams(1),
        name="modulate",
    )(x2, mod3, mod3)


def _mm2_body(a0_ref, a1_ref, w0_ref, w1_ref, o_ref):
    acc = jnp.dot(a0_ref[...], w0_ref[...].astype(BF16), preferred_element_type=F32)
    acc = acc + jnp.dot(a1_ref[...], w1_ref[...].astype(BF16), preferred_element_type=F32)
    o_ref[...] = acc.astype(o_ref.dtype)


def _matmul_concat(a0, a1, w, out_dtype, tm, tn, name):
    m, k0 = a0.shape
    k1 = a1.shape[1]
    assert k0 == k1
    n = w.shape[1]
    return pl.pallas_call(
        _mm2_body,
        grid=(m // tm, n // tn),
        in_specs=[pl.BlockSpec((tm, k0), lambda i, j: (i, 0)),
                  pl.BlockSpec((tm, k1), lambda i, j: (i, 0)),
                  pl.BlockSpec((k0, tn), lambda i, j: (0, j)),
                  pl.BlockSpec((k1, tn), lambda i, j: (1, j))],
        out_specs=pl.BlockSpec((tm, tn), lambda i, j: (i, j)),
        out_shape=jax.ShapeDtypeStruct((m, n), out_dtype),
        compiler_params=_cparams(2),
        name=name,
    )(a0, a1, w, w)


def _rms_mm_body(x_ref, g_ref, w_ref, o_ref):
    x = x_ref[...]
    ms = jnp.mean(x * x, axis=-1, keepdims=True)
    a = (x * lax.rsqrt(ms + RMS_EPS) * g_ref[...]).astype(BF16)
    o_ref[...] = jnp.dot(a, w_ref[...].astype(BF16),
                         preferred_element_type=F32).astype(o_ref.dtype)


def _rmsnorm_matmul(p, col, gain, w, tm, tn, name):
    t = p.shape[0]
    width, n = w.shape
    cb = col // width
    assert col % width == 0
    return pl.pallas_call(
        _rms_mm_body,
        grid=(t // tm, n // tn),
        in_specs=[pl.BlockSpec((tm, width), lambda i, j: (i, cb)),
                  pl.BlockSpec((1, width), lambda i, j: (0, 0)),
                  pl.BlockSpec((width, tn), lambda i, j: (0, j))],
        out_specs=pl.BlockSpec((tm, tn), lambda i, j: (i, j)),
        out_shape=jax.ShapeDtypeStruct((t, n), BF16),
        compiler_params=_cparams(2),
        name=name,
    )(p, gain.reshape(1, width), w)


def _resln_body(x_ref, y_ref, gate_ref, lg_ref, lb_ref, *rest, alpha, with_mod):
    z = alpha * x_ref[...] + (1.0 + gate_ref[0]) * y_ref[...].astype(F32)
    mu = jnp.mean(z, axis=-1, keepdims=True)
    zc = z - mu
    var = jnp.mean(zc * zc, axis=-1, keepdims=True)
    out = zc * lax.rsqrt(var + LN_EPS) * lg_ref[...] + lb_ref[...]
    if with_mod:
        sh_ref, sc_ref, o_ref, u_ref = rest
        u_ref[...] = (out * (1.0 + sc_ref[0]) + sh_ref[0]).astype(BF16)
    else:
        (o_ref,) = rest
    o_ref[...] = out


def _resln(x2, y, mod3, seq, gate_idx, ln_g, ln_b, alpha, mod_idx=None):
    t, d = x2.shape
    tm = 256
    per_seq = seq // tm
    with_mod = mod_idx is not None
    row = pl.BlockSpec((tm, d), lambda i: (i, 0))
    vec = pl.BlockSpec((1, d), lambda i: (0, 0))

    def mod_spec(idx):
        return pl.BlockSpec((1, 1, d), lambda i: (6 * (i // per_seq) + idx, 0, 0))

    in_specs = [row, row, mod_spec(gate_idx), vec, vec]
    args = [x2, y, mod3, ln_g.reshape(1, d), ln_b.reshape(1, d)]
    out_specs = row
    out_shape = jax.ShapeDtypeStruct((t, d), F32)
    if with_mod:
        in_specs += [mod_spec(mod_idx[0]), mod_spec(mod_idx[1])]
        args += [mod3, mod3]
        out_specs = [row, row]
        out_shape = [out_shape, jax.ShapeDtypeStruct((t, d), BF16)]
    return pl.pallas_call(
        functools.partial(_resln_body, alpha=alpha, with_mod=with_mod),
        grid=(t // tm,),
        in_specs=in_specs,
        out_specs=out_specs,
        out_shape=out_shape,
        compiler_params=_cparams(1),
        name="residual_ln",
    )(*args)


def _ffn_gu_body(a_ref, halo_ref, wg_ref, wu_ref, cw_ref, cb_ref, o_ref, *, tiles_per_seq):
    i = pl.program_id(0)
    a = a_ref[...]
    wg = wg_ref[...].astype(BF16)
    wu = wu_ref[...].astype(BF16)
    g = jnp.dot(a, wg, preferred_element_type=F32)
    up = jnp.dot(a, wu, preferred_element_type=F32)
    gh = jnp.dot(halo_ref[...], wg, preferred_element_type=F32)
    gh = jnp.where(i % tiles_per_seq == 0, 0.0, gh)
    prev1 = gh[SUBLANES - 1:SUBLANES, :]
    prev2 = gh[SUBLANES - 2:SUBLANES - 1, :]
    r1 = pltpu.roll(g, 1, 0)
    r2 = pltpu.roll(g, 2, 0)
    row = lax.broadcasted_iota(jnp.int32, (SUBLANES, g.shape[1]), 0)
    head1 = jnp.where(row == 0, prev1, r1[:SUBLANES])
    head2 = jnp.where(row == 0, prev2, jnp.where(row == 1, prev1, r2[:SUBLANES]))
    g1 = jnp.concatenate([head1, r1[SUBLANES:]], axis=0)
    g2 = jnp.concatenate([head2, r2[SUBLANES:]], axis=0)
    cw = cw_ref[...]
    conv = cb_ref[...] + cw[0:1, :] * g2 + cw[1:2, :] * g1 + cw[2:3, :] * g
    o_ref[...] = (conv / (1.0 + jnp.exp(-conv)) * up).astype(BF16)


def _ffn_gate_up(u, w_gate, w_up, conv_w, conv_b, seq):
    t, d = u.shape
    f = w_gate.shape[1]
    tm, tn = 1024, 256
    halo_blocks = tm // SUBLANES
    return pl.pallas_call(
        functools.partial(_ffn_gu_body, tiles_per_seq=seq // tm),
        grid=(t // tm, f // tn),
        in_specs=[pl.BlockSpec((tm, d), lambda i, j: (i, 0)),
                  pl.BlockSpec((SUBLANES, d), lambda i, j: (jnp.maximum(i * halo_blocks - 1, 0), 0)),
                  pl.BlockSpec((d, tn), lambda i, j: (0, j)),
                  pl.BlockSpec((d, tn), lambda i, j: (0, j)),
                  pl.BlockSpec((CONV_WIDTH, tn), lambda i, j: (0, j)),
                  pl.BlockSpec((1, tn), lambda i, j: (0, j))],
        out_specs=pl.BlockSpec((tm, tn), lambda i, j: (i, j)),
        out_shape=jax.ShapeDtypeStruct((t, f), BF16),
        compiler_params=_cparams(2),
        name="ffn_gate_up",
    )(u, u, w_gate, w_up, conv_w, conv_b.reshape(1, f))


def _ffn_down_body(am_ref, at_ref, wm_ref, wt_ref, o_ref, acc_ref, *, n_main, n_tail):
    k = pl.program_id(2)

    @pl.when(k == 0)
    def _():
        acc_ref[...] = jnp.zeros_like(acc_ref)

    @pl.when(k < n_main)
    def _():
        acc_ref[...] += jnp.dot(am_ref[...], wm_ref[...].astype(BF16),
                                preferred_element_type=F32)

    @pl.when(k >= n_main)
    def _():
        acc_ref[...] += jnp.dot(at_ref[...], wt_ref[...].astype(BF16),
                                preferred_element_type=F32)

    @pl.when(k == n_main + n_tail - 1)
    def _():
        o_ref[...] = acc_ref[...].astype(o_ref.dtype)


def _ffn_down(h, w_down):
    t, f = h.shape
    d = w_down.shape[1]
    tm, tn, tk, tk_tail = 2048, 1024, 1024, 256
    n_main = f // tk
    n_tail = (f - n_main * tk) // tk_tail
    assert n_main * tk + n_tail * tk_tail == f and n_tail > 0
    tail0 = n_main * tk // tk_tail

    def main_k(k):
        return jnp.minimum(k, n_main - 1)

    def tail_k(k):
        return tail0 + jnp.maximum(k - n_main, 0)

    return pl.pallas_call(
        functools.partial(_ffn_down_body, n_main=n_main, n_tail=n_tail),
        grid=(t // tm, d // tn, n_main + n_tail),
        in_specs=[pl.BlockSpec((tm, tk), lambda i, j, k: (i, main_k(k))),
                  pl.BlockSpec((tm, tk_tail), lambda i, j, k: (i, tail_k(k))),
                  pl.BlockSpec((tk, tn), lambda i, j, k: (main_k(k), j)),
                  pl.BlockSpec((tk_tail, tn), lambda i, j, k: (tail_k(k), j))],
        out_specs=pl.BlockSpec((tm, tn), lambda i, j, k: (i, j)),
        out_shape=jax.ShapeDtypeStruct((t, d), BF16),
        scratch_shapes=[pltpu.VMEM((tm, tn), F32)],
        compiler_params=_cparams(3),
        name="ffn_down",
    )(h, h, w_down, w_down)


ROPE_HALF = IDX_ROPE_DIM // 2
ROLLS_SPLIT = (LANES - ROPE_HALF, ROPE_HALF)
ROLLS_SPREAD = (LANES // 2,)


def _rope_tile(x, tabs, shifts):
    out = x * tabs[0]
    for n, shift in enumerate(shifts):
        out = out + pltpu.roll(x, shift, 1) * tabs[1 + n]
    return out


KCHUNK = 2 * Q_BLOCK
N_BIAS_TILES = 3
LOG2E = math.log2(math.e)


def _bias_tiles_body(rb_ref, o_ref):
    o = pl.program_id(0)
    tq = lax.broadcasted_iota(jnp.int32, (Q_BLOCK, KCHUNK), 0)
    sk = lax.broadcasted_iota(jnp.int32, (Q_BLOCK, KCHUNK), 1)
    n = jnp.maximum(o * Q_BLOCK + tq - sk, 0)
    max_exact = REL_BUCKETS // 2
    nf = jnp.maximum(n, 1).astype(F32)
    large = max_exact + (jnp.log(nf / max_exact) / math.log(REL_MAX_DIST / max_exact)
                         * (REL_BUCKETS - max_exact)).astype(jnp.int32)
    large = jnp.minimum(large, REL_BUCKETS - 1)
    bucket = jnp.where(n < max_exact, n, large)
    hits = [bucket == b for b in range(REL_BUCKETS - 1)]
    for h in range(A_HEADS):
        far = rb_ref[REL_BUCKETS - 1, h]
        tile = jnp.zeros((Q_BLOCK, KCHUNK), F32)
        for b in range(REL_BUCKETS - 1):
            tile = jnp.where(hits[b], (rb_ref[b, h] - far) * LOG2E, tile)
        o_ref[0, h] = tile


def _bias_tiles(rel_bias):
    return pl.pallas_call(
        _bias_tiles_body,
        grid=(N_BIAS_TILES,),
        in_specs=[pl.BlockSpec(memory_space=pltpu.SMEM)],
        out_specs=pl.BlockSpec((1, A_HEADS, Q_BLOCK, KCHUNK), lambda o: (o, 0, 0, 0)),
        out_shape=jax.ShapeDtypeStruct((N_BIAS_TILES, A_HEADS, Q_BLOCK, KCHUNK), F32),
        compiler_params=_cparams(1),
        name="rel_bias_tiles",
    )(rel_bias)


SEL_ROWS = 512
SEL_SLAB = 64


def _dsa_body(iq_ref, aq_ref, ak_ref, av_ref, ik_ref, misc_ref, rope_ref, bias_ref,
              o_ref,
              ikr_ref, vaug_ref, iq2_ref, wb_ref, st_ref, madd_ref, q2_ref,
              m_ref, l_ref, acc_ref, thr_ref,
              *, topk, seq):
    i = pl.program_id(1)
    n_chunks = i // 2 + 1
    t0 = i * Q_BLOCK
    rope = (rope_ref[0], rope_ref[1], rope_ref[2])

    @pl.when(i == 0)
    def _():
        ikr_ref[...] = jnp.zeros_like(ikr_ref)
        vaug_ref[:, :A_HEAD_DIM] = av_ref[...]
        vaug_ref[:, A_HEAD_DIM:] = jnp.ones((seq, LANES), BF16)

    ikr_ref[pl.ds(pl.multiple_of(t0, Q_BLOCK), Q_BLOCK), :] = _rope_tile(
        ik_ref[...].astype(F32), rope, ROLLS_SPLIT).astype(BF16)

    for h in range(IDX_HEADS):
        sl = slice(h * Q_BLOCK, (h + 1) * Q_BLOCK)
        iq2_ref[sl, :] = _rope_tile(iq_ref[:, sl].astype(F32), rope, ROLLS_SPLIT).astype(BF16)

    w_scale = (IDX_HEADS ** -0.5) * (IDX_HEAD_DIM ** -0.5)
    for h in range(IDX_HEADS):
        col = misc_ref[:, MISC_IW + h:MISC_IW + h + 1] * w_scale
        wb_ref[h] = jnp.broadcast_to(col, (Q_BLOCK, LANES))

    heads_per_dot = 4

    def score_chunk(jj, carry):
        k0 = pl.multiple_of(jj * KCHUNK, KCHUNK)
        kc = ikr_ref[pl.ds(k0, KCHUNK), :]
        sc = jnp.zeros((Q_BLOCK, KCHUNK), F32)
        for g in range(IDX_HEADS // heads_per_dot):
            rows = slice(g * heads_per_dot * Q_BLOCK, (g + 1) * heads_per_dot * Q_BLOCK)
            d = lax.dot_general(iq2_ref[rows, :], kc, _NT, preferred_element_type=F32)
            for hh in range(heads_per_dot):
                w = wb_ref[g * heads_per_dot + hh]
                dh = jnp.maximum(d[hh * Q_BLOCK:(hh + 1) * Q_BLOCK, :], 0.0)
                sc = sc + dh * jnp.concatenate([w, w], axis=1)
        tq = t0 + lax.broadcasted_iota(jnp.int32, sc.shape, 0)
        sk = k0 + lax.broadcasted_iota(jnp.int32, sc.shape, 1)
        sc = jnp.where(sk <= tq, sc, NEG_INF)
        st_ref[pl.ds(k0, Q_BLOCK), :] = sc[:, :Q_BLOCK].T
        st_ref[pl.ds(k0 + Q_BLOCK, Q_BLOCK), :] = sc[:, Q_BLOCK:].T
        return carry

    lax.fori_loop(0, n_chunks, score_chunk, 0)

    @pl.when(n_chunks % (SEL_ROWS // KCHUNK) == 1)
    def _():
        pad0 = pl.multiple_of(n_chunks * KCHUNK, KCHUNK)
        st_ref[pl.ds(pad0, KCHUNK), :] = jnp.full((KCHUNK, Q_BLOCK), NEG_INF, F32)

    def key_to_f32(key):
        bits = key ^ ((key >> 31) & jnp.int32(0x7FFFFFFF))
        return lax.bitcast_convert_type(bits, F32)

    kf = float(topk)

    def write_mask(k0, rows, keep_fn):
        blk = st_ref[pl.ds(k0, rows), :]
        sk = k0 + lax.broadcasted_iota(jnp.int32, blk.shape, 0)
        tq = t0 + lax.broadcasted_iota(jnp.int32, blk.shape, 1)
        keep = keep_fn(blk, sk) & (sk <= tq)
        madd_ref[pl.ds(k0, rows), :] = jnp.where(keep, 0.0, NEG_INF)

    def select_static(rows):
        def count_ge(thr_row):
            thr_b = jnp.broadcast_to(thr_row, (SEL_SLAB, LANES))
            acc = jnp.zeros((SEL_SLAB, LANES), F32)
            for r in range(rows // SEL_SLAB):
                blk = st_ref[r * SEL_SLAB:(r + 1) * SEL_SLAB, :]
                acc = jnp.where(blk >= thr_b, acc + 1.0, acc)
            acc = jnp.sum(acc.reshape(SEL_SLAB // SUBLANES, SUBLANES, LANES), axis=0)
            return jnp.sum(acc, axis=0, keepdims=True)

        def select_pass(p, key):
            cand = key + jnp.left_shift(jnp.int32(1), 31 - p)
            return jnp.where(count_ge(key_to_f32(cand)) >= kf, cand, key)

        key0 = jnp.full((1, LANES), jnp.iinfo(jnp.int32).min, jnp.int32)
        thr = key_to_f32(lax.fori_loop(0, 32, select_pass, key0))
        thr_ref[0:1, :] = thr
        thr_ref[1:2, :] = count_ge(thr)
        write_mask(0, rows, lambda blk, sk: blk >= thr)

    for n in range(1, seq // SEL_ROWS + 1):
        pl.when(i // (SEL_ROWS // Q_BLOCK) + 1 == n)(
            functools.partial(select_static, n * SEL_ROWS))

    thr = thr_ref[0:1, :]
    tie = jnp.where((thr_ref[1:2, :] > kf) & (thr > NEG_INF), 1.0, 0.0)

    @pl.when(jnp.max(tie) > 0.0)
    def _():
        def count(pred):
            def body(jj, acc):
                k0 = pl.multiple_of(jj * KCHUNK, KCHUNK)
                hit = jnp.where(pred(st_ref[pl.ds(k0, KCHUNK), :], k0), 1.0, 0.0)
                return acc + jnp.sum(hit.reshape(KCHUNK // SUBLANES, SUBLANES, LANES), axis=0)
            acc = lax.fori_loop(0, n_chunks, body, jnp.zeros((SUBLANES, LANES), F32))
            return jnp.sum(acc, axis=0, keepdims=True)

        def key_index(shape, k0):
            return k0 + lax.broadcasted_iota(jnp.int32, shape, 0)

        need = kf - count(lambda blk, k0: blk > thr)

        def index_pass(p, lo):
            cand = lo + jnp.left_shift(jnp.int32(1), 10 - p)
            n_eq = count(lambda blk, k0: (blk == thr) & (key_index(blk.shape, k0) < cand))
            return jnp.where(n_eq < need, cand, lo)

        lo = lax.fori_loop(0, 11, index_pass, jnp.zeros((1, LANES), jnp.int32))

        def rewrite(jj, carry):
            write_mask(pl.multiple_of(jj * KCHUNK, KCHUNK), KCHUNK,
                       lambda blk, sk: (blk > thr) | ((blk == thr) & (sk <= lo)))
            return carry

        lax.fori_loop(0, n_chunks, rewrite, 0)

    q_scale = (A_HEAD_DIM ** -0.5) * LOG2E
    for h in range(A_HEADS):
        sl = slice(h * Q_BLOCK, (h + 1) * Q_BLOCK)
        q2_ref[sl, :] = (aq_ref[:, sl].astype(F32) * q_scale).astype(BF16)
    m_ref[...] = jnp.full_like(m_ref, NEG_INF)
    l_ref[...] = jnp.zeros_like(l_ref)
    acc_ref[...] = jnp.zeros_like(acc_ref)

    heads_per_att = 4

    def att_chunk(jj, carry, near):
        k0 = pl.multiple_of(jj * KCHUNK, KCHUNK)
        kc = ak_ref[pl.ds(k0, KCHUNK), :]
        vc = vaug_ref[pl.ds(k0, KCHUNK), :]
        sel = jnp.concatenate([madd_ref[pl.ds(k0, Q_BLOCK), :].T,
                               madd_ref[pl.ds(k0 + Q_BLOCK, Q_BLOCK), :].T], axis=1)
        for g in range(A_HEADS // heads_per_att):
            rows = slice(g * heads_per_att * Q_BLOCK, (g + 1) * heads_per_att * Q_BLOCK)
            lg = lax.dot_general(q2_ref[rows, :], kc, _NT, preferred_element_type=F32)
            ps = []
            alphas = []
            for hh in range(heads_per_att):
                h = g * heads_per_att + hh
                hs = slice(h * Q_BLOCK, (h + 1) * Q_BLOCK)
                s = lg[hh * Q_BLOCK:(hh + 1) * Q_BLOCK, :] + sel
                if near:
                    s = s + bias_ref[i - 2 * jj, h]
                m_old = m_ref[hs, :]
                m_new = jnp.maximum(m_old, jnp.max(s, axis=1, keepdims=True))
                alphas.append(jnp.exp2(m_old - m_new))
                m_ref[hs, :] = m_new
                p = jnp.exp2(s - jnp.concatenate([m_new, m_new], axis=1))
                ps.append(p.astype(BF16))
            pv = jnp.dot(jnp.concatenate(ps, axis=0), vc, preferred_element_type=F32)
            for hh in range(heads_per_att):
                h = g * heads_per_att + hh
                hs = slice(h * Q_BLOCK, (h + 1) * Q_BLOCK)
                pv_h = pv[hh * Q_BLOCK:(hh + 1) * Q_BLOCK, :]
                acc_ref[hs, :] = alphas[hh] * acc_ref[hs, :] + pv_h[:, :A_HEAD_DIM]
                l_ref[hs, :] = alphas[hh] * l_ref[hs, :] + pv_h[:, A_HEAD_DIM:]
        return carry

    first_near = jnp.maximum(i - N_BIAS_TILES + 2, 0) // 2
    lax.fori_loop(0, first_near, functools.partial(att_chunk, near=False), 0)
    lax.fori_loop(first_near, n_chunks, functools.partial(att_chunk, near=True), 0)

    for h in range(A_HEADS):
        hs = slice(h * Q_BLOCK, (h + 1) * Q_BLOCK)
        o_ref[:, hs] = (acc_ref[hs, :] / l_ref[hs, :]).astype(BF16)


def _dsa(p, p_tail, rope_a, bias_tiles, batch, seq):
    t = p.shape[0]
    nb = seq // Q_BLOCK
    topk = min(TOPK_MAX, seq // 4)
    assert topk <= KCHUNK and seq % SEL_ROWS == 0 and seq <= 2048
    qrow = lambda b, i: b * nb + i
    stat = pltpu.VMEM((A_HEADS * Q_BLOCK, LANES), F32)
    return pl.pallas_call(
        functools.partial(_dsa_body, topk=topk, seq=seq),
        grid=(batch, nb),
        in_specs=[
            pl.BlockSpec((Q_BLOCK, IDX_HEADS * IDX_HEAD_DIM), lambda b, i: (qrow(b, i), COL_IQ // 4096)),
            pl.BlockSpec((Q_BLOCK, A_HEADS * A_HEAD_DIM), lambda b, i: (qrow(b, i), COL_AQ // 2048)),
            pl.BlockSpec((seq, LANES), lambda b, i: (b, COL_AK // LANES)),
            pl.BlockSpec((seq, LANES), lambda b, i: (b, COL_AV // LANES)),
            pl.BlockSpec((Q_BLOCK, LANES), lambda b, i: (qrow(b, i), COL_IK // LANES)),
            pl.BlockSpec((Q_BLOCK, LANES), lambda b, i: (qrow(b, i), COL_MISC // LANES)),
            pl.BlockSpec((3, Q_BLOCK, LANES), lambda b, i: (0, qrow(b, i), 0)),
            pl.BlockSpec(bias_tiles.shape, lambda b, i: (0, 0, 0, 0)),
        ],
        out_specs=pl.BlockSpec((Q_BLOCK, A_HEADS * A_HEAD_DIM), lambda b, i: (qrow(b, i), 0)),
        out_shape=jax.ShapeDtypeStruct((t, A_HEADS * A_HEAD_DIM), BF16),
        scratch_shapes=[
            pltpu.VMEM((seq, IDX_HEAD_DIM), BF16),
            pltpu.VMEM((seq, 2 * LANES), BF16),
            pltpu.VMEM((IDX_HEADS * Q_BLOCK, IDX_HEAD_DIM), BF16),
            pltpu.VMEM((IDX_HEADS, Q_BLOCK, LANES), F32),
            pltpu.VMEM((seq, Q_BLOCK), F32),
            pltpu.VMEM((seq, Q_BLOCK), F32),
            pltpu.VMEM((A_HEADS * Q_BLOCK, A_HEAD_DIM), BF16),
            stat, stat, stat,
            pltpu.VMEM((SUBLANES, LANES), F32),
        ],
        compiler_params=_cparams(2),
        name="dsa_attention",
    )(p, p, p, p, p, p_tail, rope_a, bias_tiles)


MLA_BLOCK = 512
MLA_HEADS_PER_STEP = 4


def _mla_body(q_ref, kv_ref, misc_ref, ropeq_ref, ropek_ref, o_ref, krr_ref, kc_ref, vaug_ref):
    hp = pl.program_id(1)
    qi = pl.program_id(2)
    head_w = QK_NOPE_DIM + LANES
    seq = kv_ref.shape[0]

    @pl.when((hp == 0) & (qi == 0))
    def _():
        krr_ref[...] = _rope_tile(misc_ref[...], (ropek_ref[0], ropek_ref[1]),
                                  ROLLS_SPREAD).astype(BF16)

    @pl.when(qi == 0)
    def _():
        for e in range(MLA_HEADS_PER_STEP):
            c0 = e * head_w
            kc_ref[e, :, :QK_NOPE_DIM] = kv_ref[:, c0:c0 + QK_NOPE_DIM]
            kc_ref[e, :, QK_NOPE_DIM:] = krr_ref[...]
            vaug_ref[e, :, :V_HEAD_DIM] = kv_ref[:, c0 + QK_NOPE_DIM:c0 + head_w]
            vaug_ref[e, :, V_HEAD_DIM:] = jnp.ones((seq, LANES), BF16)

    scale = (QK_NOPE_DIM + QK_ROPE_DIM) ** -0.5 * LOG2E
    qcs = []
    for e in range(MLA_HEADS_PER_STEP):
        c0 = e * head_w
        q_rope = _rope_tile(q_ref[:, c0 + QK_NOPE_DIM:c0 + head_w].astype(F32),
                            (ropeq_ref[0], ropeq_ref[1]), ROLLS_SPREAD)
        qc = jnp.concatenate([q_ref[:, c0:c0 + QK_NOPE_DIM].astype(F32), q_rope], axis=1) * scale
        qcs.append(qc.astype(BF16))
    lane_tiles = MLA_BLOCK // LANES

    def step(j, carry, masked):
        k0 = pl.multiple_of(j * MLA_BLOCK, MLA_BLOCK)
        out = []
        for e in range(MLA_HEADS_PER_STEP):
            m_old, l_old, acc = carry[e]
            s = lax.dot_general(qcs[e], kc_ref[e, pl.ds(k0, MLA_BLOCK), :], _NT,
                                preferred_element_type=F32)
            if masked:
                tq = lax.broadcasted_iota(jnp.int32, s.shape, 0)
                sk = lax.broadcasted_iota(jnp.int32, s.shape, 1)
                s = jnp.where(sk <= tq, s, NEG_INF)
            m_new = jnp.maximum(m_old, jnp.max(s, axis=1, keepdims=True))
            alpha = jnp.exp2(m_old - m_new)
            p = jnp.exp2(s - jnp.concatenate([m_new] * lane_tiles, axis=1))
            pv = jnp.dot(p.astype(BF16), vaug_ref[e, pl.ds(k0, MLA_BLOCK), :],
                         preferred_element_type=F32)
            out.append((m_new, alpha * l_old + pv[:, V_HEAD_DIM:],
                        alpha * acc + pv[:, :V_HEAD_DIM]))
        return tuple(out)

    init = (jnp.full((MLA_BLOCK, LANES), NEG_INF, F32),
            jnp.zeros((MLA_BLOCK, LANES), F32),
            jnp.zeros((MLA_BLOCK, V_HEAD_DIM), F32))
    carry = lax.fori_loop(0, qi, functools.partial(step, masked=False),
                          (init,) * MLA_HEADS_PER_STEP)
    carry = step(qi, carry, masked=True)
    for e in range(MLA_HEADS_PER_STEP):
        _, l_fin, acc = carry[e]
        o_ref[:, e * V_HEAD_DIM:(e + 1) * V_HEAD_DIM] = (acc / l_fin).astype(BF16)


def _mla(q, kv, p, rope_b, batch, seq):
    t = q.shape[0]
    nq = seq // MLA_BLOCK
    head_w = QK_NOPE_DIM + LANES
    step_w = MLA_HEADS_PER_STEP * head_w
    qrow = lambda b, hp, qi: b * nq + qi
    return pl.pallas_call(
        _mla_body,
        grid=(batch, B_HEADS // MLA_HEADS_PER_STEP, nq),
        in_specs=[
            pl.BlockSpec((MLA_BLOCK, step_w), lambda b, hp, qi: (qrow(b, hp, qi), hp)),
            pl.BlockSpec((seq, step_w), lambda b, hp, qi: (b, hp)),
            pl.BlockSpec((seq, LANES), lambda b, hp, qi: (b, COL_MISC // LANES)),
            pl.BlockSpec((2, MLA_BLOCK, LANES), lambda b, hp, qi: (0, qrow(b, hp, qi), 0)),
            pl.BlockSpec((2, seq, LANES), lambda b, hp, qi: (0, b, 0)),
        ],
        out_specs=pl.BlockSpec((MLA_BLOCK, MLA_HEADS_PER_STEP * V_HEAD_DIM),
                               lambda b, hp, qi: (qrow(b, hp, qi), hp)),
        out_shape=jax.ShapeDtypeStruct((t, B_HEADS * V_HEAD_DIM), BF16),
        scratch_shapes=[
            pltpu.VMEM((seq, LANES), BF16),
            pltpu.VMEM((MLA_HEADS_PER_STEP, seq, head_w), BF16),
            pltpu.VMEM((MLA_HEADS_PER_STEP, seq, head_w), BF16)],
        compiler_params=_cparams(3),
        name="mla_attention",
    )(q, kv, p, rope_b, rope_b)


def _rope_tables(positions):
    freqs = ROPE_THETA ** (-jnp.arange(ROPE_HALF, dtype=F32) / ROPE_HALF)
    ang = positions.astype(F32).reshape(-1, 1) * freqs
    cos, sin = jnp.cos(ang), jnp.sin(ang)
    one, zero = jnp.ones_like(cos), jnp.zeros_like(cos)
    cat = lambda parts: jnp.concatenate(parts, axis=1)
    rope_a = jnp.stack([cat([one, one, cos, cos]), cat([zero, zero, -sin, zero]),
                        cat([zero, zero, zero, sin])])
    rope_b = jnp.stack([cat([cos, zero, cos, zero]), cat([-sin, zero, sin, zero])])
    return rope_a, rope_b


IN_SPLITS = (A_HEADS * A_HEAD_DIM, A_HEAD_DIM, A_HEAD_DIM, IDX_HEADS * IDX_HEAD_DIM,
             IDX_HEAD_DIM, IDX_HEADS, Q_LORA_RANK, KV_LORA_RANK, QK_ROPE_DIM)
IN_OFFS = [int(v) for v in np.concatenate([[0], np.cumsum(IN_SPLITS)])]


def _in_proj_tail_weight(wt):
    iw, ql, kvl, kr = [wt[IN_OFFS[n]:IN_OFFS[n + 1]] for n in range(5, 9)]
    g = QK_ROPE_DIM // 2
    pad = jnp.zeros((TAIL_WIDTH - Q_LORA_RANK - KV_LORA_RANK - QK_ROPE_DIM - IDX_HEADS,
                     wt.shape[1]), wt.dtype)
    out = jnp.concatenate([ql, kvl, kr[:g], iw, kr[g:], pad], axis=0)
    assert out.shape[0] == TAIL_WIDTH and MISC_IW == g
    assert (COL_QL, COL_KVL, COL_MISC) == (0, Q_LORA_RANK, Q_LORA_RANK + KV_LORA_RANK)
    return out


def _in_proj_step_plan():
    aq0, ak0, iq0, ik0 = (IN_OFFS[n] // IN_TILE for n in (0, 1, 3, 4))
    assert all(IN_OFFS[n] % IN_TILE == 0 for n in (0, 1, 3, 4))
    pairs = lambda t0, n: [(t0 + 2 * k, t0 + 2 * k + 1) for k in range(n // IN_STEP)]
    plan = pairs(iq0, IN_SPLITS[3]) + pairs(aq0, IN_SPLITS[0]) + [(ak0, ik0)]
    assert len(plan) * IN_STEP == MAIN_WIDTH
    as_i32 = lambda v: jnp.asarray(np.asarray(v, np.int32))
    return as_i32([p[0] for p in plan]), as_i32([p[1] for p in plan])


def _in_proj_body(ia_ref, ib_ref, a_ref, wa_ref, wb_ref, o_ref):
    a = a_ref[...]
    o_ref[:, :IN_TILE] = lax.dot_general(a, wa_ref[...].astype(BF16), _NT,
                                         preferred_element_type=F32).astype(o_ref.dtype)
    o_ref[:, IN_TILE:] = lax.dot_general(a, wb_ref[...].astype(BF16), _NT,
                                         preferred_element_type=F32).astype(o_ref.dtype)


def _in_proj_main(u, wt):
    t, d = u.shape
    tm = 1024
    idx_a, idx_b = _in_proj_step_plan()
    grid_spec = pltpu.PrefetchScalarGridSpec(
        num_scalar_prefetch=2,
        grid=(t // tm, MAIN_WIDTH // IN_STEP),
        in_specs=[pl.BlockSpec((tm, d), lambda i, j, ia, ib: (i, 0)),
                  pl.BlockSpec((IN_TILE, d), lambda i, j, ia, ib: (ia[j], 0)),
                  pl.BlockSpec((IN_TILE, d), lambda i, j, ia, ib: (ib[j], 0))],
        out_specs=pl.BlockSpec((tm, IN_STEP), lambda i, j, ia, ib: (i, j)),
    )
    return pl.pallas_call(
        _in_proj_body,
        grid_spec=grid_spec,
        out_shape=jax.ShapeDtypeStruct((t, MAIN_WIDTH), BF16),
        compiler_params=_cparams(2),
        name="in_proj",
    )(idx_a, idx_b, u, wt, wt)


def _mm_nt_body(a_ref, w_ref, o_ref):
    o_ref[...] = lax.dot_general(a_ref[...], w_ref[...].astype(BF16), _NT,
                                 preferred_element_type=F32).astype(o_ref.dtype)


def _matmul_nt(a, wt, out_dtype, tm, tn, name):
    m, k = a.shape
    n = wt.shape[0]
    return pl.pallas_call(
        _mm_nt_body,
        grid=(m // tm, n // tn),
        in_specs=[pl.BlockSpec((tm, k), lambda i, j: (i, 0)),
                  pl.BlockSpec((tn, k), lambda i, j: (j, 0))],
        out_specs=pl.BlockSpec((tm, tn), lambda i, j: (i, j)),
        out_shape=jax.ShapeDtypeStruct((m, n), out_dtype),
        compiler_params=_cparams(2),
        name=name,
    )(a, wt)


def _reorder_w_uq(w):
    r = w.shape[0]
    g = QK_ROPE_DIM // 2
    w3 = w.reshape(r, B_HEADS, QK_NOPE_DIM + QK_ROPE_DIM)
    zero = jnp.zeros((r, B_HEADS, g), w.dtype)
    w3 = jnp.concatenate([w3[:, :, :QK_NOPE_DIM], w3[:, :, QK_NOPE_DIM:QK_NOPE_DIM + g], zero,
                          w3[:, :, QK_NOPE_DIM + g:], zero], axis=2)
    return w3.reshape(r, B_HEADS * 2 * LANES)


def kernel(x, c, positions, w_ada, b_ada, w_in, rel_bias, q_norm_g, w_uq, kv_norm_g, w_ukv,
           w_o, ln1_g, ln1_b, w_gate, w_up, conv_w, conv_b, w_down, ln2_g, ln2_b):
    batch, seq, d = x.shape
    depth = w_ada.shape[0]
    t = batch * seq
    assert d == D_MODEL and batch <= SUBLANES and seq % 1024 == 0
    alpha = (2 * depth) ** 0.25

    rope_a, rope_b = _rope_tables(positions)
    bias_tiles = _bias_tiles(rel_bias)
    c8 = jnp.zeros((SUBLANES, d), F32).at[:batch].set(c)
    x2 = x.reshape(t, d)

    for l in range(depth):
        mod = _ada(c8, w_ada[l], b_ada[l])
        mod3 = mod[:batch].reshape(batch * 6, 1, d)

        u = _modulate(x2, mod3, seq, shift_idx=0, scale_idx=1)
        w_in_t = jnp.swapaxes(w_in[l], 0, 1)
        p = _in_proj_main(u, w_in_t)
        p_tail = _matmul_nt(u, _in_proj_tail_weight(w_in_t), F32, 1024, 512, "in_proj_tail")
        y_a = _dsa(p, p_tail, rope_a, bias_tiles, batch, seq)
        q = _rmsnorm_matmul(p_tail, COL_QL, q_norm_g[l], _reorder_w_uq(w_uq[l]),
                            1024, 2048, "q_up_proj")
        kv = _rmsnorm_matmul(p_tail, COL_KVL, kv_norm_g[l], w_ukv[l], 1024, 4096, "kv_up_proj")
        y_b = _mla(q, kv, p_tail, rope_b, batch, seq)
        mix = _matmul_concat(y_a, y_b, w_o[l], BF16, 1024, 512, "out_proj")
        x2, u = _resln(x2, mix, mod3, seq, 2, ln1_g[l], ln1_b[l], alpha, mod_idx=(3, 4))

        hidden = _ffn_gate_up(u, w_gate[l], w_up[l], conv_w[l], conv_b[l], seq)
        y = _ffn_down(hidden, w_down[l])
        x2 = _resln(x2, y, mod3, seq, 5, ln2_g[l], ln2_b[l], alpha)

    return x2.reshape(batch, seq, d)
```

```python
import functools
import math

import jax
import jax.numpy as jnp
import numpy as np
from jax import lax
from jax.experimental import pallas as pl
from jax.experimental.pallas import tpu as pltpu

F32 = jnp.float32
BF16 = jnp.bfloat16

D_MODEL = 4096
A_HEAD_DIM = 128
A_HEADS = 16
IDX_HEADS = 32
IDX_HEAD_DIM = 128
IDX_ROPE_DIM = 64
TOPK_MAX = 256
V_HEAD_DIM = 128
B_HEADS = 16
Q_LORA_RANK = 1024
KV_LORA_RANK = 512
QK_NOPE_DIM = 128
QK_ROPE_DIM = 64
CONV_WIDTH = 3
REL_BUCKETS = 32
REL_MAX_DIST = 128
ROPE_THETA = 10000.0
Q_BLOCK = 128
LN_EPS = 1e-5
RMS_EPS = 1e-6
NEG_INF = -1e30

LANES = 128
SUBLANES = 8
VMEM_LIMIT_BYTES = 56 * 1024 * 1024

IN_TILE = 256
IN_STEP = 2 * IN_TILE
COL_IQ = 0
COL_AQ = 4096
COL_AK = 6144
COL_AV = 6272
COL_IK = 6400
MAIN_WIDTH = 6656
COL_QL = 0
COL_KVL = 1024
COL_MISC = 1536
MISC_IW = 32
TAIL_WIDTH = 2048

_NT = (((1,), (1,)), ((), ()))


def _cparams(n_axes):
    return pltpu.CompilerParams(
        dimension_semantics=("arbitrary",) * n_axes,
        vmem_limit_bytes=VMEM_LIMIT_BYTES)


def _ada_body(c_ref, w_ref, b_ref, o_ref):
    c = c_ref[...]
    act = (c / (1.0 + jnp.exp(-c))).astype(BF16)
    o_ref[...] = jnp.dot(act, w_ref[...].astype(BF16),
                         preferred_element_type=F32) + b_ref[...]


def _ada(c8, w, b):
    d, n = w.shape
    tn = 512
    return pl.pallas_call(
        _ada_body,
        grid=(n // tn,),
        in_specs=[pl.BlockSpec((SUBLANES, d), lambda j: (0, 0)),
                  pl.BlockSpec((d, tn), lambda j: (0, j)),
                  pl.BlockSpec((1, tn), lambda j: (0, j))],
        out_specs=pl.BlockSpec((SUBLANES, tn), lambda j: (0, j)),
        out_shape=jax.ShapeDtypeStruct((SUBLANES, n), F32),
        compiler_params=_cparams(1),
        name="ada_proj",
    )(c8, w, b.reshape(1, n))


def _modulate_body(x_ref, sh_ref, sc_ref, o_ref):
    o_ref[...] = (x_ref[...] * (1.0 + sc_ref[0]) + sh_ref[0]).astype(BF16)


def _modulate(x2, mod3, seq, shift_idx, scale_idx):
    t, d = x2.shape
    tm = 256
    per_seq = seq // tm
    return pl.pallas_call(
        _modulate_body,
        grid=(t // tm,),
        in_specs=[pl.BlockSpec((tm, d), lambda i: (i, 0)),
                  pl.BlockSpec((1, 1, d), lambda i: (6 * (i // per_seq) + shift_idx, 0, 0)),
                  pl.BlockSpec((1, 1, d), lambda i: (6 * (i // per_seq) + scale_idx, 0, 0))],
        out_specs=pl.BlockSpec((tm, d), lambda i: (i, 0)),
        out_shape=jax.ShapeDtypeStruct((t, d), BF16),
        compiler_params=_cparams(1),
        name="modulate",
    )(x2, mod3, mod3)


def _mm2_body(a0_ref, a1_ref, w0_ref, w1_ref, o_ref):
    acc = jnp.dot(a0_ref[...], w0_ref[...].astype(BF16), preferred_element_type=F32)
    acc = acc + jnp.dot(a1_ref[...], w1_ref[...].astype(BF16), preferred_element_type=F32)
    o_ref[...] = acc.astype(o_ref.dtype)


def _matmul_concat(a0, a1, w, out_dtype, tm, tn, name):
    m, k0 = a0.shape
    k1 = a1.shape[1]
    assert k0 == k1
    n = w.shape[1]
    return pl.pallas_call(
        _mm2_body,
        grid=(m // tm, n // tn),
        in_specs=[pl.BlockSpec((tm, k0), lambda i, j: (i, 0)),
                  pl.BlockSpec((tm, k1), lambda i, j: (i, 0)),
                  pl.BlockSpec((k0, tn), lambda i, j: (0, j)),
                  pl.BlockSpec((k1, tn), lambda i, j: (1, j))],
        out_specs=pl.BlockSpec((tm, tn), lambda i, j: (i, j)),
        out_shape=jax.ShapeDtypeStruct((m, n), out_dtype),
        compiler_params=_cparams(2),
        name=name,
    )(a0, a1, w, w)


def _rms_mm_body(x_ref, g_ref, w_ref, o_ref):
    x = x_ref[...]
    ms = jnp.mean(x * x, axis=-1, keepdims=True)
    a = (x * lax.rsqrt(ms + RMS_EPS) * g_ref[...]).astype(BF16)
    o_ref[...] = jnp.dot(a, w_ref[...].astype(BF16),
                         preferred_element_type=F32).astype(o_ref.dtype)


def _rmsnorm_matmul(p, col, gain, w, tm, tn, name):
    t = p.shape[0]
    width, n = w.shape
    cb = col // width
    assert col % width == 0
    return pl.pallas_call(
        _rms_mm_body,
        grid=(t // tm, n // tn),
        in_specs=[pl.BlockSpec((tm, width), lambda i, j: (i, cb)),
                  pl.BlockSpec((1, width), lambda i, j: (0, 0)),
                  pl.BlockSpec((width, tn), lambda i, j: (0, j))],
        out_specs=pl.BlockSpec((tm, tn), lambda i, j: (i, j)),
        out_shape=jax.ShapeDtypeStruct((t, n), BF16),
        compiler_params=_cparams(2),
        name=name,
    )(p, gain.reshape(1, width), w)


def _resln_body(x_ref, y_ref, gate_ref, lg_ref, lb_ref, *rest, alpha, with_mod):
    z = alpha * x_ref[...] + (1.0 + gate_ref[0]) * y_ref[...].astype(F32)
    mu = jnp.mean(z, axis=-1, keepdims=True)
    zc = z - mu
    var = jnp.mean(zc * zc, axis=-1, keepdims=True)
    out = zc * lax.rsqrt(var + LN_EPS) * lg_ref[...] + lb_ref[...]
    if with_mod:
        sh_ref, sc_ref, o_ref, u_ref = rest
        u_ref[...] = (out * (1.0 + sc_ref[0]) + sh_ref[0]).astype(BF16)
    else:
        (o_ref,) = rest
    o_ref[...] = out


def _resln(x2, y, mod3, seq, gate_idx, ln_g, ln_b, alpha, mod_idx=None):
    t, d = x2.shape
    tm = 256
    per_seq = seq // tm
    with_mod = mod_idx is not None
    row = pl.BlockSpec((tm, d), lambda i: (i, 0))
    vec = pl.BlockSpec((1, d), lambda i: (0, 0))

    def mod_spec(idx):
        return pl.BlockSpec((1, 1, d), lambda i: (6 * (i // per_seq) + idx, 0, 0))

    in_specs = [row, row, mod_spec(gate_idx), vec, vec]
    args = [x2, y, mod3, ln_g.reshape(1, d), ln_b.reshape(1, d)]
    out_specs = row
    out_shape = jax.ShapeDtypeStruct((t, d), F32)
    if with_mod:
        in_specs += [mod_spec(mod_idx[0]), mod_spec(mod_idx[1])]
        args += [mod3, mod3]
        out_specs = [row, row]
        out_shape = [out_shape, jax.ShapeDtypeStruct((t, d), BF16)]
    return pl.pallas_call(
        functools.partial(_resln_body, alpha=alpha, with_mod=with_mod),
        grid=(t // tm,),
        in_specs=in_specs,
        out_specs=out_specs,
        out_shape=out_shape,
        compiler_params=_cparams(1),
        name="residual_ln",
    )(*args)


def _ffn_gu_body(a_ref, halo_ref, wg_ref, wu_ref, cw_ref, cb_ref, o_ref, *, tiles_per_seq):
    i = pl.program_id(0)
    a = a_ref[...]
    wg = wg_ref[...].astype(BF16)
    wu = wu_ref[...].astype(BF16)
    g = jnp.dot(a, wg, preferred_element_type=F32)
    up = jnp.dot(a, wu, preferred_element_type=F32)
    gh = jnp.dot(halo_ref[...], wg, preferred_element_type=F32)
    gh = jnp.where(i % tiles_per_seq == 0, 0.0, gh)
    prev1 = gh[SUBLANES - 1:SUBLANES, :]
    prev2 = gh[SUBLANES - 2:SUBLANES - 1, :]
    r1 = pltpu.roll(g, 1, 0)
    r2 = pltpu.roll(g, 2, 0)
    row = lax.broadcasted_iota(jnp.int32, (SUBLANES, g.shape[1]), 0)
    head1 = jnp.where(row == 0, prev1, r1[:SUBLANES])
    head2 = jnp.where(row == 0, prev2, jnp.where(row == 1, prev1, r2[:SUBLANES]))
    g1 = jnp.concatenate([head1, r1[SUBLANES:]], axis=0)
    g2 = jnp.concatenate([head2, r2[SUBLANES:]], axis=0)
    cw = cw_ref[...]
    conv = cb_ref[...] + cw[0:1, :] * g2 + cw[1:2, :] * g1 + cw[2:3, :] * g
    o_ref[...] = (conv / (1.0 + jnp.exp(-conv)) * up).astype(BF16)


def _ffn_gate_up(u, w_gate, w_up, conv_w, conv_b, seq):
    t, d = u.shape
    f = w_gate.shape[1]
    tm, tn = 1024, 256
    halo_blocks = tm // SUBLANES
    return pl.pallas_call(
        functools.partial(_ffn_gu_body, tiles_per_seq=seq // tm),
        grid=(t // tm, f // tn),
        in_specs=[pl.BlockSpec((tm, d), lambda i, j: (i, 0)),
                  pl.BlockSpec((SUBLANES, d), lambda i, j: (jnp.maximum(i * halo_blocks - 1, 0), 0)),
                  pl.BlockSpec((d, tn), lambda i, j: (0, j)),
                  pl.BlockSpec((d, tn), lambda i, j: (0, j)),
                  pl.BlockSpec((CONV_WIDTH, tn), lambda i, j: (0, j)),
                  pl.BlockSpec((1, tn), lambda i, j: (0, j))],
        out_specs=pl.BlockSpec((tm, tn), lambda i, j: (i, j)),
        out_shape=jax.ShapeDtypeStruct((t, f), BF16),
        compiler_params=_cparams(2),
        name="ffn_gate_up",
    )(u, u, w_gate, w_up, conv_w, conv_b.reshape(1, f))


def _ffn_down_body(am_ref, at_ref, wm_ref, wt_ref, o_ref, acc_ref, *, n_main, n_tail):
    k = pl.program_id(2)

    @pl.when(k == 0)
    def _():
        acc_ref[...] = jnp.zeros_like(acc_ref)

    @pl.when(k < n_main)
    def _():
        acc_ref[...] += jnp.dot(am_ref[...], wm_ref[...].astype(BF16),
                                preferred_element_type=F32)

    @pl.when(k >= n_main)
    def _():
        acc_ref[...] += jnp.dot(at_ref[...], wt_ref[...].astype(BF16),
                                preferred_element_type=F32)

    @pl.when(k == n_main + n_tail - 1)
    def _():
        o_ref[...] = acc_ref[...].astype(o_ref.dtype)


def _ffn_down(h, w_down):
    t, f = h.shape
    d = w_down.shape[1]
    tm, tn, tk, tk_tail = 2048, 1024, 1024, 256
    n_main = f // tk
    n_tail = (f - n_main * tk) // tk_tail
    assert n_main * tk + n_tail * tk_tail == f and n_tail > 0
    tail0 = n_main * tk // tk_tail

    def main_k(k):
        return jnp.minimum(k, n_main - 1)

    def tail_k(k):
        return tail0 + jnp.maximum(k - n_main, 0)

    return pl.pallas_call(
        functools.partial(_ffn_down_body, n_main=n_main, n_tail=n_tail),
        grid=(t // tm, d // tn, n_main + n_tail),
        in_specs=[pl.BlockSpec((tm, tk), lambda i, j, k: (i, main_k(k))),
                  pl.BlockSpec((tm, tk_tail), lambda i, j, k: (i, tail_k(k))),
                  pl.BlockSpec((tk, tn), lambda i, j, k: (main_k(k), j)),
                  pl.BlockSpec((tk_tail, tn), lambda i, j, k: (tail_k(k), j))],
        out_specs=pl.BlockSpec((tm, tn), lambda i, j, k: (i, j)),
        out_shape=jax.ShapeDtypeStruct((t, d), BF16),
        scratch_shapes=[pltpu.VMEM((tm, tn), F32)],
        compiler_params=_cparams(3),
        name="ffn_down",
    )(h, h, w_down, w_down)


ROPE_HALF = IDX_ROPE_DIM // 2
ROLLS_SPLIT = (LANES - ROPE_HALF, ROPE_HALF)
ROLLS_SPREAD = (LANES // 2,)


def _rope_tile(x, tabs, shifts):
    out = x * tabs[0]
    for n, shift in enumerate(shifts):
        out = out + pltpu.roll(x, shift, 1) * tabs[1 + n]
    return out


KCHUNK = 2 * Q_BLOCK
N_BIAS_TILES = 3
LOG2E = math.log2(math.e)


def _bias_tiles_body(rb_ref, o_ref):
    o = pl.program_id(0)
    tq = lax.broadcasted_iota(jnp.int32, (Q_BLOCK, KCHUNK), 0)
    sk = lax.broadcasted_iota(jnp.int32, (Q_BLOCK, KCHUNK), 1)
    n = jnp.maximum(o * Q_BLOCK + tq - sk, 0)
    max_exact = REL_BUCKETS // 2
    nf = jnp.maximum(n, 1).astype(F32)
    large = max_exact + (jnp.log(nf / max_exact) / math.log(REL_MAX_DIST / max_exact)
                         * (REL_BUCKETS - max_exact)).astype(jnp.int32)
    large = jnp.minimum(large, REL_BUCKETS - 1)
    bucket = jnp.where(n < max_exact, n, large)
    hits = [bucket == b for b in range(REL_BUCKETS - 1)]
    for h in range(A_HEADS):
        far = rb_ref[REL_BUCKETS - 1, h]
        tile = jnp.zeros((Q_BLOCK, KCHUNK), F32)
        for b in range(REL_BUCKETS - 1):
            tile = jnp.where(hits[b], (rb_ref[b, h] - far) * LOG2E, tile)
        o_ref[0, h] = tile


def _bias_tiles(rel_bias):
    return pl.pallas_call(
        _bias_tiles_body,
        grid=(N_BIAS_TILES,),
        in_specs=[pl.BlockSpec(memory_space=pltpu.SMEM)],
        out_specs=pl.BlockSpec((1, A_HEADS, Q_BLOCK, KCHUNK), lambda o: (o, 0, 0, 0)),
        out_shape=jax.ShapeDtypeStruct((N_BIAS_TILES, A_HEADS, Q_BLOCK, KCHUNK), F32),
        compiler_params=_cparams(1),
        name="rel_bias_tiles",
    )(rel_bias)


SEL_ROWS = 512
SEL_SLAB = 64


def _dsa_body(iq_ref, aq_ref, ak_ref, av_ref, ik_ref, misc_ref, rope_ref, bias_ref,
              o_ref,
              ikr_ref, vaug_ref, iq2_ref, wb_ref, st_ref, madd_ref, q2_ref,
              m_ref, l_ref, acc_ref, thr_ref,
              *, topk, seq):
    i = pl.program_id(1)
    n_chunks = i // 2 + 1
    t0 = i * Q_BLOCK
    rope = (rope_ref[0], rope_ref[1], rope_ref[2])

    @pl.when(i == 0)
    def _():
        ikr_ref[...] = jnp.zeros_like(ikr_ref)
        vaug_ref[:, :A_HEAD_DIM] = av_ref[...]
        vaug_ref[:, A_HEAD_DIM:] = jnp.ones((seq, LANES), BF16)

    ikr_ref[pl.ds(pl.multiple_of(t0, Q_BLOCK), Q_BLOCK), :] = _rope_tile(
        ik_ref[...].astype(F32), rope, ROLLS_SPLIT).astype(BF16)

    for h in range(IDX_HEADS):
        sl = slice(h * Q_BLOCK, (h + 1) * Q_BLOCK)
        iq2_ref[sl, :] = _rope_tile(iq_ref[:, sl].astype(F32), rope, ROLLS_SPLIT).astype(BF16)

    w_scale = (IDX_HEADS ** -0.5) * (IDX_HEAD_DIM ** -0.5)
    for h in range(IDX_HEADS):
        col = misc_ref[:, MISC_IW + h:MISC_IW + h + 1] * w_scale
        wb_ref[h] = jnp.broadcast_to(col, (Q_BLOCK, LANES))

    heads_per_dot = 4

    def score_chunk(jj, carry):
        k0 = pl.multiple_of(jj * KCHUNK, KCHUNK)
        kc = ikr_ref[pl.ds(k0, KCHUNK), :]
        sc = jnp.zeros((Q_BLOCK, KCHUNK), F32)
        for g in range(IDX_HEADS // heads_per_dot):
            rows = slice(g * heads_per_dot * Q_BLOCK, (g + 1) * heads_per_dot * Q_BLOCK)
            d = lax.dot_general(iq2_ref[rows, :], kc, _NT, preferred_element_type=F32)
            for hh in range(heads_per_dot):
                w = wb_ref[g * heads_per_dot + hh]
                dh = jnp.maximum(d[hh * Q_BLOCK:(hh + 1) * Q_BLOCK, :], 0.0)
                sc = sc + dh * jnp.concatenate([w, w], axis=1)
        tq = t0 + lax.broadcasted_iota(jnp.int32, sc.shape, 0)
        sk = k0 + lax.broadcasted_iota(jnp.int32, sc.shape, 1)
        sc = jnp.where(sk <= tq, sc, NEG_INF)
        st_ref[pl.ds(k0, Q_BLOCK), :] = sc[:, :Q_BLOCK].T
        st_ref[pl.ds(k0 + Q_BLOCK, Q_BLOCK), :] = sc[:, Q_BLOCK:].T
        return carry

    lax.fori_loop(0, n_chunks, score_chunk, 0)

    @pl.when(n_chunks % (SEL_ROWS // KCHUNK) == 1)
    def _():
        pad0 = pl.multiple_of(n_chunks * KCHUNK, KCHUNK)
        st_ref[pl.ds(pad0, KCHUNK), :] = jnp.full((KCHUNK, Q_BLOCK), NEG_INF, F32)

    def key_to_f32(key):
        bits = key ^ ((key >> 31) & jnp.int32(0x7FFFFFFF))
        return lax.bitcast_convert_type(bits, F32)

    kf = float(topk)

    def write_mask(k0, rows, keep_fn):
        blk = st_ref[pl.ds(k0, rows), :]
        sk = k0 + lax.broadcasted_iota(jnp.int32, blk.shape, 0)
        tq = t0 + lax.broadcasted_iota(jnp.int32, blk.shape, 1)
        keep = keep_fn(blk, sk) & (sk <= tq)
        madd_ref[pl.ds(k0, rows), :] = jnp.where(keep, 0.0, NEG_INF)

    def select_static(rows):
        def count_ge(thr_row):
            thr_b = jnp.broadcast_to(thr_row, (SEL_SLAB, LANES))
            acc = jnp.zeros((SEL_SLAB, LANES), F32)
            for r in range(rows // SEL_SLAB):
                blk = st_ref[r * SEL_SLAB:(r + 1) * SEL_SLAB, :]
                acc = jnp.where(blk >= thr_b, acc + 1.0, acc)
            acc = jnp.sum(acc.reshape(SEL_SLAB // SUBLANES, SUBLANES, LANES), axis=0)
            return jnp.sum(acc, axis=0, keepdims=True)

        def select_pass(p, key):
            cand = key + jnp.left_shift(jnp.int32(1), 31 - p)
            return jnp.where(count_ge(key_to_f32(cand)) >= kf, cand, key)

        key0 = jnp.full((1, LANES), jnp.iinfo(jnp.int32).min, jnp.int32)
        thr = key_to_f32(lax.fori_loop(0, 32, select_pass, key0))
        thr_ref[0:1, :] = thr
        thr_ref[1:2, :] = count_ge(thr)
        write_mask(0, rows, lambda blk, sk: blk >= thr)

    for n in range(1, seq // SEL_ROWS + 1):
        pl.when(i // (SEL_ROWS // Q_BLOCK) + 1 == n)(
            functools.partial(select_static, n * SEL_ROWS))

    thr = thr_ref[0:1, :]
    tie = jnp.where((thr_ref[1:2, :] > kf) & (thr > NEG_INF), 1.0, 0.0)

    @pl.when(jnp.max(tie) > 0.0)
    def _():
        def count(pred):
            def body(jj, acc):
                k0 = pl.multiple_of(jj * KCHUNK, KCHUNK)
                hit = jnp.where(pred(st_ref[pl.ds(k0, KCHUNK), :], k0), 1.0, 0.0)
                return acc + jnp.sum(hit.reshape(KCHUNK // SUBLANES, SUBLANES, LANES), axis=0)
            acc = lax.fori_loop(0, n_chunks, body, jnp.zeros((SUBLANES, LANES), F32))
            return jnp.sum(acc, axis=0, keepdims=True)

        def key_index(shape, k0):
            return k0 + lax.broadcasted_iota(jnp.int32, shape, 0)

        need = kf - count(lambda blk, k0: blk > thr)

        def index_pass(p, lo):
            cand = lo + jnp.left_shift(jnp.int32(1), 10 - p)
            n_eq = count(lambda blk, k0: (blk == thr) & (key_index(blk.shape, k0) < cand))
            return jnp.where(n_eq < need, cand, lo)

        lo = lax.fori_loop(0, 11, index_pass, jnp.zeros((1, LANES), jnp.int32))

        def rewrite(jj, carry):
            write_mask(pl.multiple_of(jj * KCHUNK, KCHUNK), KCHUNK,
                       lambda blk, sk: (blk > thr) | ((blk == thr) & (sk <= lo)))
            return carry

        lax.fori_loop(0, n_chunks, rewrite, 0)

    q_scale = (A_HEAD_DIM ** -0.5) * LOG2E
    for h in range(A_HEADS):
        sl = slice(h * Q_BLOCK, (h + 1) * Q_BLOCK)
        q2_ref[sl, :] = (aq_ref[:, sl].astype(F32) * q_scale).astype(BF16)
    m_ref[...] = jnp.full_like(m_ref, NEG_INF)
    l_ref[...] = jnp.zeros_like(l_ref)
    acc_ref[...] = jnp.zeros_like(acc_ref)

    heads_per_att = 4

    def att_chunk(jj, carry, near):
        k0 = pl.multiple_of(jj * KCHUNK, KCHUNK)
        kc = ak_ref[pl.ds(k0, KCHUNK), :]
        vc = vaug_ref[pl.ds(k0, KCHUNK), :]
        sel = jnp.concatenate([madd_ref[pl.ds(k0, Q_BLOCK), :].T,
                               madd_ref[pl.ds(k0 + Q_BLOCK, Q_BLOCK), :].T], axis=1)
        for g in range(A_HEADS // heads_per_att):
            rows = slice(g * heads_per_att * Q_BLOCK, (g + 1) * heads_per_att * Q_BLOCK)
            lg = lax.dot_general(q2_ref[rows, :], kc, _NT, preferred_element_type=F32)
            ps = []
            alphas = []
            for hh in range(heads_per_att):
                h = g * heads_per_att + hh
                hs = slice(h * Q_BLOCK, (h + 1) * Q_BLOCK)
                s = lg[hh * Q_BLOCK:(hh + 1) * Q_BLOCK, :] + sel
                if near:
                    s = s + bias_ref[i - 2 * jj, h]
                m_old = m_ref[hs, :]
                m_new = jnp.maximum(m_old, jnp.max(s, axis=1, keepdims=True))
                alphas.append(jnp.exp2(m_old - m_new))
                m_ref[hs, :] = m_new
                p = jnp.exp2(s - jnp.concatenate([m_new, m_new], axis=1))
                ps.append(p.astype(BF16))
            pv = jnp.dot(jnp.concatenate(ps, axis=0), vc, preferred_element_type=F32)
            for hh in range(heads_per_att):
                h = g * heads_per_att + hh
                hs = slice(h * Q_BLOCK, (h + 1) * Q_BLOCK)
                pv_h = pv[hh * Q_BLOCK:(hh + 1) * Q_BLOCK, :]
                acc_ref[hs, :] = alphas[hh] * acc_ref[hs, :] + pv_h[:, :A_HEAD_DIM]
                l_ref[hs, :] = alphas[hh] * l_ref[hs, :] + pv_h[:, A_HEAD_DIM:]
        return carry

    first_near = jnp.maximum(i - N_BIAS_TILES + 2, 0) // 2
    lax.fori_loop(0, first_near, functools.partial(att_chunk, near=False), 0)
    lax.fori_loop(first_near, n_chunks, functools.partial(att_chunk, near=True), 0)

    for h in range(A_HEADS):
        hs = slice(h * Q_BLOCK, (h + 1) * Q_BLOCK)
        o_ref[:, hs] = (acc_ref[hs, :] / l_ref[hs, :]).astype(BF16)


def _dsa(p, p_tail, rope_a, bias_tiles, batch, seq):
    t = p.shape[0]
    nb = seq // Q_BLOCK
    topk = min(TOPK_MAX, seq // 4)
    assert topk <= KCHUNK and seq % SEL_ROWS == 0 and seq <= 2048
    qrow = lambda b, i: b * nb + i
    stat = pltpu.VMEM((A_HEADS * Q_BLOCK, LANES), F32)
    return pl.pallas_call(
        functools.partial(_dsa_body, topk=topk, seq=seq),
        grid=(batch, nb),
        in_specs=[
            pl.BlockSpec((Q_BLOCK, IDX_HEADS * IDX_HEAD_DIM), lambda b, i: (qrow(b, i), COL_IQ // 4096)),
            pl.BlockSpec((Q_BLOCK, A_HEADS * A_HEAD_DIM), lambda b, i: (qrow(b, i), COL_AQ // 2048)),
            pl.BlockSpec((seq, LANES), lambda b, i: (b, COL_AK // LANES)),
            pl.BlockSpec((seq, LANES), lambda b, i: (b, COL_AV // LANES)),
            pl.BlockSpec((Q_BLOCK, LANES), lambda b, i: (qrow(b, i), COL_IK // LANES)),
            pl.BlockSpec((Q_BLOCK, LANES), lambda b, i: (qrow(b, i), COL_MISC // LANES)),
            pl.BlockSpec((3, Q_BLOCK, LANES), lambda b, i: (0, qrow(b, i), 0)),
            pl.BlockSpec(bias_tiles.shape, lambda b, i: (0, 0, 0, 0)),
        ],
        out_specs=pl.BlockSpec((Q_BLOCK, A_HEADS * A_HEAD_DIM), lambda b, i: (qrow(b, i), 0)),
        out_shape=jax.ShapeDtypeStruct((t, A_HEADS * A_HEAD_DIM), BF16),
        scratch_shapes=[
            pltpu.VMEM((seq, IDX_HEAD_DIM), BF16),
            pltpu.VMEM((seq, 2 * LANES), BF16),
            pltpu.VMEM((IDX_HEADS * Q_BLOCK, IDX_HEAD_DIM), BF16),
            pltpu.VMEM((IDX_HEADS, Q_BLOCK, LANES), F32),
            pltpu.VMEM((seq, Q_BLOCK), F32),
            pltpu.VMEM((seq, Q_BLOCK), F32),
            pltpu.VMEM((A_HEADS * Q_BLOCK, A_HEAD_DIM), BF16),
            stat, stat, stat,
            pltpu.VMEM((SUBLANES, LANES), F32),
        ],
        compiler_params=_cparams(2),
        name="dsa_attention",
    )(p, p, p, p, p, p_tail, rope_a, bias_tiles)


MLA_BLOCK = 512
MLA_HEADS_PER_STEP = 4


def _mla_body(q_ref, kv_ref, misc_ref, ropeq_ref, ropek_ref, o_ref, krr_ref, kc_ref, vaug_ref,
              qc_ref, m_ref, l_ref, acc_ref):
    hp = pl.program_id(1)
    qi = pl.program_id(2)
    head_w = QK_NOPE_DIM + LANES
    seq = kv_ref.shape[0]

    @pl.when((hp == 0) & (qi == 0))
    def _():
        krr_ref[...] = _rope_tile(misc_ref[...], (ropek_ref[0], ropek_ref[1]),
                                  ROLLS_SPREAD).astype(BF16)

    @pl.when(qi == 0)
    def _():
        for e in range(MLA_HEADS_PER_STEP):
            c0 = e * head_w
            kc_ref[e, :, :QK_NOPE_DIM] = kv_ref[:, c0:c0 + QK_NOPE_DIM]
            kc_ref[e, :, QK_NOPE_DIM:] = krr_ref[...]
            vaug_ref[e, :, :V_HEAD_DIM] = kv_ref[:, c0 + QK_NOPE_DIM:c0 + head_w]
            vaug_ref[e, :, V_HEAD_DIM:] = jnp.ones((seq, LANES), BF16)

    scale = (QK_NOPE_DIM + QK_ROPE_DIM) ** -0.5 * LOG2E
    for e in range(MLA_HEADS_PER_STEP):
        c0 = e * head_w
        q_rope = _rope_tile(q_ref[:, c0 + QK_NOPE_DIM:c0 + head_w].astype(F32),
                            (ropeq_ref[0], ropeq_ref[1]), ROLLS_SPREAD)
        qc = jnp.concatenate([q_ref[:, c0:c0 + QK_NOPE_DIM].astype(F32), q_rope], axis=1) * scale
        qc_ref[e] = qc.astype(BF16)
    m_ref[...] = jnp.full_like(m_ref, NEG_INF)
    l_ref[...] = jnp.zeros_like(l_ref)
    acc_ref[...] = jnp.zeros_like(acc_ref)
    lane_tiles = MLA_BLOCK // LANES

    def step(j, carry, masked):
        k0 = pl.multiple_of(j * MLA_BLOCK, MLA_BLOCK)
        for e in range(MLA_HEADS_PER_STEP):
            s = lax.dot_general(qc_ref[e], kc_ref[e, pl.ds(k0, MLA_BLOCK), :], _NT,
                                preferred_element_type=F32)
            if masked:
                tq = lax.broadcasted_iota(jnp.int32, s.shape, 0)
                sk = lax.broadcasted_iota(jnp.int32, s.shape, 1)
                s = jnp.where(sk <= tq, s, NEG_INF)
            m_old = m_ref[e]
            m_new = jnp.maximum(m_old, jnp.max(s, axis=1, keepdims=True))
            alpha = jnp.exp2(m_old - m_new)
            m_ref[e] = m_new
            p = jnp.exp2(s - jnp.concatenate([m_new] * lane_tiles, axis=1))
            pv = jnp.dot(p.astype(BF16), vaug_ref[e, pl.ds(k0, MLA_BLOCK), :],
                         preferred_element_type=F32)
            l_ref[e] = alpha * l_ref[e] + pv[:, V_HEAD_DIM:]
            acc_ref[e] = alpha * acc_ref[e] + pv[:, :V_HEAD_DIM]
        return carry

    lax.fori_loop(0, qi, functools.partial(step, masked=False), 0)
    step(qi, 0, masked=True)
    for e in range(MLA_HEADS_PER_STEP):
        o_ref[:, e * V_HEAD_DIM:(e + 1) * V_HEAD_DIM] = (acc_ref[e] / l_ref[e]).astype(BF16)


def _mla(q, kv, p, rope_b, batch, seq):
    t = q.shape[0]
    nq = seq // MLA_BLOCK
    head_w = QK_NOPE_DIM + LANES
    step_w = MLA_HEADS_PER_STEP * head_w
    qrow = lambda b, hp, qi: b * nq + qi
    return pl.pallas_call(
        _mla_body,
        grid=(batch, B_HEADS // MLA_HEADS_PER_STEP, nq),
        in_specs=[
            pl.BlockSpec((MLA_BLOCK, step_w), lambda b, hp, qi: (qrow(b, hp, qi), hp)),
            pl.BlockSpec((seq, step_w), lambda b, hp, qi: (b, hp)),
            pl.BlockSpec((seq, LANES), lambda b, hp, qi: (b, COL_MISC // LANES)),
            pl.BlockSpec((2, MLA_BLOCK, LANES), lambda b, hp, qi: (0, qrow(b, hp, qi), 0)),
            pl.BlockSpec((2, seq, LANES), lambda b, hp, qi: (0, b, 0)),
        ],
        out_specs=pl.BlockSpec((MLA_BLOCK, MLA_HEADS_PER_STEP * V_HEAD_DIM),
                               lambda b, hp, qi: (qrow(b, hp, qi), hp)),
        out_shape=jax.ShapeDtypeStruct((t, B_HEADS * V_HEAD_DIM), BF16),
        scratch_shapes=[
            pltpu.VMEM((seq, LANES), BF16),
            pltpu.VMEM((MLA_HEADS_PER_STEP, seq, head_w), BF16),
            pltpu.VMEM((MLA_HEADS_PER_STEP, seq, head_w), BF16),
            pltpu.VMEM((MLA_HEADS_PER_STEP, MLA_BLOCK, head_w), BF16),
            pltpu.VMEM((MLA_HEADS_PER_STEP, MLA_BLOCK, LANES), F32),
            pltpu.VMEM((MLA_HEADS_PER_STEP, MLA_BLOCK, LANES), F32),
            pltpu.VMEM((MLA_HEADS_PER_STEP, MLA_BLOCK, V_HEAD_DIM), F32)],
        compiler_params=_cparams(3),
        name="mla_attention",
    )(q, kv, p, rope_b, rope_b)


def _rope_tables(positions):
    freqs = ROPE_THETA ** (-jnp.arange(ROPE_HALF, dtype=F32) / ROPE_HALF)
    ang = positions.astype(F32).reshape(-1, 1) * freqs
    cos, sin = jnp.cos(ang), jnp.sin(ang)
    one, zero = jnp.ones_like(cos), jnp.zeros_like(cos)
    cat = lambda parts: jnp.concatenate(parts, axis=1)
    rope_a = jnp.stack([cat([one, one, cos, cos]), cat([zero, zero, -sin, zero]),
                        cat([zero, zero, zero, sin])])
    rope_b = jnp.stack([cat([cos, zero, cos, zero]), cat([-sin, zero, sin, zero])])
    return rope_a, rope_b


IN_SPLITS = (A_HEADS * A_HEAD_DIM, A_HEAD_DIM, A_HEAD_DIM, IDX_HEADS * IDX_HEAD_DIM,
             IDX_HEAD_DIM, IDX_HEADS, Q_LORA_RANK, KV_LORA_RANK, QK_ROPE_DIM)
IN_OFFS = [int(v) for v in np.concatenate([[0], np.cumsum(IN_SPLITS)])]


def _in_proj_tail_weight(wt):
    iw, ql, kvl, kr = [wt[IN_OFFS[n]:IN_OFFS[n + 1]] for n in range(5, 9)]
    g = QK_ROPE_DIM // 2
    pad = jnp.zeros((TAIL_WIDTH - Q_LORA_RANK - KV_LORA_RANK - QK_ROPE_DIM - IDX_HEADS,
                     wt.shape[1]), wt.dtype)
    out = jnp.concatenate([ql, kvl, kr[:g], iw, kr[g:], pad], axis=0)
    assert out.shape[0] == TAIL_WIDTH and MISC_IW == g
    assert (COL_QL, COL_KVL, COL_MISC) == (0, Q_LORA_RANK, Q_LORA_RANK + KV_LORA_RANK)
    return out


def _in_proj_step_plan():
    aq0, ak0, iq0, ik0 = (IN_OFFS[n] // IN_TILE for n in (0, 1, 3, 4))
    assert all(IN_OFFS[n] % IN_TILE == 0 for n in (0, 1, 3, 4))
    pairs = lambda t0, n: [(t0 + 2 * k, t0 + 2 * k + 1) for k in range(n // IN_STEP)]
    plan = pairs(iq0, IN_SPLITS[3]) + pairs(aq0, IN_SPLITS[0]) + [(ak0, ik0)]
    assert len(plan) * IN_STEP == MAIN_WIDTH
    as_i32 = lambda v: jnp.asarray(np.asarray(v, np.int32))
    return as_i32([p[0] for p in plan]), as_i32([p[1] for p in plan])


def _in_proj_body(ia_ref, ib_ref, a_ref, wa_ref, wb_ref, o_ref):
    a = a_ref[...]
    o_ref[:, :IN_TILE] = lax.dot_general(a, wa_ref[...].astype(BF16), _NT,
                                         preferred_element_type=F32).astype(o_ref.dtype)
    o_ref[:, IN_TILE:] = lax.dot_general(a, wb_ref[...].astype(BF16), _NT,
                                         preferred_element_type=F32).astype(o_ref.dtype)


def _in_proj_main(u, wt):
    t, d = u.shape
    tm = 1024
    idx_a, idx_b = _in_proj_step_plan()
    grid_spec = pltpu.PrefetchScalarGridSpec(
        num_scalar_prefetch=2,
        grid=(t // tm, MAIN_WIDTH // IN_STEP),
        in_specs=[pl.BlockSpec((tm, d), lambda i, j, ia, ib: (i, 0)),
                  pl.BlockSpec((IN_TILE, d), lambda i, j, ia, ib: (ia[j], 0)),
                  pl.BlockSpec((IN_TILE, d), lambda i, j, ia, ib: (ib[j], 0))],
        out_specs=pl.BlockSpec((tm, IN_STEP), lambda i, j, ia, ib: (i, j)),
    )
    return pl.pallas_call(
        _in_proj_body,
        grid_spec=grid_spec,
        out_shape=jax.ShapeDtypeStruct((t, MAIN_WIDTH), BF16),
        compiler_params=_cparams(2),
        name="in_proj",
    )(idx_a, idx_b, u, wt, wt)


def _mm_nt_body(a_ref, w_ref, o_ref):
    o_ref[...] = lax.dot_general(a_ref[...], w_ref[...].astype(BF16), _NT,
                                 preferred_element_type=F32).astype(o_ref.dtype)


def _matmul_nt(a, wt, out_dtype, tm, tn, name):
    m, k = a.shape
    n = wt.shape[0]
    return pl.pallas_call(
        _mm_nt_body,
        grid=(m // tm, n // tn),
        in_specs=[pl.BlockSpec((tm, k), lambda i, j: (i, 0)),
                  pl.BlockSpec((tn, k), lambda i, j: (j, 0))],
        out_specs=pl.BlockSpec((tm, tn), lambda i, j: (i, j)),
        out_shape=jax.ShapeDtypeStruct((m, n), out_dtype),
        compiler_params=_cparams(2),
        name=name,
    )(a, wt)


def _reorder_w_uq(w):
    r = w.shape[0]
    g = QK_ROPE_DIM // 2
    w3 = w.reshape(r, B_HEADS, QK_NOPE_DIM + QK_ROPE_DIM)
    zero = jnp.zeros((r, B_HEADS, g), w.dtype)
    w3 = jnp.concatenate([w3[:, :, :QK_NOPE_DIM], w3[:, :, QK_NOPE_DIM:QK_NOPE_DIM + g], zero,
                          w3[:, :, QK_NOPE_DIM + g:], zero], axis=2)
    return w3.reshape(r, B_HEADS * 2 * LANES)


def kernel(x, c, positions, w_ada, b_ada, w_in, rel_bias, q_norm_g, w_uq, kv_norm_g, w_ukv,
           w_o, ln1_g, ln1_b, w_gate, w_up, conv_w, conv_b, w_down, ln2_g, ln2_b):
    batch, seq, d = x.shape
    depth = w_ada.shape[0]
    t = batch * seq
    assert d == D_MODEL and batch <= SUBLANES and seq % 1024 == 0
    alpha = (2 * depth) ** 0.25

    rope_a, rope_b = _rope_tables(positions)
    bias_tiles = _bias_tiles(rel_bias)
    c8 = jnp.zeros((SUBLANES, d), F32).at[:batch].set(c)
    x2 = x.reshape(t, d)

    for l in range(depth):
        mod = _ada(c8, w_ada[l], b_ada[l])
        mod3 = mod[:batch].reshape(batch * 6, 1, d)

        u = _modulate(x2, mod3, seq, shift_idx=0, scale_idx=1)
        w_in_t = jnp.swapaxes(w_in[l], 0, 1)
        p = _in_proj_main(u, w_in_t)
        p_tail = _matmul_nt(u, _in_proj_tail_weight(w_in_t), F32, 1024, 512, "in_proj_tail")
        y_a = _dsa(p, p_tail, rope_a, bias_tiles, batch, seq)
        q = _rmsnorm_matmul(p_tail, COL_QL, q_norm_g[l], _reorder_w_uq(w_uq[l]),
                            1024, 2048, "q_up_proj")
        kv = _rmsnorm_matmul(p_tail, COL_KVL, kv_norm_g[l], w_ukv[l], 1024, 4096, "kv_up_proj")
        y_b = _mla(q, kv, p_tail, rope_b, batch, seq)
        mix = _matmul_concat(y_a, y_b, w_o[l], BF16, 1024, 512, "out_proj")
        x2, u = _resln(x2, mix, mod3, seq, 2, ln1_g[l], ln1_b[l], alpha, mod_idx=(3, 4))

        hidden = _ffn_gate_up(u, w_gate[l], w_up[l], conv_w[l], conv_b[l], seq)
        y = _ffn_down(hidden, w_down[l])
        x2 = _resln(x2, y, mod3, seq, 5, ln2_g[l], ln2_b[l], alpha)

    return x2.reshape(batch, seq, d)
```
